```python
import jax, jax.numpy as jnp
from jax import lax
import numpy as np

D_MODEL = 1024
BATCH = 8
SEQ = 2048
DEPTH = 1
DEC_BATCH = 128
DEC_SEQ = 8
PAST_LEN = 16384
PAGE_SIZE = 128

MIX_WIDTH = D_MODEL
C_CONV = MIX_WIDTH // 2
C_RWKV = MIX_WIDTH - C_CONV
HEAD_DIM = 64
N_HEADS = C_RWKV // HEAD_DIM
CONV_WIDTH = 31
D_DECAY_LORA = 64
D_AAA_LORA = 64
D_GATE_LORA = 128
D_SHIFT = 3 * C_RWKV + D_DECAY_LORA + D_AAA_LORA + D_GATE_LORA
D_IN = 2 * C_CONV + D_SHIFT
N_EXPERT_GROUPS = 4
EXPERTS_PER_GROUP = 8
TOP_K = 2
D_EXPERT = 256
RMS_EPS = 1e-6
LN_EPS = 1e-5
GN_EPS = 64e-5
RWKV_SPLITS = (C_RWKV, 2 * C_RWKV, 3 * C_RWKV, 3 * C_RWKV + D_DECAY_LORA,
               3 * C_RWKV + D_DECAY_LORA + D_AAA_LORA)

kernel_name = "hymba_conformer_rwkv7_hiermoe_step"


def rmsnorm(x, g):
    xf = x.astype(jnp.float32)
    y = xf * lax.rsqrt(jnp.mean(xf * xf, axis=-1, keepdims=True) + RMS_EPS)
    return (y * g.astype(jnp.float32)).astype(x.dtype)


def conv_mixer(u, conv_buf, conv_w, conv_b, ln_g, ln_b):
    val, gate = jnp.split(u, 2, axis=-1)
    h = val * jax.nn.sigmoid(gate)
    hx = jnp.concatenate([conv_buf.astype(h.dtype), h], axis=1)
    out = lax.conv_general_dilated(
        hx, conv_w.astype(hx.dtype)[:, None, :], window_strides=(1,), padding='VALID',
        dimension_numbers=('NWC', 'WIO', 'NWC'), feature_group_count=C_CONV) + conv_b
    new_buf = hx[:, -(CONV_WIDTH - 1):]
    of = out.astype(jnp.float32)
    mu = jnp.mean(of, axis=-1, keepdims=True)
    var = jnp.mean(jnp.square(of - mu), axis=-1, keepdims=True)
    of = (of - mu) * lax.rsqrt(var + LN_EPS) * ln_g.astype(jnp.float32) + ln_b.astype(jnp.float32)
    return jax.nn.silu(of).astype(u.dtype), new_buf


def wkv_step(S, inp):
    r_t, w_t, k_t, v_t, kk_t, a_t = inp
    sa = jnp.einsum('bhvk,bhk->bhv', S, -kk_t)
    S = (S * w_t[:, :, None, :] + sa[..., None] * (kk_t * a_t)[:, :, None, :]
         + v_t[..., None] * k_t[:, :, None, :])
    y = jnp.einsum('bhvk,bhk->bhv', S, r_t)
    return S, y


def rwkv_mixer(z, shift_buf, wkv_state, mu_shift, w0, w_decay_up, a0, w_aaa_up, w_gate_up,
               k_k, k_a, r_k, gn_g, gn_b):
    B, T, _ = z.shape
    f32 = jnp.float32
    z_prev = jnp.concatenate([shift_buf[:, None, :].astype(z.dtype), z[:, :-1]], axis=1)
    zs = z + (z_prev - z) * mu_shift
    new_shift = z[:, -1]
    r, k, v, wl, al, gl = jnp.split(zs, RWKV_SPLITS, axis=-1)
    w_log = -jax.nn.softplus(-(w0 + jnp.tanh(wl) @ w_decay_up).astype(f32)) - 0.5
    decay = jnp.exp(-jnp.exp(w_log))
    a = jax.nn.sigmoid((a0 + al @ w_aaa_up).astype(f32))
    g = jax.nn.sigmoid(gl) @ w_gate_up

    def heads(t):
        return t.astype(f32).reshape(B, T, N_HEADS, HEAD_DIM)

    kk = heads(k * k_k)
    kk = kk / jnp.maximum(jnp.sqrt(jnp.sum(kk * kk, axis=-1, keepdims=True)), 1e-12)
    k_mod = k.astype(f32) * (1.0 + (a - 1.0) * k_a.astype(f32))
    r_h, k_h, v_h, w_h, a_h = heads(r), heads(k_mod), heads(v), heads(decay), heads(a)
    xs = tuple(jnp.moveaxis(t, 1, 0) for t in (r_h, w_h, k_h, v_h, kk, a_h))
    S_final, y = lax.scan(wkv_step, wkv_state.astype(f32), xs)
    y = jnp.moveaxis(y, 0, 1)
    mu = jnp.mean(y, axis=-1, keepdims=True)
    var = jnp.mean(jnp.square(y - mu), axis=-1, keepdims=True)
    y_n = ((y - mu) * lax.rsqrt(var + GN_EPS)).reshape(B, T, C_RWKV)
    y_n = y_n * gn_g.astype(f32) + gn_b.astype(f32)
    bonus = jnp.sum(r_h * k_h * r_k.astype(f32).reshape(N_HEADS, HEAD_DIM), axis=-1,
                    keepdims=True) * v_h
    out = (y_n + bonus.reshape(B, T, C_RWKV)) * g.astype(f32)
    return out.astype(z.dtype), new_shift, S_final.astype(wkv_state.dtype)


def hier_moe(x, w_rg, b_rg, w_re, b_re, w_eg, w_eu, w_ed):
    B, T, D = x.shape
    xt = x.reshape(-1, D)
    g_logits = (xt @ w_rg + b_rg).astype(jnp.float32)
    g_prob = jax.nn.softmax(g_logits, axis=-1)
    g_idx = jnp.argmax(g_logits, axis=-1)
    e_logits = (xt @ w_re + b_re).astype(jnp.float32).reshape(-1, N_EXPERT_GROUPS, EXPERTS_PER_GROUP)
    e_in = jnp.take_along_axis(e_logits, g_idx[:, None, None], axis=1)[:, 0]
    top_vals, top_idx = lax.top_k(e_in, TOP_K)
    p_group = jnp.take_along_axis(g_prob, g_idx[:, None], axis=1)
    top_w = jax.nn.softmax(top_vals, axis=-1) * p_group
    comb_e = jnp.sum(jax.nn.one_hot(top_idx, EXPERTS_PER_GROUP, dtype=jnp.float32)
                     * top_w[..., None], axis=1)
    combine = comb_e[:, None, :] * jax.nn.one_hot(g_idx, N_EXPERT_GROUPS, dtype=jnp.float32)[:, :, None]
    y = jnp.zeros_like(xt)
    for gi in range(N_EXPERT_GROUPS):
        hg = jnp.einsum('nd,edf->nef', xt, w_eg[gi])
        hu = jnp.einsum('nd,edf->nef', xt, w_eu[gi])
        h = jax.nn.silu(hg) * hu * combine[:, gi, :, None].astype(xt.dtype)
        y = y + jnp.einsum('nef,efd->nd', h, w_ed[gi])
    return y.reshape(B, T, D)


def layer(x, conv_buf, shift_buf, wkv_state, lp):
    (norm1_g, w_in, conv_w, conv_b, conv_ln_g, conv_ln_b, mu_shift, w0, w_decay_up, a0,
     w_aaa_up, w_gate_up, k_k, k_a, r_k, gn_g, gn_b, w_out, norm2_g, w_router_group,
     b_router_group, w_router_expert, b_router_expert, w_exp_gate, w_exp_up, w_exp_down) = lp
    xn = rmsnorm(x, norm1_g)
    proj = xn @ w_in
    conv_out, new_conv = conv_mixer(proj[..., :2 * C_CONV], conv_buf, conv_w, conv_b,
                                    conv_ln_g, conv_ln_b)
    rwkv_out, new_shift, new_wkv = rwkv_mixer(proj[..., 2 * C_CONV:], shift_buf, wkv_state,
                                              mu_shift, w0, w_decay_up, a0, w_aaa_up,
                                              w_gate_up, k_k, k_a, r_k, gn_g, gn_b)
    h = x + jnp.concatenate([conv_out, rwkv_out], axis=-1) @ w_out
    h = h + hier_moe(rmsnorm(h, norm2_g), w_router_group, b_router_group, w_router_expert,
                     b_router_expert, w_exp_gate, w_exp_up, w_exp_down)
    return h, new_conv, new_shift, new_wkv


def setup_inputs(seed: int = 0) -> dict:
    key = jax.random.key(seed)
    ks = iter(jax.random.split(key, 40))
    f32 = jnp.float32

    def nrm(shape, scale):
        return jax.random.normal(next(ks), shape, f32) * scale

    L, G, E = DEPTH, N_EXPERT_GROUPS, EXPERTS_PER_GROUP
    return {
        "x_prompt": nrm((BATCH, SEQ, D_MODEL), 1.0),
        "x_sample": nrm((DEC_BATCH, DEC_SEQ, D_MODEL), 1.0),
        "state_conv": nrm((L, DEC_BATCH, CONV_WIDTH - 1, C_CONV), 0.5),
        "state_shift": nrm((L, DEC_BATCH, D_SHIFT), 1.0),
        "state_wkv": nrm((L, DEC_BATCH, N_HEADS, HEAD_DIM, HEAD_DIM), 0.1),
        "norm1_g": 1.0 + nrm((L, D_MODEL), 0.02),
        "w_in": nrm((L, D_MODEL, D_IN), D_MODEL ** -0.5),
        "conv_w": nrm((L, CONV_WIDTH, C_CONV), CONV_WIDTH ** -0.5),
        "conv_b": nrm((L, C_CONV), 0.02),
        "conv_ln_g": 1.0 + nrm((L, C_CONV), 0.02),
        "conv_ln_b": nrm((L, C_CONV), 0.02),
        "mu_shift": jax.random.uniform(next(ks), (L, D_SHIFT), f32),
        "w0": jax.random.uniform(next(ks), (L, C_RWKV), f32, -5.0, 1.0),
        "w_decay_up": nrm((L, D_DECAY_LORA, C_RWKV), 0.5 * D_DECAY_LORA ** -0.5),
        "a0": nrm((L, C_RWKV), 0.1),
        "w_aaa_up": nrm((L, D_AAA_LORA, C_RWKV), 0.5 * D_AAA_LORA ** -0.5),
        "w_gate_up": nrm((L, D_GATE_LORA, C_RWKV), D_GATE_LORA ** -0.5),
        "k_k": 0.85 + nrm((L, C_RWKV), 0.02),
        "k_a": 1.0 + nrm((L, C_RWKV), 0.02),
        "r_k": nrm((L, C_RWKV), 0.1),
        "gn_g": 1.0 + nrm((L, C_RWKV), 0.02),
        "gn_b": nrm((L, C_RWKV), 0.02),
        "w_out": nrm((L, MIX_WIDTH, D_MODEL), MIX_WIDTH ** -0.5),
        "norm2_g": 1.0 + nrm((L, D_MODEL), 0.02),
        "w_router_group": nrm((L, D_MODEL, G), D_MODEL ** -0.5),
        "b_router_group": nrm((L, G), 0.01),
        "w_router_expert": nrm((L, D_MODEL, G * E), D_MODEL ** -0.5),
        "b_router_expert": nrm((L, G * E), 0.01),
        "w_exp_gate": nrm((L, G, E, D_MODEL, D_EXPERT), D_MODEL ** -0.5),
        "w_exp_up": nrm((L, G, E, D_MODEL, D_EXPERT), D_MODEL ** -0.5),
        "w_exp_down": nrm((L, G, E, D_EXPERT, D_MODEL), D_EXPERT ** -0.5),
        "norm_f_g": 1.0 + nrm((D_MODEL,), 0.02),
    }


def reference(x_prompt, x_sample, state_conv, state_shift, state_wkv, norm1_g, w_in, conv_w,
              conv_b, conv_ln_g, conv_ln_b, mu_shift, w0, w_decay_up, a0, w_aaa_up, w_gate_up,
              k_k, k_a, r_k, gn_g, gn_b, w_out, norm2_g, w_router_group, b_router_group,
              w_router_expert, b_router_expert, w_exp_gate, w_exp_up, w_exp_down, norm_f_g):
    B = x_prompt.shape[0]
    dt = x_prompt.dtype
    h_p, h_s = x_prompt, x_sample
    conv_p, shift_p, wkv_p, conv_s, shift_s, wkv_s = [], [], [], [], [], []
    for l in range(DEPTH):
        lp = (norm1_g[l], w_in[l], conv_w[l], conv_b[l], conv_ln_g[l], conv_ln_b[l], mu_shift[l],
              w0[l], w_decay_up[l], a0[l], w_aaa_up[l], w_gate_up[l], k_k[l], k_a[l], r_k[l],
              gn_g[l], gn_b[l], w_out[l], norm2_g[l], w_router_group[l], b_router_group[l],
              w_router_expert[l], b_router_expert[l], w_exp_gate[l], w_exp_up[l], w_exp_down[l])
        h_p, c, s, w = layer(h_p,
                             jnp.zeros((B, CONV_WIDTH - 1, C_CONV), dt),
                             jnp.zeros((B, D_SHIFT), dt),
                             jnp.zeros((B, N_HEADS, HEAD_DIM, HEAD_DIM), dt), lp)
        conv_p.append(c); shift_p.append(s); wkv_p.append(w)
        h_s, c, s, w = layer(h_s, state_conv[l], state_shift[l], state_wkv[l], lp)
        conv_s.append(c); shift_s.append(s); wkv_s.append(w)
    y_prompt = rmsnorm(h_p, norm_f_g)
    y_sample = rmsnorm(h_s, norm_f_g)
    return (y_prompt, y_sample, jnp.stack(conv_p), jnp.stack(shift_p), jnp.stack(wkv_p),
            jnp.stack(conv_s), jnp.stack(shift_s), jnp.stack(wkv_s))
```

```python
import functools

import jax
import jax.numpy as jnp
from jax import lax
from jax.experimental import pallas as pl
from jax.experimental.pallas import tpu as pltpu

F32 = jnp.float32
BF16 = jnp.bfloat16

D_MODEL = 1024
C_CONV = 512
C_RWKV = 512
HEAD_DIM = 64
N_HEADS = 8
CONV_WIDTH = 31
D_SHIFT = 1792
N_EXPERT_GROUPS = 4
EXPERTS_PER_GROUP = 8
N_EXPERTS = N_EXPERT_GROUPS * EXPERTS_PER_GROUP
D_EXPERT = 256
RMS_EPS = 1e-6
LN_EPS = 1e-5
GN_EPS = 64e-5

LANES = 128
SUBLANES = 8
CONV_PAD = 32
BH_PER_GROUP = 64
K_HALF = HEAD_DIM // 2
ROUTE_COLS = LANES
EXPERT_COL0 = N_EXPERT_GROUPS
VMEM_LIMIT = 56 * 1024 * 1024


def _cparams(*sem):
    return pltpu.CompilerParams(dimension_semantics=sem, vmem_limit_bytes=VMEM_LIMIT)


def _split_bf16(x):
    hi = x.astype(BF16)
    lo = (x - hi.astype(F32)).astype(BF16)
    return hi, lo


def _segsum(x, ones_bd):
    hi, lo = _split_bf16(x)
    return (jnp.dot(hi, ones_bd, preferred_element_type=F32)
            + jnp.dot(lo, ones_bd, preferred_element_type=F32))


def _inproj_kernel(x_ref, g_ref, w_ref, h_ref, z_ref):
    x = x_ref[...]
    xn = x * lax.rsqrt(jnp.mean(x * x, axis=-1, keepdims=True) + RMS_EPS) * g_ref[...]
    p = jnp.dot(xn.astype(BF16), w_ref[...], preferred_element_type=F32)
    h_ref[...] = p[:, :C_CONV] * jax.nn.sigmoid(p[:, C_CONV:2 * C_CONV])
    z_ref[...] = p[:, 2 * C_CONV:]


def _inproj(x, g, w_bf, tm):
    n = x.shape[0]
    d_in = w_bf.shape[1]
    return pl.pallas_call(
        _inproj_kernel,
        grid=(n // tm,),
        in_specs=[pl.BlockSpec((tm, D_MODEL), lambda i: (i, 0)),
                  pl.BlockSpec((1, D_MODEL), lambda i: (0, 0)),
                  pl.BlockSpec((D_MODEL, d_in), lambda i: (0, 0))],
        out_specs=[pl.BlockSpec((tm, C_CONV), lambda i: (i, 0)),
                   pl.BlockSpec((tm, D_SHIFT), lambda i: (i, 0))],
        out_shape=[jax.ShapeDtypeStruct((n, C_CONV), F32),
                   jax.ShapeDtypeStruct((n, D_SHIFT), F32)],
        compiler_params=_cparams("parallel"),
        name="inproj",
    )(x, g, w_bf)


def _conv_kernel(hx_ref, w_ref, b_ref, lg_ref, lb_ref, o_ref, win_ref, *, rows, n_chunks, n_seq):
    def seq_body(s, _):
        def chunk_body(c, _):
            t0 = pl.multiple_of(c * rows, SUBLANES)
            win_ref[...] = hx_ref[s, pl.ds(t0, rows + CONV_PAD), :]
            acc = jnp.zeros((rows, C_CONV), F32) + b_ref[...]
            for j in range(CONV_WIDTH):
                acc = acc + win_ref[pl.ds(j + 2, rows), :] * w_ref[pl.ds(j, 1), :]
            mu = jnp.mean(acc, axis=-1, keepdims=True)
            d = acc - mu
            var = jnp.mean(d * d, axis=-1, keepdims=True)
            of = d * lax.rsqrt(var + LN_EPS) * lg_ref[...] + lb_ref[...]
            o_ref[s, pl.ds(t0, rows), :] = (of * jax.nn.sigmoid(of)).astype(o_ref.dtype)
            return 0
        lax.fori_loop(0, n_chunks, chunk_body, 0)
        return 0
    lax.fori_loop(0, n_seq, seq_body, 0)


def _conv(hx, conv_w, conv_b, ln_g, ln_b, n_seq, rows):
    b, tp, _ = hx.shape
    t = tp - CONV_PAD
    kern = functools.partial(_conv_kernel, rows=rows, n_chunks=t // rows, n_seq=n_seq)
    vec = pl.BlockSpec((1, C_CONV), lambda i: (0, 0))
    return pl.pallas_call(
        kern,
        grid=(b // n_seq,),
        in_specs=[pl.BlockSpec((n_seq, tp, C_CONV), lambda i: (i, 0, 0)),
                  pl.BlockSpec((CONV_WIDTH, C_CONV), lambda i: (0, 0)),
                  vec, vec, vec],
        out_specs=pl.BlockSpec((n_seq, t, C_CONV), lambda i: (i, 0, 0)),
        out_shape=jax.ShapeDtypeStruct((b, t, C_CONV), BF16),
        scratch_shapes=[pltpu.VMEM((rows + CONV_PAD, C_CONV), F32)],
        compiler_params=_cparams("parallel"),
        name="conv",
    )(hx, conv_w, conv_b, ln_g, ln_b)


def _prep_kernel(z_ref, bnd_ref, mu_ref, w0_ref, wd_ref, a0_ref, wa_ref, wg_ref, kk_ref, ka_ref,
                 rk_ref, ones_ref,
                 r_ref, w_ref, k_ref, v_ref, nkk_ref, b_ref, bonus_ref, g_ref, *, period):
    z = z_ref[...]
    rows = z.shape[0]
    row_id = lax.broadcasted_iota(jnp.int32, z.shape, 0)
    z_prev = jnp.where(row_id % period == 0, bnd_ref[...], pltpu.roll(z, 1, axis=0))
    zs = z + (z_prev - z) * mu_ref[...]
    r = zs[:, :C_RWKV]
    k = zs[:, C_RWKV:2 * C_RWKV]
    v = zs[:, 2 * C_RWKV:3 * C_RWKV]
    wa = zs[:, 3 * C_RWKV:3 * C_RWKV + LANES]
    gl = zs[:, 3 * C_RWKV + LANES:]
    ones_bd = ones_ref[...]

    lw = w0_ref[...] + jnp.dot(jnp.tanh(wa).astype(BF16), wd_ref[...], preferred_element_type=F32)
    w_log = -(jnp.maximum(-lw, 0.0) + jnp.log(1.0 + jnp.exp(-jnp.abs(lw)))) - 0.5
    decay = jnp.exp(-jnp.exp(w_log))
    a = jax.nn.sigmoid(a0_ref[...] + jnp.dot(wa.astype(BF16), wa_ref[...],
                                             preferred_element_type=F32))
    g = jnp.dot(jax.nn.sigmoid(gl).astype(BF16), wg_ref[...], preferred_element_type=F32)

    kk = k * kk_ref[...]
    norm = jnp.maximum(jnp.sqrt(_segsum(kk * kk, ones_bd)), 1e-12)
    kk = kk / norm
    k_mod = k * (1.0 + (a - 1.0) * ka_ref[...])
    bonus = _segsum(r * k_mod * rk_ref[...], ones_bd) * v

    r_ref[...] = r
    w_ref[...] = decay
    k_ref[...] = k_mod
    v_ref[...] = v
    nkk_ref[...] = -kk
    b_ref[...] = kk * a
    bonus_ref[...] = bonus
    g_ref[...] = g


def _prep(z2d, bnd, params, rows, period):
    n = z2d.shape[0]
    (mu, w0, wd_pad, a0, wa_pad, wg, k_k, k_a, r_k, ones_bd) = params
    kern = functools.partial(_prep_kernel, period=period)
    vec = lambda c: pl.BlockSpec((1, c), lambda i: (0, 0))
    mat = lambda a: pl.BlockSpec(a.shape, lambda i: (0, 0))
    out_spec = pl.BlockSpec((rows, C_RWKV), lambda i: (i, 0))
    out_sd = jax.ShapeDtypeStruct((n, C_RWKV), F32)
    return pl.pallas_call(
        kern,
        grid=(n // rows,),
        in_specs=[pl.BlockSpec((rows, D_SHIFT), lambda i: (i, 0)),
                  pl.BlockSpec((None, bnd.shape[1], D_SHIFT), lambda i: (i, 0, 0)),
                  vec(D_SHIFT), vec(C_RWKV), mat(wd_pad), vec(C_RWKV), mat(wa_pad), mat(wg),
                  vec(C_RWKV), vec(C_RWKV), vec(C_RWKV), mat(ones_bd)],
        out_specs=[out_spec] * 8,
        out_shape=[out_sd] * 8,
        compiler_params=_cparams("parallel"),
        name="rwkv_prep",
    )(z2d, bnd, mu, w0, wd_pad, a0, wa_pad, wg, k_k, k_a, r_k, ones_bd)


def _fold_halves(x):
    return x + pltpu.roll(x, BH_PER_GROUP, axis=1)


def _wkv_kernel(w_ref, nkk_ref, b_ref, k_ref, r_ref, v_ref, nkk0_ref, s0_ref,
                y_ref, sout_ref, s_ref, sa_ref, *, t_blk):
    tb = pl.program_id(1)

    @pl.when(tb == 0)
    def _():
        s_ref[...] = s0_ref[0]
        acc = jnp.zeros((HEAD_DIM, LANES), F32)
        for kp in range(K_HALF):
            acc = acc + s0_ref[0, kp] * nkk0_ref[0, pl.ds(kp, 1), :]
        sa_ref[...] = _fold_halves(acc)

    def step(t, _):
        sa = sa_ref[...]
        vv = v_ref[0, t]
        yacc = jnp.zeros((HEAD_DIM, LANES), F32)
        sacc = jnp.zeros((HEAD_DIM, LANES), F32)
        for kp in range(K_HALF):
            row = pl.ds(kp, 1)
            s = (s_ref[kp] * w_ref[0, t, row, :] + sa * b_ref[0, t, row, :]
                 + vv * k_ref[0, t, row, :])
            s_ref[kp] = s
            yacc = yacc + s * r_ref[0, t, row, :]
            sacc = sacc + s * nkk_ref[0, t, row, :]
        y_ref[0, t] = _fold_halves(yacc)
        sa_ref[...] = _fold_halves(sacc)
        return 0

    lax.fori_loop(0, t_blk, step, 0)

    @pl.when(tb == pl.num_programs(1) - 1)
    def _():
        sout_ref[0] = s_ref[...]


def _wkv(w, nkk_next, b, k, r, v, nkk0, s0, t_blk):
    g, t = w.shape[0], w.shape[1]
    kspec = pl.BlockSpec((1, t_blk, K_HALF, LANES), lambda i, j: (i, j, 0, 0))
    vspec = pl.BlockSpec((1, t_blk, HEAD_DIM, LANES), lambda i, j: (i, j, 0, 0))
    sspec = pl.BlockSpec((1, K_HALF, HEAD_DIM, LANES), lambda i, j: (i, 0, 0, 0))
    return pl.pallas_call(
        functools.partial(_wkv_kernel, t_blk=t_blk),
        grid=(g, t // t_blk),
        in_specs=[kspec, kspec, kspec, kspec, kspec, vspec,
                  pl.BlockSpec((1, K_HALF, LANES), lambda i, j: (i, 0, 0)), sspec],
        out_specs=[vspec, sspec],
        out_shape=[jax.ShapeDtypeStruct((g, t, HEAD_DIM, LANES), F32),
                   jax.ShapeDtypeStruct((g, K_HALF, HEAD_DIM, LANES), F32)],
        scratch_shapes=[pltpu.VMEM((K_HALF, HEAD_DIM, LANES), F32),
                        pltpu.VMEM((HEAD_DIM, LANES), F32)],
        compiler_params=_cparams("parallel", "arbitrary"),
        name="wkv",
    )(w, nkk_next, b, k, r, v, nkk0, s0)


def _to_klanes(x, n_grp, t):
    x = x.reshape(n_grp, 8, t, N_HEADS, 2, K_HALF)
    return x.transpose(0, 2, 5, 4, 1, 3).reshape(n_grp, t, K_HALF, LANES)


def _to_vlanes(x, n_grp, t):
    x = x.reshape(n_grp, 8, t, N_HEADS, HEAD_DIM).transpose(0, 2, 4, 1, 3)
    x = x.reshape(n_grp, t, HEAD_DIM, BH_PER_GROUP)
    return jnp.concatenate([x, x], axis=-1)


def _state_to_lanes(s, n_grp):
    s = s.reshape(n_grp, 8, N_HEADS, HEAD_DIM, 2, K_HALF)
    return s.transpose(0, 5, 3, 4, 1, 2).reshape(n_grp, K_HALF, HEAD_DIM, LANES)


def _state_from_lanes(s, n_grp):
    s = s.reshape(n_grp, K_HALF, HEAD_DIM, 2, 8, N_HEADS)
    return s.transpose(0, 4, 5, 2, 3, 1).reshape(n_grp * 8, N_HEADS, HEAD_DIM, HEAD_DIM)


def _y_from_lanes(y, n_grp, t):
    y = y[..., :BH_PER_GROUP].reshape(n_grp, t, HEAD_DIM, 8, N_HEADS)
    return y.transpose(0, 3, 1, 4, 2).reshape(n_grp * 8 * t, C_RWKV)


def _outproj_kernel(x_ref, c_ref, y_ref, bonus_ref, g_ref, gng_ref, gnb_ref, ones_ref,
                    wo_c_ref, wo_r_ref, n2_ref, wr_hi_ref, wr_lo_ref, br_ref,
                    h_ref, xn_ref, comb_ref):
    ones_bd = ones_ref[...]
    y = y_ref[...]
    inv_n = 1.0 / HEAD_DIM
    mu = _segsum(y, ones_bd) * inv_n
    d = y - mu
    var = _segsum(d * d, ones_bd) * inv_n
    yn = d * lax.rsqrt(var + GN_EPS) * gng_ref[...] + gnb_ref[...]
    rw = (yn + bonus_ref[...]) * g_ref[...]
    h = (x_ref[...]
         + jnp.dot(c_ref[...], wo_c_ref[...], preferred_element_type=F32)
         + jnp.dot(rw.astype(BF16), wo_r_ref[...], preferred_element_type=F32))
    h_ref[...] = h
    xn = h * lax.rsqrt(jnp.mean(h * h, axis=-1, keepdims=True) + RMS_EPS) * n2_ref[...]
    xn_ref[...] = xn.astype(BF16)

    x_hi, x_lo = _split_bf16(xn)
    logits = (jnp.dot(x_hi, wr_hi_ref[...], preferred_element_type=F32)
              + jnp.dot(x_lo, wr_hi_ref[...], preferred_element_type=F32)
              + jnp.dot(x_hi, wr_lo_ref[...], preferred_element_type=F32)) + br_ref[...]

    col = lax.broadcasted_iota(jnp.int32, logits.shape, 1)
    neg = jnp.float32(-jnp.inf)
    big = jnp.int32(ROUTE_COLS)
    is_g = col < N_EXPERT_GROUPS
    g_logit = jnp.where(is_g, logits, neg)
    g_max = jnp.max(g_logit, axis=-1, keepdims=True)
    g_idx = jnp.min(jnp.where(g_logit == g_max, col, big), axis=-1, keepdims=True)
    p_group = 1.0 / jnp.sum(jnp.where(is_g, jnp.exp(logits - g_max), 0.0), axis=-1, keepdims=True)
    lo_col = EXPERT_COL0 + g_idx * EXPERTS_PER_GROUP
    in_grp = (col >= lo_col) & (col < lo_col + EXPERTS_PER_GROUP)
    e1 = jnp.where(in_grp, logits, neg)
    m1 = jnp.max(e1, axis=-1, keepdims=True)
    i1 = jnp.min(jnp.where(e1 == m1, col, big), axis=-1, keepdims=True)
    e2 = jnp.where(col == i1, neg, e1)
    m2 = jnp.max(e2, axis=-1, keepdims=True)
    i2 = jnp.min(jnp.where(e2 == m2, col, big), axis=-1, keepdims=True)
    ex = jnp.exp(m2 - m1)
    w1 = p_group / (1.0 + ex)
    w2 = p_group * ex / (1.0 + ex)
    comb_ref[...] = jnp.where(col == i1, w1, 0.0) + jnp.where(col == i2, w2, 0.0)


def _outproj(x, conv_out, y, bonus, g, params, tm):
    n = x.shape[0]
    (gn_g, gn_b, ones_bd, wo_c, wo_r, n2_g, wr_hi, wr_lo, br) = params
    row = lambda c: pl.BlockSpec((tm, c), lambda i: (i, 0))
    vec = lambda c: pl.BlockSpec((1, c), lambda i: (0, 0))
    mat = lambda a: pl.BlockSpec(a.shape, lambda i: (0, 0))
    return pl.pallas_call(
        _outproj_kernel,
        grid=(n // tm,),
        in_specs=[row(D_MODEL), row(C_CONV), row(C_RWKV), row(C_RWKV), row(C_RWKV),
                  vec(C_RWKV), vec(C_RWKV), mat(ones_bd), mat(wo_c), mat(wo_r), vec(D_MODEL),
                  mat(wr_hi), mat(wr_lo), vec(ROUTE_COLS)],
        out_specs=[row(D_MODEL), row(D_MODEL), row(ROUTE_COLS)],
        out_shape=[jax.ShapeDtypeStruct((n, D_MODEL), F32),
                   jax.ShapeDtypeStruct((n, D_MODEL), BF16),
                   jax.ShapeDtypeStruct((n, ROUTE_COLS), F32)],
        compiler_params=_cparams("parallel"),
        name="outproj_route",
    )(x, conv_out, y, bonus, g, gn_g, gn_b, ones_bd, wo_c, wo_r, n2_g, wr_hi, wr_lo, br)


def _moe_kernel(xn_ref, comb_ref, h_ref, wgu_ref, wd_ref, nf_ref, o_ref, acc_ref):
    e = pl.program_id(1)

    @pl.when(e == 0)
    def _():
        acc_ref[...] = h_ref[...]

    comb = comb_ref[...]
    col = lax.broadcasted_iota(jnp.int32, comb.shape, 1)
    cw = jnp.sum(jnp.where(col == e + EXPERT_COL0, comb, 0.0), axis=-1, keepdims=True)
    gu = jnp.dot(xn_ref[...], wgu_ref[0], preferred_element_type=F32)
    hg = gu[:, :D_EXPERT]
    act = hg * jax.nn.sigmoid(hg) * gu[:, D_EXPERT:] * cw
    acc_ref[...] += jnp.dot(act.astype(BF16), wd_ref[0], preferred_element_type=F32)

    @pl.when(e == pl.num_programs(1) - 1)
    def _():
        h = acc_ref[...]
        o_ref[...] = h * lax.rsqrt(jnp.mean(h * h, axis=-1, keepdims=True) + RMS_EPS) * nf_ref[...]


def _moe(xn, comb, h, wgu, wd, nf_g, tm):
    n = xn.shape[0]
    return pl.pallas_call(
        _moe_kernel,
        grid=(n // tm, N_EXPERTS),
        in_specs=[pl.BlockSpec((tm, D_MODEL), lambda i, e: (i, 0)),
                  pl.BlockSpec((tm, ROUTE_COLS), lambda i, e: (i, 0)),
                  pl.BlockSpec((tm, D_MODEL), lambda i, e: (i, 0)),
                  pl.BlockSpec((1, D_MODEL, 2 * D_EXPERT), lambda i, e: (e, 0, 0)),
                  pl.BlockSpec((1, D_EXPERT, D_MODEL), lambda i, e: (e, 0, 0)),
                  pl.BlockSpec((1, D_MODEL), lambda i, e: (0, 0))],
        out_specs=pl.BlockSpec((tm, D_MODEL), lambda i, e: (i, 0)),
        out_shape=jax.ShapeDtypeStruct((n, D_MODEL), F32),
        scratch_shapes=[pltpu.VMEM((tm, D_MODEL), F32)],
        compiler_params=_cparams("parallel", "arbitrary"),
        name="moe",
    )(xn, comb, h, wgu, wd, nf_g)


def _layer(x, conv_buf, shift_buf, wkv_state, lw, *, tm, conv_seqs, conv_rows, prep_rows,
           wkv_tblk, moe_tm):
    b, t, _ = x.shape
    n = b * t
    n_grp = b // 8
    x2d = x.reshape(n, D_MODEL)

    h_glu, z = _inproj(x2d, lw["norm1_g"], lw["w_in"], tm)

    hx = jnp.concatenate([jnp.zeros((b, 2, C_CONV), F32), conv_buf,
                          h_glu.reshape(b, t, C_CONV)], axis=1)
    new_conv = hx[:, -(CONV_WIDTH - 1):]
    conv_out = _conv(hx, lw["conv_w"], lw["conv_b"], lw["conv_ln_g"], lw["conv_ln_b"],
                     conv_seqs, conv_rows)

    z3 = z.reshape(b, t, D_SHIFT)
    new_shift = z3[:, -1]
    if prep_rows > t:
        period = t
        bnd = jnp.repeat(shift_buf, t, axis=0).reshape(n // prep_rows, prep_rows, D_SHIFT)
    else:
        period = prep_rows
        prev_rows = z3[:, prep_rows - 1::prep_rows][:, :-1]
        bnd = jnp.concatenate([shift_buf[:, None], prev_rows], axis=1).reshape(-1, 1, D_SHIFT)
    r, w, k, v, nkk, bb, bonus, g = _prep(z, bnd, lw["prep"], prep_rows, period)

    r3, w3, k3, v3, nkk3, b3 = (a.reshape(b, t, C_RWKV) for a in (r, w, k, v, nkk, bb))
    nkk_l = _to_klanes(nkk3, n_grp, t)
    nkk_next = jnp.concatenate([nkk_l[:, 1:], jnp.zeros_like(nkk_l[:, :1])], axis=1)
    y_l, s_l = _wkv(_to_klanes(w3, n_grp, t), nkk_next, _to_klanes(b3, n_grp, t),
                    _to_klanes(k3, n_grp, t), _to_klanes(r3, n_grp, t),
                    _to_vlanes(v3, n_grp, t), nkk_l[:, 0], _state_to_lanes(wkv_state, n_grp),
                    wkv_tblk)
    y = _y_from_lanes(y_l, n_grp, t)
    new_wkv = _state_from_lanes(s_l, n_grp)

    h, xn2, comb = _outproj(x2d, conv_out.reshape(n, C_CONV), y, bonus, g, lw["outproj"], tm)
    out = _moe(xn2, comb, h, lw["w_gu"], lw["w_d"], lw["norm_f_g"], moe_tm)
    return out.reshape(b, t, D_MODEL), new_conv, new_shift, new_wkv


def _prepare_weights(norm1_g, w_in, conv_w, conv_b, conv_ln_g, conv_ln_b, mu_shift, w0,
                     w_decay_up, a0, w_aaa_up, w_gate_up, k_k, k_a, r_k, gn_g, gn_b, w_out,
                     norm2_g, w_router_group, b_router_group, w_router_expert, b_router_expert,
                     w_exp_gate, w_exp_up, w_exp_down, norm_f_g):
    row = lambda a: a.reshape(1, -1)
    lane = jnp.arange(C_RWKV) // HEAD_DIM
    ones_bd = (lane[:, None] == lane[None, :]).astype(BF16)
    zeros_lora = jnp.zeros((LANES - w_decay_up.shape[0], C_RWKV), F32)
    wd_pad = jnp.concatenate([w_decay_up, zeros_lora], axis=0).astype(BF16)
    wa_pad = jnp.concatenate([zeros_lora, w_aaa_up], axis=0).astype(BF16)
    pad_cols = ROUTE_COLS - N_EXPERT_GROUPS - N_EXPERTS
    w_route = jnp.concatenate([w_router_group, w_router_expert,
                               jnp.zeros((D_MODEL, pad_cols), F32)], axis=1)
    wr_hi, wr_lo = _split_bf16(w_route)
    b_route = jnp.concatenate([b_router_group, b_router_expert, jnp.zeros((pad_cols,), F32)])
    w_gu = jnp.concatenate([w_exp_gate, w_exp_up], axis=-1).astype(BF16)
    return {
        "norm1_g": row(norm1_g), "w_in": w_in.astype(BF16),
        "conv_w": conv_w, "conv_b": row(conv_b), "conv_ln_g": row(conv_ln_g),
        "conv_ln_b": row(conv_ln_b),
        "prep": (row(mu_shift), row(w0), wd_pad, row(a0), wa_pad, w_gate_up.astype(BF16),
                 row(k_k), row(k_a), row(r_k), ones_bd),
        "outproj": (row(gn_g), row(gn_b), ones_bd, w_out[:C_CONV].astype(BF16),
                    w_out[C_CONV:].astype(BF16), row(norm2_g), wr_hi, wr_lo, row(b_route)),
        "w_gu": w_gu.reshape(N_EXPERTS, D_MODEL, 2 * D_EXPERT),
        "w_d": w_exp_down.astype(BF16).reshape(N_EXPERTS, D_EXPERT, D_MODEL),
        "norm_f_g": row(norm_f_g),
    }


def kernel(x_prompt, x_sample, state_conv, state_shift, state_wkv, norm1_g, w_in, conv_w, conv_b, conv_ln_g, conv_ln_b, mu_shift, w0, w_decay_up, a0, w_aaa_up, w_gate_up, k_k, k_a, r_k, gn_g, gn_b, w_out, norm2_g, w_router_group, b_router_group, w_router_expert, b_router_expert, w_exp_gate, w_exp_up, w_exp_down, norm_f_g):
    depth = norm1_g.shape[0]
    assert depth == 1, "single-layer trunk"
    b = x_prompt.shape[0]
    lw = _prepare_weights(norm1_g[0], w_in[0], conv_w[0], conv_b[0], conv_ln_g[0], conv_ln_b[0],
                          mu_shift[0], w0[0], w_decay_up[0], a0[0], w_aaa_up[0], w_gate_up[0],
                          k_k[0], k_a[0], r_k[0], gn_g[0], gn_b[0], w_out[0], norm2_g[0],
                          w_router_group[0], b_router_group[0], w_router_expert[0],
                          b_router_expert[0], w_exp_gate[0], w_exp_up[0], w_exp_down[0], norm_f_g)
    dt = x_prompt.dtype
    y_p, conv_p, shift_p, wkv_p = _layer(
        x_prompt,
        jnp.zeros((b, CONV_WIDTH - 1, C_CONV), dt),
        jnp.zeros((b, D_SHIFT), dt),
        jnp.zeros((b, N_HEADS, HEAD_DIM, HEAD_DIM), dt), lw,
        tm=512, conv_seqs=1, conv_rows=32, prep_rows=256, wkv_tblk=64, moe_tm=1024)
    y_s, conv_s, shift_s, wkv_s = _layer(
        x_sample, state_conv[0], state_shift[0], state_wkv[0], lw,
        tm=512, conv_seqs=16, conv_rows=8, prep_rows=256, wkv_tblk=8, moe_tm=1024)
    return (y_p, y_s, conv_p[None], shift_p[None], wkv_p[None],
            conv_s[None], shift_s[None], wkv_s[None])
```

```python
import functools

import numpy as np
import jax
import jax.numpy as jnp
from jax import lax
from jax.experimental import pallas as pl
from jax.experimental.pallas import tpu as pltpu

F32 = jnp.float32
BF16 = jnp.bfloat16

D_MODEL = 1024
C_CONV = 512
C_RWKV = 512
HEAD_DIM = 64
N_HEADS = 8
CONV_WIDTH = 31
D_SHIFT = 1792
N_EXPERT_GROUPS = 4
EXPERTS_PER_GROUP = 8
N_EXPERTS = N_EXPERT_GROUPS * EXPERTS_PER_GROUP
D_EXPERT = 256
RMS_EPS = 1e-6
LN_EPS = 1e-5
GN_EPS = 64e-5

LANES = 128
SUBLANES = 8
CONV_PAD = 32
WKV_BATCH = 8
BH_PER_GROUP = WKV_BATCH * N_HEADS
K_HALF = HEAD_DIM // 2
WKV_PARTS = 2
LOAD_ROWS = LANES
ROUTE_COLS = LANES
EXPERT_COL0 = N_EXPERT_GROUPS
VMEM_LIMIT = 56 * 1024 * 1024


def _key_perm():
    kp, kh, h = np.meshgrid(np.arange(K_HALF), np.arange(2), np.arange(N_HEADS), indexing="ij")
    return (h * HEAD_DIM + kh * K_HALF + kp).reshape(-1)


def _value_perm():
    v, h = np.meshgrid(np.arange(HEAD_DIM), np.arange(N_HEADS), indexing="ij")
    return (h * HEAD_DIM + v).reshape(-1)


PERM_KEY = _key_perm()
PERM_VAL = _value_perm()
PERM_Z = np.concatenate([PERM_KEY, C_RWKV + PERM_KEY, 2 * C_RWKV + PERM_VAL,
                         np.arange(3 * C_RWKV, D_SHIFT)])
INV_PERM_Z = np.argsort(PERM_Z)


def _cparams(*sem):
    return pltpu.CompilerParams(dimension_semantics=sem, vmem_limit_bytes=VMEM_LIMIT)


def _split_bf16(x):
    hi = x.astype(BF16)
    lo = (x - hi.astype(F32)).astype(BF16)
    return hi, lo


def _segsum(x, ones_bd):
    hi, lo = _split_bf16(x)
    return (jnp.dot(hi, ones_bd, preferred_element_type=F32)
            + jnp.dot(lo, ones_bd, preferred_element_type=F32))


def _inproj_kernel(x_ref, g_ref, w_ref, h_ref, z_ref):
    x = x_ref[...]
    xn = x * lax.rsqrt(jnp.mean(x * x, axis=-1, keepdims=True) + RMS_EPS) * g_ref[...]
    p = jnp.dot(xn.astype(BF16), w_ref[...], preferred_element_type=F32)
    h_ref[...] = p[:, :C_CONV] * jax.nn.sigmoid(p[:, C_CONV:2 * C_CONV])
    z_ref[...] = p[:, 2 * C_CONV:]


def _inproj(x, g, w_bf, tm):
    n = x.shape[0]
    d_in = w_bf.shape[1]
    return pl.pallas_call(
        _inproj_kernel,
        grid=(n // tm,),
        in_specs=[pl.BlockSpec((tm, D_MODEL), lambda i: (i, 0)),
                  pl.BlockSpec((1, D_MODEL), lambda i: (0, 0)),
                  pl.BlockSpec((D_MODEL, d_in), lambda i: (0, 0))],
        out_specs=[pl.BlockSpec((tm, C_CONV), lambda i: (i, 0)),
                   pl.BlockSpec((tm, D_SHIFT), lambda i: (i, 0))],
        out_shape=[jax.ShapeDtypeStruct((n, C_CONV), F32),
                   jax.ShapeDtypeStruct((n, D_SHIFT), F32)],
        compiler_params=_cparams("parallel"),
        name="inproj",
    )(x, g, w_bf)


def _conv_kernel(hx_ref, w_ref, b_ref, lg_ref, lb_ref, o_ref, win_ref, *, rows, n_chunks, n_seq):
    def seq_body(s, _):
        def chunk_body(c, _):
            t0 = pl.multiple_of(c * rows, SUBLANES)
            win_ref[...] = hx_ref[s, pl.ds(t0, rows + CONV_PAD), :]
            acc = jnp.zeros((rows, C_CONV), F32) + b_ref[...]
            for j in range(CONV_WIDTH):
                acc = acc + win_ref[pl.ds(j + 2, rows), :] * w_ref[pl.ds(j, 1), :]
            mu = jnp.mean(acc, axis=-1, keepdims=True)
            d = acc - mu
            var = jnp.mean(d * d, axis=-1, keepdims=True)
            of = d * lax.rsqrt(var + LN_EPS) * lg_ref[...] + lb_ref[...]
            o_ref[s, pl.ds(t0, rows), :] = (of * jax.nn.sigmoid(of)).astype(o_ref.dtype)
            return 0
        lax.fori_loop(0, n_chunks, chunk_body, 0)
        return 0
    lax.fori_loop(0, n_seq, seq_body, 0)


def _conv(hx, conv_w, conv_b, ln_g, ln_b, n_seq, rows):
    b, tp, _ = hx.shape
    t = tp - CONV_PAD
    kern = functools.partial(_conv_kernel, rows=rows, n_chunks=t // rows, n_seq=n_seq)
    vec = pl.BlockSpec((1, C_CONV), lambda i: (0, 0))
    return pl.pallas_call(
        kern,
        grid=(b // n_seq,),
        in_specs=[pl.BlockSpec((n_seq, tp, C_CONV), lambda i: (i, 0, 0)),
                  pl.BlockSpec((CONV_WIDTH, C_CONV), lambda i: (0, 0)),
                  vec, vec, vec],
        out_specs=pl.BlockSpec((n_seq, t, C_CONV), lambda i: (i, 0, 0)),
        out_shape=jax.ShapeDtypeStruct((b, t, C_CONV), BF16),
        scratch_shapes=[pltpu.VMEM((rows + CONV_PAD, C_CONV), F32)],
        compiler_params=_cparams("parallel"),
        name="conv",
    )(hx, conv_w, conv_b, ln_g, ln_b)


def _prep_kernel(z_ref, bnd_ref, mu_ref, w0_ref, wd_ref, a0_ref, wa_ref, wg_ref, kk_ref, ka_ref,
                 rk_ref, ones_ref,
                 r_ref, w_ref, k_ref, v_ref, nkk_ref, b_ref, bonus_ref, g_ref, *, period):
    z = z_ref[...]
    row_id = lax.broadcasted_iota(jnp.int32, z.shape, 0)
    z_prev = jnp.where(row_id % period == 0, bnd_ref[...], pltpu.roll(z, 1, axis=0))
    zs = z + (z_prev - z) * mu_ref[...]
    r = zs[:, :C_RWKV]
    k = zs[:, C_RWKV:2 * C_RWKV]
    v = zs[:, 2 * C_RWKV:3 * C_RWKV]
    wa = zs[:, 3 * C_RWKV:3 * C_RWKV + LANES]
    gl = zs[:, 3 * C_RWKV + LANES:]
    ones_bd = ones_ref[...]

    lw = w0_ref[...] + jnp.dot(jnp.tanh(wa).astype(BF16), wd_ref[...], preferred_element_type=F32)
    w_log = -(jnp.maximum(-lw, 0.0) + jnp.log(1.0 + jnp.exp(-jnp.abs(lw)))) - 0.5
    decay = jnp.exp(-jnp.exp(w_log))
    a = jax.nn.sigmoid(a0_ref[...] + jnp.dot(wa.astype(BF16), wa_ref[...],
                                             preferred_element_type=F32))
    g = jnp.dot(jax.nn.sigmoid(gl).astype(BF16), wg_ref[...], preferred_element_type=F32)

    kk = k * kk_ref[...]
    norm = jnp.maximum(jnp.sqrt(_segsum(kk * kk, ones_bd)), 1e-12)
    kk = kk / norm
    k_mod = k * (1.0 + (a - 1.0) * ka_ref[...])
    bonus = _segsum(r * k_mod * rk_ref[...], ones_bd) * v

    def store_cols(ref, x):
        xt = x.T
        ntok = ref.shape[2]
        for q in range(ref.shape[0]):
            ref[q] = xt[:, q * ntok:(q + 1) * ntok]

    store_cols(r_ref, r)
    store_cols(w_ref, decay)
    store_cols(k_ref, k_mod)
    store_cols(v_ref, v)
    store_cols(nkk_ref, -kk)
    store_cols(b_ref, kk * a)
    bonus_ref[...] = bonus
    g_ref[...] = g


def _col_major_spec(rows, length):
    if rows <= length:
        per_b = length // rows
        return pl.BlockSpec((1, C_RWKV, rows), lambda i: (i // per_b, 0, i % per_b))
    return pl.BlockSpec((rows // length, C_RWKV, length), lambda i: (i, 0, 0))


def _prep(z2d, bnd, params, rows, period):
    n = z2d.shape[0]
    length = n // WKV_BATCH
    (mu, w0, wd_pad, a0, wa_pad, wg, k_k, k_a, r_k, ones_bd) = params
    kern = functools.partial(_prep_kernel, period=period)
    vec = lambda c: pl.BlockSpec((1, c), lambda i: (0, 0))
    mat = lambda a: pl.BlockSpec(a.shape, lambda i: (0, 0))
    row_spec = pl.BlockSpec((rows, C_RWKV), lambda i: (i, 0))
    row_sd = jax.ShapeDtypeStruct((n, C_RWKV), F32)
    col_spec = _col_major_spec(rows, length)
    col_sd = jax.ShapeDtypeStruct((WKV_BATCH, C_RWKV, length), F32)
    return pl.pallas_call(
        kern,
        grid=(n // rows,),
        in_specs=[pl.BlockSpec((rows, D_SHIFT), lambda i: (i, 0)),
                  pl.BlockSpec((None, bnd.shape[1], D_SHIFT), lambda i: (i, 0, 0)),
                  vec(D_SHIFT), vec(C_RWKV), mat(wd_pad), vec(C_RWKV), mat(wa_pad), mat(wg),
                  vec(C_RWKV), vec(C_RWKV), vec(C_RWKV), mat(ones_bd)],
        out_specs=[col_spec] * 6 + [row_spec] * 2,
        out_shape=[col_sd] * 6 + [row_sd] * 2,
        compiler_params=_cparams("parallel"),
        name="rwkv_prep",
    )(z2d, bnd, mu, w0, wd_pad, a0, wa_pad, wg, k_k, k_a, r_k, ones_bd)


def _fold_halves(x):
    return x + pltpu.roll(x, BH_PER_GROUP, axis=x.ndim - 1)


def _key_cols_to_lanes(src_ref, dst_ref):
    chunk = 2 * N_HEADS
    for kp in range(K_HALF):
        pieces = [src_ref[b8, pl.ds(kp * chunk + half * N_HEADS, N_HEADS), :]
                  for half in range(2) for b8 in range(WKV_BATCH)]
        dst_ref[kp, :LOAD_ROWS, :] = jnp.concatenate(pieces, axis=0).T


def _value_cols_to_lanes(src_ref, dst_ref):
    for v in range(HEAD_DIM):
        pieces = [src_ref[b8, pl.ds(v * N_HEADS, N_HEADS), :]
                  for _ in range(2) for b8 in range(WKV_BATCH)]
        dst_ref[pl.ds(v, LOAD_ROWS, stride=HEAD_DIM), :] = jnp.concatenate(pieces, axis=0).T


def _lanes_to_value_cols(y_s, yt_ref):
    for v in range(HEAD_DIM):
        yt = y_s[pl.ds(v, LOAD_ROWS, stride=HEAD_DIM), :].T
        folded = yt[:BH_PER_GROUP] + yt[BH_PER_GROUP:]
        for b8 in range(WKV_BATCH):
            yt_ref[b8, pl.ds(v * N_HEADS, N_HEADS), :] = folded[b8 * N_HEADS:(b8 + 1) * N_HEADS]


def _wkv_kernel(w_ref, nkk_ref, b_ref, k_ref, r_ref, v_ref, s0_ref, y_ref, sout_ref,
                w_s, nkk_s, b_s, k_s, r_s, v_s, y_s, s_ref,
                *, t_blk, steps_per_load, carry_state, n_parts):
    j = pl.program_id(0)
    sub = j % steps_per_load
    part = HEAD_DIM // n_parts
    parts = [slice(i * part, (i + 1) * part) for i in range(n_parts)]

    @pl.when(sub == 0)
    def _():
        for src, dst in ((w_ref, w_s), (nkk_ref, nkk_s), (b_ref, b_s), (k_ref, k_s), (r_ref, r_s)):
            _key_cols_to_lanes(src, dst)
        _value_cols_to_lanes(v_ref, v_s)

    if carry_state:
        @pl.when(j == 0)
        def _():
            s_ref[...] = s0_ref[0]
    else:
        s_ref[...] = s0_ref[0]

    tok0 = sub * t_blk
    acc = jnp.zeros((HEAD_DIM, LANES), F32)
    for kp in range(K_HALF):
        acc = acc + s_ref[kp] * nkk_s[kp, pl.ds(tok0, 1), :]
    sa_init = tuple(_fold_halves(acc[p, :]) for p in parts[:-1]) + (acc[parts[-1], :],)

    def part_step(tok, nxt_tok, rows, sa):
        cur = pl.ds(tok, 1)
        vrows = pl.ds(pl.multiple_of(tok * HEAD_DIM, HEAD_DIM) + rows.start, part)
        vv = v_s[vrows, :]
        yacc = jnp.zeros((part, LANES), F32)
        sacc = jnp.zeros((part, LANES), F32)
        for kp in range(K_HALF):
            s = (s_ref[kp, rows, :] * w_s[kp, cur, :] + sa * b_s[kp, cur, :]
                 + vv * k_s[kp, cur, :])
            s_ref[kp, rows, :] = s
            yacc = yacc + s * r_s[kp, cur, :]
            sacc = sacc + s * nkk_s[kp, pl.ds(nxt_tok, 1), :]
        y_s[vrows, :] = yacc
        return sacc

    def step(t, sa_parts):
        tok = tok0 + t
        nxt_tok = tok0 + jnp.minimum(t + 1, t_blk - 1)
        sa_last = _fold_halves(sa_parts[-1])
        nxt = [_fold_halves(part_step(tok, nxt_tok, rows, sa))
               for rows, sa in zip(parts[:-1], sa_parts[:-1])]
        return tuple(nxt) + (part_step(tok, nxt_tok, parts[-1], sa_last),)

    lax.fori_loop(0, t_blk, step, sa_init)

    if carry_state:
        @pl.when(j == pl.num_programs(0) - 1)
        def _():
            sout_ref[0] = s_ref[...]
    else:
        sout_ref[0] = s_ref[...]

    @pl.when(sub == steps_per_load - 1)
    def _():
        _lanes_to_value_cols(y_s, y_ref)


def _wkv(w, nkk, b, k, r, v, s0, *, t_blk, carry_state):
    length = w.shape[2]
    steps_per_load = LOAD_ROWS // t_blk
    n_steps = length // t_blk
    tok_spec = pl.BlockSpec((WKV_BATCH, C_RWKV, LOAD_ROWS), lambda j: (0, 0, j // steps_per_load))
    state_blk = (1, K_HALF, HEAD_DIM, LANES)
    if carry_state:
        sspec = pl.BlockSpec(state_blk, lambda j: (0, 0, 0, 0))
    else:
        sspec = pl.BlockSpec(state_blk, lambda j: (j, 0, 0, 0))
    key_rows = pltpu.VMEM((K_HALF, LOAD_ROWS + SUBLANES, LANES), F32)
    val_rows = pltpu.VMEM((LOAD_ROWS * HEAD_DIM, LANES), F32)
    return pl.pallas_call(
        functools.partial(_wkv_kernel, t_blk=t_blk, steps_per_load=steps_per_load,
                          carry_state=carry_state, n_parts=WKV_PARTS),
        grid=(n_steps,),
        in_specs=[tok_spec] * 6 + [sspec],
        out_specs=[tok_spec, sspec],
        out_shape=[jax.ShapeDtypeStruct(w.shape, F32), jax.ShapeDtypeStruct(s0.shape, F32)],
        scratch_shapes=[key_rows] * 5 + [val_rows, val_rows,
                                         pltpu.VMEM((K_HALF, HEAD_DIM, LANES), F32)],
        compiler_params=_cparams("arbitrary"),
        name="wkv",
    )(w, nkk, b, k, r, v, s0)


def _state_to_lanes(s, n_grp):
    s = s.reshape(WKV_BATCH, n_grp, N_HEADS, HEAD_DIM, 2, K_HALF)
    return s.transpose(1, 5, 3, 4, 0, 2).reshape(n_grp, K_HALF, HEAD_DIM, LANES)


def _state_from_lanes(s, n_grp):
    s = s.reshape(n_grp, K_HALF, HEAD_DIM, 2, WKV_BATCH, N_HEADS)
    return s.transpose(4, 0, 5, 2, 3, 1).reshape(n_grp * WKV_BATCH, N_HEADS, HEAD_DIM, HEAD_DIM)


def _outproj_kernel(x_ref, c_ref, y_ref, bonus_ref, g_ref, gng_ref, gnb_ref, ones_ref,
                    wo_c_ref, wo_r_ref, n2_ref, wr_hi_ref, wr_lo_ref, br_ref,
                    h_ref, xn_ref, comb_ref):
    ones_bd = ones_ref[...]
    y = jnp.concatenate([y_ref[q].T for q in range(y_ref.shape[0])], axis=0)
    inv_n = 1.0 / HEAD_DIM
    mu = _segsum(y, ones_bd) * inv_n
    d = y - mu
    var = _segsum(d * d, ones_bd) * inv_n
    yn = d * lax.rsqrt(var + GN_EPS) * gng_ref[...] + gnb_ref[...]
    rw = (yn + bonus_ref[...]) * g_ref[...]
    h = (x_ref[...]
         + jnp.dot(c_ref[...], wo_c_ref[...], preferred_element_type=F32)
         + jnp.dot(rw.astype(BF16), wo_r_ref[...], preferred_element_type=F32))
    h_ref[...] = h
    xn = h * lax.rsqrt(jnp.mean(h * h, axis=-1, keepdims=True) + RMS_EPS) * n2_ref[...]
    xn_ref[...] = xn.astype(BF16)

    x_hi, x_lo = _split_bf16(xn)
    logits = (jnp.dot(x_hi, wr_hi_ref[...], preferred_element_type=F32)
              + jnp.dot(x_lo, wr_hi_ref[...], preferred_element_type=F32)
              + jnp.dot(x_hi, wr_lo_ref[...], preferred_element_type=F32)) + br_ref[...]

    col = lax.broadcasted_iota(jnp.int32, logits.shape, 1)
    neg = jnp.float32(-jnp.inf)
    big = jnp.int32(ROUTE_COLS)
    is_g = col < N_EXPERT_GROUPS
    g_logit = jnp.where(is_g, logits, neg)
    g_max = jnp.max(g_logit, axis=-1, keepdims=True)
    g_idx = jnp.min(jnp.where(g_logit == g_max, col, big), axis=-1, keepdims=True)
    p_group = 1.0 / jnp.sum(jnp.where(is_g, jnp.exp(logits - g_max), 0.0), axis=-1, keepdims=True)
    lo_col = EXPERT_COL0 + g_idx * EXPERTS_PER_GROUP
    in_grp = (col >= lo_col) & (col < lo_col + EXPERTS_PER_GROUP)
    e1 = jnp.where(in_grp, logits, neg)
    m1 = jnp.max(e1, axis=-1, keepdims=True)
    i1 = jnp.min(jnp.where(e1 == m1, col, big), axis=-1, keepdims=True)
    e2 = jnp.where(col == i1, neg, e1)
    m2 = jnp.max(e2, axis=-1, keepdims=True)
    i2 = jnp.min(jnp.where(e2 == m2, col, big), axis=-1, keepdims=True)
    ex = jnp.exp(m2 - m1)
    w1 = p_group / (1.0 + ex)
    w2 = p_group * ex / (1.0 + ex)
    comb_ref[...] = jnp.where(col == i1, w1, 0.0) + jnp.where(col == i2, w2, 0.0)


def _outproj(x, conv_out, y, bonus, g, params, tm):
    n = x.shape[0]
    (gn_g, gn_b, ones_bd, wo_c, wo_r, n2_g, wr_hi, wr_lo, br) = params
    row = lambda c: pl.BlockSpec((tm, c), lambda i: (i, 0))
    vec = lambda c: pl.BlockSpec((1, c), lambda i: (0, 0))
    mat = lambda a: pl.BlockSpec(a.shape, lambda i: (0, 0))
    return pl.pallas_call(
        _outproj_kernel,
        grid=(n // tm,),
        in_specs=[row(D_MODEL), row(C_CONV), _col_major_spec(tm, n // WKV_BATCH), row(C_RWKV),
                  row(C_RWKV),
                  vec(C_RWKV), vec(C_RWKV), mat(ones_bd), mat(wo_c), mat(wo_r), vec(D_MODEL),
                  mat(wr_hi), mat(wr_lo), vec(ROUTE_COLS)],
        out_specs=[row(D_MODEL), row(D_MODEL), row(ROUTE_COLS)],
        out_shape=[jax.ShapeDtypeStruct((n, D_MODEL), F32),
                   jax.ShapeDtypeStruct((n, D_MODEL), BF16),
                   jax.ShapeDtypeStruct((n, ROUTE_COLS), F32)],
        compiler_params=_cparams("parallel"),
        name="outproj_route",
    )(x, conv_out, y, bonus, g, gn_g, gn_b, ones_bd, wo_c, wo_r, n2_g, wr_hi, wr_lo, br)


def _moe_kernel(xn_ref, comb_ref, h_ref, wgu_ref, wd_ref, nf_ref, o_ref, acc_ref):
    e = pl.program_id(1)

    @pl.when(e == 0)
    def _():
        acc_ref[...] = h_ref[...]

    comb = comb_ref[...]
    col = lax.broadcasted_iota(jnp.int32, comb.shape, 1)
    cw = jnp.sum(jnp.where(col == e + EXPERT_COL0, comb, 0.0), axis=-1, keepdims=True)
    gu = jnp.dot(xn_ref[...], wgu_ref[0], preferred_element_type=F32)
    hg = gu[:, :D_EXPERT]
    act = hg * jax.nn.sigmoid(hg) * gu[:, D_EXPERT:] * cw
    acc_ref[...] += jnp.dot(act.astype(BF16), wd_ref[0], preferred_element_type=F32)

    @pl.when(e == pl.num_programs(1) - 1)
    def _():
        h = acc_ref[...]
        o_ref[...] = h * lax.rsqrt(jnp.mean(h * h, axis=-1, keepdims=True) + RMS_EPS) * nf_ref[...]


def _moe(xn, comb, h, wgu, wd, nf_g, tm):
    n = xn.shape[0]
    return pl.pallas_call(
        _moe_kernel,
        grid=(n // tm, N_EXPERTS),
        in_specs=[pl.BlockSpec((tm, D_MODEL), lambda i, e: (i, 0)),
                  pl.BlockSpec((tm, ROUTE_COLS), lambda i, e: (i, 0)),
                  pl.BlockSpec((tm, D_MODEL), lambda i, e: (i, 0)),
                  pl.BlockSpec((1, D_MODEL, 2 * D_EXPERT), lambda i, e: (e, 0, 0)),
                  pl.BlockSpec((1, D_EXPERT, D_MODEL), lambda i, e: (e, 0, 0)),
                  pl.BlockSpec((1, D_MODEL), lambda i, e: (0, 0))],
        out_specs=pl.BlockSpec((tm, D_MODEL), lambda i, e: (i, 0)),
        out_shape=jax.ShapeDtypeStruct((n, D_MODEL), F32),
        scratch_shapes=[pltpu.VMEM((tm, D_MODEL), F32)],
        compiler_params=_cparams("parallel", "arbitrary"),
        name="moe",
    )(xn, comb, h, wgu, wd, nf_g)


def _layer(x, conv_buf, shift_buf, wkv_state, lw, *, tm, conv_seqs, conv_rows, prep_rows,
           wkv_tblk, moe_tm):
    b, t, _ = x.shape
    n = b * t
    n_grp = b // WKV_BATCH
    x2d = x.reshape(n, D_MODEL)

    h_glu, z = _inproj(x2d, lw["norm1_g"], lw["w_in"], tm)

    hx = jnp.concatenate([jnp.zeros((b, 2, C_CONV), F32), conv_buf,
                          h_glu.reshape(b, t, C_CONV)], axis=1)
    new_conv = hx[:, -(CONV_WIDTH - 1):]
    conv_out = _conv(hx, lw["conv_w"], lw["conv_b"], lw["conv_ln_g"], lw["conv_ln_b"],
                     conv_seqs, conv_rows)

    z3 = z.reshape(b, t, D_SHIFT)
    new_shift = z3[:, -1][:, INV_PERM_Z]
    shift_buf = shift_buf[:, PERM_Z]
    if prep_rows > t:
        period = t
        bnd = jnp.repeat(shift_buf, t, axis=0).reshape(n // prep_rows, prep_rows, D_SHIFT)
    else:
        period = prep_rows
        prev_rows = z3[:, prep_rows - 1::prep_rows][:, :-1]
        bnd = jnp.concatenate([shift_buf[:, None], prev_rows], axis=1).reshape(-1, 1, D_SHIFT)
    r, w, k, v, nkk, bb, bonus, g = _prep(z, bnd, lw["prep"], prep_rows, period)

    y_cols, s_l = _wkv(w, nkk, bb, k, r, v, _state_to_lanes(wkv_state, n_grp),
                       t_blk=wkv_tblk, carry_state=(n_grp == 1))
    new_wkv = _state_from_lanes(s_l, n_grp)

    h, xn2, comb = _outproj(x2d, conv_out.reshape(n, C_CONV), y_cols, bonus, g, lw["outproj"], tm)
    out = _moe(xn2, comb, h, lw["w_gu"], lw["w_d"], lw["norm_f_g"], moe_tm)
    return out.reshape(b, t, D_MODEL), new_conv, new_shift, new_wkv


def _prepare_weights(norm1_g, w_in, conv_w, conv_b, conv_ln_g, conv_ln_b, mu_shift, w0,
                     w_decay_up, a0, w_aaa_up, w_gate_up, k_k, k_a, r_k, gn_g, gn_b, w_out,
                     norm2_g, w_router_group, b_router_group, w_router_expert, b_router_expert,
                     w_exp_gate, w_exp_up, w_exp_down, norm_f_g):
    row = lambda a: a.reshape(1, -1)
    head = jnp.arange(C_RWKV) % N_HEADS
    ones_bd = (head[:, None] == head[None, :]).astype(BF16)
    w_in_perm = jnp.concatenate([w_in[:, :2 * C_CONV], w_in[:, 2 * C_CONV:][:, PERM_Z]], axis=1)
    zeros_lora = jnp.zeros((LANES - w_decay_up.shape[0], C_RWKV), F32)
    wd_pad = jnp.concatenate([w_decay_up[:, PERM_KEY], zeros_lora], axis=0).astype(BF16)
    wa_pad = jnp.concatenate([zeros_lora, w_aaa_up[:, PERM_KEY]], axis=0).astype(BF16)
    pad_cols = ROUTE_COLS - N_EXPERT_GROUPS - N_EXPERTS
    w_route = jnp.concatenate([w_router_group, w_router_expert,
                               jnp.zeros((D_MODEL, pad_cols), F32)], axis=1)
    wr_hi, wr_lo = _split_bf16(w_route)
    b_route = jnp.concatenate([b_router_group, b_router_expert, jnp.zeros((pad_cols,), F32)])
    w_gu = jnp.concatenate([w_exp_gate, w_exp_up], axis=-1).astype(BF16)
    return {
        "norm1_g": row(norm1_g), "w_in": w_in_perm.astype(BF16),
        "conv_w": conv_w, "conv_b": row(conv_b), "conv_ln_g": row(conv_ln_g),
        "conv_ln_b": row(conv_ln_b),
        "prep": (row(mu_shift[PERM_Z]), row(w0[PERM_KEY]), wd_pad, row(a0[PERM_KEY]), wa_pad,
                 w_gate_up[:, PERM_VAL].astype(BF16), row(k_k[PERM_KEY]), row(k_a[PERM_KEY]),
                 row(r_k[PERM_KEY]), ones_bd),
        "outproj": (row(gn_g[PERM_VAL]), row(gn_b[PERM_VAL]), ones_bd, w_out[:C_CONV].astype(BF16),
                    w_out[C_CONV:][PERM_VAL].astype(BF16), row(norm2_g), wr_hi, wr_lo, row(b_route)),
        "w_gu": w_gu.reshape(N_EXPERTS, D_MODEL, 2 * D_EXPERT),
        "w_d": w_exp_down.astype(BF16).reshape(N_EXPERTS, D_EXPERT, D_MODEL),
        "norm_f_g": row(norm_f_g),
    }


def kernel(x_prompt, x_sample, state_conv, state_shift, state_wkv, norm1_g, w_in, conv_w, conv_b, conv_ln_g, conv_ln_b, mu_shift, w0, w_decay_up, a0, w_aaa_up, w_gate_up, k_k, k_a, r_k, gn_g, gn_b, w_out, norm2_g, w_router_group, b_router_group, w_router_expert, b_router_expert, w_exp_gate, w_exp_up, w_exp_down, norm_f_g):
    depth = norm1_g.shape[0]
    assert depth == 1, "single-layer trunk"
    b = x_prompt.shape[0]
    assert b == WKV_BATCH and x_prompt.shape[1] % LOAD_ROWS == 0
    assert x_sample.shape[0] * x_sample.shape[1] == WKV_BATCH * LOAD_ROWS
    lw = _prepare_weights(norm1_g[0], w_in[0], conv_w[0], conv_b[0], conv_ln_g[0], conv_ln_b[0],
                          mu_shift[0], w0[0], w_decay_up[0], a0[0], w_aaa_up[0], w_gate_up[0],
                          k_k[0], k_a[0], r_k[0], gn_g[0], gn_b[0], w_out[0], norm2_g[0],
                          w_router_group[0], b_router_group[0], w_router_expert[0],
                          b_router_expert[0], w_exp_gate[0], w_exp_up[0], w_exp_down[0], norm_f_g)
    dt = x_prompt.dtype
    y_p, conv_p, shift_p, wkv_p = _layer(
        x_prompt,
        jnp.zeros((b, CONV_WIDTH - 1, C_CONV), dt),
        jnp.zeros((b, D_SHIFT), dt),
        jnp.zeros((b, N_HEADS, HEAD_DIM, HEAD_DIM), dt), lw,
        tm=512, conv_seqs=1, conv_rows=32, prep_rows=256, wkv_tblk=LOAD_ROWS, moe_tm=1024)
    y_s, conv_s, shift_s, wkv_s = _layer(
        x_sample, state_conv[0], state_shift[0], state_wkv[0], lw,
        tm=512, conv_seqs=16, conv_rows=8, prep_rows=256, wkv_tblk=x_sample.shape[1], moe_tm=1024)
    return (y_p, y_s, conv_p[None], shift_p[None], wkv_p[None],
            conv_s[None], shift_s[None], wkv_s[None])
```

```python
import functools

import numpy as np
import jax
import jax.numpy as jnp
from jax import lax
from jax.experimental import pallas as pl
from jax.experimental.pallas import tpu as pltpu

F32 = jnp.float32
BF16 = jnp.bfloat16

D_MODEL = 1024
C_CONV = 512
C_RWKV = 512
HEAD_DIM = 64
N_HEADS = 8
CONV_WIDTH = 31
D_SHIFT = 1792
N_EXPERT_GROUPS = 4
EXPERTS_PER_GROUP = 8
N_EXPERTS = N_EXPERT_GROUPS * EXPERTS_PER_GROUP
D_EXPERT = 256
RMS_EPS = 1e-6
LN_EPS = 1e-5
GN_EPS = 64e-5

LANES = 128
SUBLANES = 8
CONV_PAD = 32
WKV_BATCH = 8
BH_PER_GROUP = WKV_BATCH * N_HEADS
K_HALF = HEAD_DIM // 2
WKV_PARTS = 2
LOAD_ROWS = LANES
ROUTE_COLS = LANES
EXPERT_COL0 = N_EXPERT_GROUPS
MOE_ROWS = 256
VMEM_LIMIT = 56 * 1024 * 1024


def _key_perm():
    kp, kh, h = np.meshgrid(np.arange(K_HALF), np.arange(2), np.arange(N_HEADS), indexing="ij")
    return (h * HEAD_DIM + kh * K_HALF + kp).reshape(-1)


def _value_perm():
    v, h = np.meshgrid(np.arange(HEAD_DIM), np.arange(N_HEADS), indexing="ij")
    return (h * HEAD_DIM + v).reshape(-1)


PERM_KEY = _key_perm()
PERM_VAL = _value_perm()
PERM_Z = np.concatenate([PERM_KEY, C_RWKV + PERM_KEY, 2 * C_RWKV + PERM_VAL,
                         np.arange(3 * C_RWKV, D_SHIFT)])
INV_PERM_Z = np.argsort(PERM_Z)


def _cparams(*sem):
    return pltpu.CompilerParams(dimension_semantics=sem, vmem_limit_bytes=VMEM_LIMIT)


def _split_bf16(x):
    hi = x.astype(BF16)
    lo = (x - hi.astype(F32)).astype(BF16)
    return hi, lo


def _segsum(x, ones_bd):
    hi, lo = _split_bf16(x)
    return (jnp.dot(hi, ones_bd, preferred_element_type=F32)
            + jnp.dot(lo, ones_bd, preferred_element_type=F32))


def _inproj_kernel(x_ref, g_ref, w_ref, h_ref, z_ref):
    x = x_ref[...]
    xn = x * lax.rsqrt(jnp.mean(x * x, axis=-1, keepdims=True) + RMS_EPS) * g_ref[...]
    p = jnp.dot(xn.astype(BF16), w_ref[...], preferred_element_type=F32)
    h_ref[...] = p[:, :C_CONV] * jax.nn.sigmoid(p[:, C_CONV:2 * C_CONV])
    z_ref[...] = p[:, 2 * C_CONV:]


def _inproj(x, g, w_bf, tm):
    n = x.shape[0]
    d_in = w_bf.shape[1]
    return pl.pallas_call(
        _inproj_kernel,
        grid=(n // tm,),
        in_specs=[pl.BlockSpec((tm, D_MODEL), lambda i: (i, 0)),
                  pl.BlockSpec((1, D_MODEL), lambda i: (0, 0)),
                  pl.BlockSpec((D_MODEL, d_in), lambda i: (0, 0))],
        out_specs=[pl.BlockSpec((tm, C_CONV), lambda i: (i, 0)),
                   pl.BlockSpec((tm, D_SHIFT), lambda i: (i, 0))],
        out_shape=[jax.ShapeDtypeStruct((n, C_CONV), F32),
                   jax.ShapeDtypeStruct((n, D_SHIFT), F32)],
        compiler_params=_cparams("parallel"),
        name="inproj",
    )(x, g, w_bf)


def _conv_kernel(h_ref, hist_ref, w_ref, b_ref, lg_ref, lb_ref, o_ref, win_ref, sh_ref,
                 *, rows, sub_rows, n_chunks, n_seq):
    shifted_rows = rows + CONV_PAD - SUBLANES

    def seq_body(s, _):
        def chunk_body(c, _):
            t0 = pl.multiple_of(c * rows, SUBLANES)
            if n_chunks == 1:
                win_ref[pl.ds(0, CONV_PAD), :] = hist_ref[s]
            else:
                prev0 = pl.multiple_of(jnp.maximum(t0 - CONV_PAD, 0), SUBLANES)
                win_ref[pl.ds(0, CONV_PAD), :] = jnp.where(
                    c == 0, hist_ref[s], h_ref[s, pl.ds(prev0, CONV_PAD), :])
            win_ref[pl.ds(CONV_PAD, rows), :] = h_ref[s, pl.ds(t0, rows), :]
            sh_ref[0] = win_ref[...]
            for sft in range(1, SUBLANES):
                sh_ref[sft, pl.ds(0, shifted_rows), :] = win_ref[pl.ds(sft, shifted_rows), :]
            for blk in range(rows // sub_rows):
                r0 = blk * sub_rows
                acc = jnp.zeros((sub_rows, C_CONV), F32) + b_ref[...]
                for j in range(CONV_WIDTH):
                    off = j + 2
                    acc = acc + (sh_ref[off % SUBLANES, pl.ds(off - off % SUBLANES + r0, sub_rows), :]
                                 * w_ref[pl.ds(j, 1), :])
                mu = jnp.mean(acc, axis=-1, keepdims=True)
                d = acc - mu
                var = jnp.mean(d * d, axis=-1, keepdims=True)
                of = d * lax.rsqrt(var + LN_EPS) * lg_ref[...] + lb_ref[...]
                o_ref[s, pl.ds(t0 + r0, sub_rows), :] = (of * jax.nn.sigmoid(of)).astype(o_ref.dtype)
            return 0
        lax.fori_loop(0, n_chunks, chunk_body, 0)
        return 0
    lax.fori_loop(0, n_seq, seq_body, 0)


def _conv(h, hist, conv_w, conv_b, ln_g, ln_b, n_seq, rows):
    b, t, _ = h.shape
    sub_rows = min(rows, 32)
    kern = functools.partial(_conv_kernel, rows=rows, sub_rows=sub_rows, n_chunks=t // rows,
                             n_seq=n_seq)
    vec = pl.BlockSpec((1, C_CONV), lambda i: (0, 0))
    return pl.pallas_call(
        kern,
        grid=(b // n_seq,),
        in_specs=[pl.BlockSpec((n_seq, t, C_CONV), lambda i: (i, 0, 0)),
                  pl.BlockSpec((n_seq, CONV_PAD, C_CONV), lambda i: (i, 0, 0)),
                  pl.BlockSpec((CONV_WIDTH, C_CONV), lambda i: (0, 0)),
                  vec, vec, vec],
        out_specs=pl.BlockSpec((n_seq, t, C_CONV), lambda i: (i, 0, 0)),
        out_shape=jax.ShapeDtypeStruct((b, t, C_CONV), BF16),
        scratch_shapes=[pltpu.VMEM((rows + CONV_PAD, C_CONV), F32),
                        pltpu.VMEM((SUBLANES, rows + CONV_PAD, C_CONV), F32)],
        compiler_params=_cparams("parallel"),
        name="conv",
    )(h, hist, conv_w, conv_b, ln_g, ln_b)


def _prep_kernel(z_ref, bnd_ref, halo_ref, mu_ref, w0_ref, wd_ref, a0_ref, wa_ref, wg_ref, kk_ref,
                 ka_ref, rk_ref, ones_ref,
                 r_ref, w_ref, k_ref, v_ref, nkk_ref, b_ref, bonus_ref, g_ref,
                 *, period, tiles_per_seq):
    z = z_ref[...]
    row_id = lax.broadcasted_iota(jnp.int32, z.shape, 0)
    bnd = bnd_ref[...]
    if tiles_per_seq > 1:
        first_tile = pl.program_id(0) % tiles_per_seq == 0
        bnd = jnp.where(first_tile, bnd, halo_ref[pl.ds(SUBLANES - 1, 1), :])
    z_prev = jnp.where(row_id % period == 0, bnd, pltpu.roll(z, 1, axis=0))
    zs = z + (z_prev - z) * mu_ref[...]
    r = zs[:, :C_RWKV]
    k = zs[:, C_RWKV:2 * C_RWKV]
    v = zs[:, 2 * C_RWKV:3 * C_RWKV]
    wa = zs[:, 3 * C_RWKV:3 * C_RWKV + LANES]
    gl = zs[:, 3 * C_RWKV + LANES:]
    ones_bd = ones_ref[...]

    lw = w0_ref[...] + jnp.dot(jnp.tanh(wa).astype(BF16), wd_ref[...], preferred_element_type=F32)
    w_log = -(jnp.maximum(-lw, 0.0) + jnp.log(1.0 + jnp.exp(-jnp.abs(lw)))) - 0.5
    decay = jnp.exp(-jnp.exp(w_log))
    a = jax.nn.sigmoid(a0_ref[...] + jnp.dot(wa.astype(BF16), wa_ref[...],
                                             preferred_element_type=F32))
    g = jnp.dot(jax.nn.sigmoid(gl).astype(BF16), wg_ref[...], preferred_element_type=F32)

    kk = k * kk_ref[...]
    norm = jnp.maximum(jnp.sqrt(_segsum(kk * kk, ones_bd)), 1e-12)
    kk = kk / norm
    k_mod = k * (1.0 + (a - 1.0) * ka_ref[...])
    bonus = _segsum(r * k_mod * rk_ref[...], ones_bd) * v

    def store_cols(ref, x):
        xt = x.T
        ntok = ref.shape[2]
        for q in range(ref.shape[0]):
            ref[q] = xt[:, q * ntok:(q + 1) * ntok]

    store_cols(r_ref, r)
    store_cols(w_ref, decay)
    store_cols(k_ref, k_mod)
    store_cols(v_ref, v)
    store_cols(nkk_ref, -kk)
    store_cols(b_ref, kk * a)
    bonus_ref[...] = bonus
    g_ref[...] = g


def _col_major_spec(rows, length):
    if rows <= length:
        per_b = length // rows
        return pl.BlockSpec((1, C_RWKV, rows), lambda i: (i // per_b, 0, i % per_b))
    return pl.BlockSpec((rows // length, C_RWKV, length), lambda i: (i, 0, 0))


def _prep(z2d, bnd, params, rows, period, tiles_per_seq):
    n = z2d.shape[0]
    length = n // WKV_BATCH
    (mu, w0, wd_pad, a0, wa_pad, wg, k_k, k_a, r_k, ones_bd) = params
    kern = functools.partial(_prep_kernel, period=period, tiles_per_seq=tiles_per_seq)
    halo_blocks = rows // SUBLANES
    vec = lambda c: pl.BlockSpec((1, c), lambda i: (0, 0))
    mat = lambda a: pl.BlockSpec(a.shape, lambda i: (0, 0))
    row_spec = pl.BlockSpec((rows, C_RWKV), lambda i: (i, 0))
    row_sd = jax.ShapeDtypeStruct((n, C_RWKV), F32)
    col_spec = _col_major_spec(rows, length)
    col_sd = jax.ShapeDtypeStruct((WKV_BATCH, C_RWKV, length), F32)
    return pl.pallas_call(
        kern,
        grid=(n // rows,),
        in_specs=[pl.BlockSpec((rows, D_SHIFT), lambda i: (i, 0)),
                  pl.BlockSpec((None, bnd.shape[1], D_SHIFT), lambda i: (i // tiles_per_seq, 0, 0)),
                  pl.BlockSpec((SUBLANES, D_SHIFT),
                               lambda i: (jnp.maximum(i * halo_blocks - 1, 0), 0)),
                  vec(D_SHIFT), vec(C_RWKV), mat(wd_pad), vec(C_RWKV), mat(wa_pad), mat(wg),
                  vec(C_RWKV), vec(C_RWKV), vec(C_RWKV), mat(ones_bd)],
        out_specs=[col_spec] * 6 + [row_spec] * 2,
        out_shape=[col_sd] * 6 + [row_sd] * 2,
        compiler_params=_cparams("parallel"),
        name="rwkv_prep",
    )(z2d, bnd, z2d, mu, w0, wd_pad, a0, wa_pad, wg, k_k, k_a, r_k, ones_bd)


def _fold_halves(x):
    return x + pltpu.roll(x, BH_PER_GROUP, axis=x.ndim - 1)


def _key_cols_to_lanes(src_ref, dst_ref):
    chunk = 2 * N_HEADS
    for kp in range(K_HALF):
        pieces = [src_ref[b8, pl.ds(kp * chunk + half * N_HEADS, N_HEADS), :]
                  for half in range(2) for b8 in range(WKV_BATCH)]
        dst_ref[kp, :LOAD_ROWS, :] = jnp.concatenate(pieces, axis=0).T


def _value_cols_to_lanes(src_ref, dst_ref):
    for v in range(HEAD_DIM):
        pieces = [src_ref[b8, pl.ds(v * N_HEADS, N_HEADS), :]
                  for _ in range(2) for b8 in range(WKV_BATCH)]
        dst_ref[pl.ds(v, LOAD_ROWS, stride=HEAD_DIM), :] = jnp.concatenate(pieces, axis=0).T


def _lanes_to_value_cols(y_s, yt_ref):
    for v in range(HEAD_DIM):
        yt = y_s[pl.ds(v, LOAD_ROWS, stride=HEAD_DIM), :].T
        folded = yt[:BH_PER_GROUP] + yt[BH_PER_GROUP:]
        for b8 in range(WKV_BATCH):
            yt_ref[b8, pl.ds(v * N_HEADS, N_HEADS), :] = folded[b8 * N_HEADS:(b8 + 1) * N_HEADS]


def _wkv_kernel(w_ref, nkk_ref, b_ref, k_ref, r_ref, v_ref, s0_ref, y_ref, sout_ref,
                w_s, nkk_s, b_s, k_s, r_s, v_s, y_s, s_ref,
                *, t_blk, steps_per_load, carry_state, n_parts):
    j = pl.program_id(0)
    sub = j % steps_per_load
    part = HEAD_DIM // n_parts
    parts = [slice(i * part, (i + 1) * part) for i in range(n_parts)]

    @pl.when(sub == 0)
    def _():
        for src, dst in ((w_ref, w_s), (nkk_ref, nkk_s), (b_ref, b_s), (k_ref, k_s), (r_ref, r_s)):
            _key_cols_to_lanes(src, dst)
        _value_cols_to_lanes(v_ref, v_s)

    if carry_state:
        @pl.when(j == 0)
        def _():
            s_ref[...] = s0_ref[0]
    else:
        s_ref[...] = s0_ref[0]

    tok0 = sub * t_blk
    acc = jnp.zeros((HEAD_DIM, LANES), F32)
    for kp in range(K_HALF):
        acc = acc + s_ref[kp] * nkk_s[kp, pl.ds(tok0, 1), :]
    sa_init = tuple(_fold_halves(acc[p, :]) for p in parts[:-1]) + (acc[parts[-1], :],)

    def part_step(tok, nxt_tok, rows, sa):
        cur = pl.ds(tok, 1)
        vrows = pl.ds(pl.multiple_of(tok * HEAD_DIM, HEAD_DIM) + rows.start, part)
        vv = v_s[vrows, :]
        yacc = jnp.zeros((part, LANES), F32)
        sacc = jnp.zeros((part, LANES), F32)
        for kp in range(K_HALF):
            s = (s_ref[kp, rows, :] * w_s[kp, cur, :] + sa * b_s[kp, cur, :]
                 + vv * k_s[kp, cur, :])
            s_ref[kp, rows, :] = s
            yacc = yacc + s * r_s[kp, cur, :]
            sacc = sacc + s * nkk_s[kp, pl.ds(nxt_tok, 1), :]
        y_s[vrows, :] = yacc
        return sacc

    def step(t, sa_parts):
        tok = tok0 + t
        nxt_tok = tok0 + jnp.minimum(t + 1, t_blk - 1)
        sa_last = _fold_halves(sa_parts[-1])
        nxt = [_fold_halves(part_step(tok, nxt_tok, rows, sa))
               for rows, sa in zip(parts[:-1], sa_parts[:-1])]
        return tuple(nxt) + (part_step(tok, nxt_tok, parts[-1], sa_last),)

    lax.fori_loop(0, t_blk, step, sa_init)

    if carry_state:
        @pl.when(j == pl.num_programs(0) - 1)
        def _():
            sout_ref[0] = s_ref[...]
    else:
        sout_ref[0] = s_ref[...]

    @pl.when(sub == steps_per_load - 1)
    def _():
        _lanes_to_value_cols(y_s, y_ref)


def _wkv(w, nkk, b, k, r, v, s0, *, t_blk, carry_state):
    length = w.shape[2]
    steps_per_load = LOAD_ROWS // t_blk
    n_steps = length // t_blk
    tok_spec = pl.BlockSpec((WKV_BATCH, C_RWKV, LOAD_ROWS), lambda j: (0, 0, j // steps_per_load))
    state_blk = (1, K_HALF, HEAD_DIM, LANES)
    if carry_state:
        sspec = pl.BlockSpec(state_blk, lambda j: (0, 0, 0, 0))
    else:
        sspec = pl.BlockSpec(state_blk, lambda j: (j, 0, 0, 0))
    key_rows = pltpu.VMEM((K_HALF, LOAD_ROWS + SUBLANES, LANES), F32)
    val_rows = pltpu.VMEM((LOAD_ROWS * HEAD_DIM, LANES), F32)
    return pl.pallas_call(
        functools.partial(_wkv_kernel, t_blk=t_blk, steps_per_load=steps_per_load,
                          carry_state=carry_state, n_parts=WKV_PARTS),
        grid=(n_steps,),
        in_specs=[tok_spec] * 6 + [sspec],
        out_specs=[tok_spec, sspec],
        out_shape=[jax.ShapeDtypeStruct(w.shape, F32), jax.ShapeDtypeStruct(s0.shape, F32)],
        scratch_shapes=[key_rows] * 5 + [val_rows, val_rows,
                                         pltpu.VMEM((K_HALF, HEAD_DIM, LANES), F32)],
        compiler_params=_cparams("arbitrary"),
        name="wkv",
    )(w, nkk, b, k, r, v, s0)


def _state_to_lanes(s, n_grp):
    s = s.reshape(WKV_BATCH, n_grp, N_HEADS, HEAD_DIM, 2, K_HALF)
    return s.transpose(1, 5, 3, 4, 0, 2).reshape(n_grp, K_HALF, HEAD_DIM, LANES)


def _state_from_lanes(s, n_grp):
    s = s.reshape(n_grp, K_HALF, HEAD_DIM, 2, WKV_BATCH, N_HEADS)
    return s.transpose(4, 0, 5, 2, 3, 1).reshape(n_grp * WKV_BATCH, N_HEADS, HEAD_DIM, HEAD_DIM)


def _outproj_kernel(x_ref, c_ref, y_ref, bonus_ref, g_ref, gng_ref, gnb_ref, ones_ref,
                    wo_c_ref, wo_r_ref, n2_ref, wr_hi_ref, wr_lo_ref, br_ref,
                    h_ref, xn_ref, route_ref, cnt_ref):
    ones_bd = ones_ref[...]
    y = jnp.concatenate([y_ref[q].T for q in range(y_ref.shape[0])], axis=0)
    inv_n = 1.0 / HEAD_DIM
    mu = _segsum(y, ones_bd) * inv_n
    d = y - mu
    var = _segsum(d * d, ones_bd) * inv_n
    yn = d * lax.rsqrt(var + GN_EPS) * gng_ref[...] + gnb_ref[...]
    rw = (yn + bonus_ref[...]) * g_ref[...]
    h = (x_ref[...]
         + jnp.dot(c_ref[...], wo_c_ref[...], preferred_element_type=F32)
         + jnp.dot(rw.astype(BF16), wo_r_ref[...], preferred_element_type=F32))
    h_ref[...] = h
    xn = h * lax.rsqrt(jnp.mean(h * h, axis=-1, keepdims=True) + RMS_EPS) * n2_ref[...]
    xn_ref[...] = xn

    x_hi, x_lo = _split_bf16(xn)
    logits = (jnp.dot(x_hi, wr_hi_ref[...], preferred_element_type=F32)
              + jnp.dot(x_lo, wr_hi_ref[...], preferred_element_type=F32)
              + jnp.dot(x_hi, wr_lo_ref[...], preferred_element_type=F32)) + br_ref[...]

    col = lax.broadcasted_iota(jnp.int32, logits.shape, 1)
    neg = jnp.float32(-jnp.inf)
    big = jnp.int32(ROUTE_COLS)
    is_g = col < N_EXPERT_GROUPS
    g_logit = jnp.where(is_g, logits, neg)
    g_max = jnp.max(g_logit, axis=-1, keepdims=True)
    g_idx = jnp.min(jnp.where(g_logit == g_max, col, big), axis=-1, keepdims=True)
    p_group = 1.0 / jnp.sum(jnp.where(is_g, jnp.exp(logits - g_max), 0.0), axis=-1, keepdims=True)
    lo_col = EXPERT_COL0 + g_idx * EXPERTS_PER_GROUP
    in_grp = (col >= lo_col) & (col < lo_col + EXPERTS_PER_GROUP)
    e1 = jnp.where(in_grp, logits, neg)
    m1 = jnp.max(e1, axis=-1, keepdims=True)
    i1 = jnp.min(jnp.where(e1 == m1, col, big), axis=-1, keepdims=True)
    e2 = jnp.where(col == i1, neg, e1)
    m2 = jnp.max(e2, axis=-1, keepdims=True)
    i2 = jnp.min(jnp.where(e2 == m2, col, big), axis=-1, keepdims=True)
    ex = jnp.exp(m2 - m1)
    w1 = p_group / (1.0 + ex)
    w2 = p_group * ex / (1.0 + ex)

    exp1 = i1 - EXPERT_COL0
    exp2 = i2 - EXPERT_COL0
    oh1 = jnp.where(col == exp1, 1.0, 0.0)
    oh2 = jnp.where(col == exp2, 1.0, 0.0)
    tm = logits.shape[0]
    earlier = jnp.where(lax.broadcasted_iota(jnp.int32, (tm, tm), 0)
                        > lax.broadcasted_iota(jnp.int32, (tm, tm), 1), 1.0, 0.0).astype(BF16)
    before1 = jnp.dot(earlier, oh1.astype(BF16), preferred_element_type=F32)
    before2 = jnp.dot(earlier, oh2.astype(BF16), preferred_element_type=F32)
    cnt1 = jnp.sum(oh1, axis=0, keepdims=True)
    cnt2 = jnp.sum(oh2, axis=0, keepdims=True)
    rank1 = jnp.sum(before1 * oh1, axis=-1, keepdims=True)
    rank2 = jnp.sum((before2 + cnt1) * oh2, axis=-1, keepdims=True)
    fields = (exp1.astype(F32), exp2.astype(F32), w1, w2, rank1, rank2)
    route = jnp.zeros_like(logits)
    for c, val in enumerate(fields):
        route = jnp.where(col == c, val, route)
    route_ref[...] = route
    cnt_ref[...] = jnp.broadcast_to(cnt1 + cnt2, cnt_ref.shape)


def _outproj(x, conv_out, y, bonus, g, params, tm):
    n = x.shape[0]
    (gn_g, gn_b, ones_bd, wo_c, wo_r, n2_g, wr_hi, wr_lo, br) = params
    row = lambda c: pl.BlockSpec((tm, c), lambda i: (i, 0))
    vec = lambda c: pl.BlockSpec((1, c), lambda i: (0, 0))
    mat = lambda a: pl.BlockSpec(a.shape, lambda i: (0, 0))
    return pl.pallas_call(
        _outproj_kernel,
        grid=(n // tm,),
        in_specs=[row(D_MODEL), row(C_CONV), _col_major_spec(tm, n // WKV_BATCH), row(C_RWKV),
                  row(C_RWKV),
                  vec(C_RWKV), vec(C_RWKV), mat(ones_bd), mat(wo_c), mat(wo_r), vec(D_MODEL),
                  mat(wr_hi), mat(wr_lo), vec(ROUTE_COLS)],
        out_specs=[row(D_MODEL), row(D_MODEL), row(ROUTE_COLS),
                   pl.BlockSpec((None, SUBLANES, ROUTE_COLS), lambda i: (i, 0, 0))],
        out_shape=[jax.ShapeDtypeStruct((n, D_MODEL), F32),
                   jax.ShapeDtypeStruct((n, D_MODEL), F32),
                   jax.ShapeDtypeStruct((n, ROUTE_COLS), F32),
                   jax.ShapeDtypeStruct((n // tm, SUBLANES, ROUTE_COLS), F32)],
        compiler_params=_cparams("parallel"),
        name="outproj_route",
    )(x, conv_out, y, bonus, g, gn_g, gn_b, ones_bd, wo_c, wo_r, n2_g, wr_hi, wr_lo, br)


def _moe_plan(route, cnt, tm):
    n = route.shape[0]
    ids = route[:, 0:2].astype(jnp.int32)
    ranks = route[:, 4:6].astype(jnp.int32)
    tile_cnt = cnt[:, 0, :N_EXPERTS].astype(jnp.int32)
    before_tile = jnp.cumsum(tile_cnt, axis=0) - tile_cnt
    total = jnp.sum(tile_cnt, axis=0)
    tiles_e = (total + MOE_ROWS - 1) // MOE_ROWS
    tile_end = jnp.cumsum(tiles_e)
    row_start = (tile_end - tiles_e) * MOE_ROWS
    tok_tile = jnp.arange(n, dtype=jnp.int32)[:, None] // tm
    pos = row_start[ids] + before_tile[tok_tile, ids] + ranks
    n_row_tiles = 2 * n // MOE_ROWS + N_EXPERTS
    t = jnp.arange(n_row_tiles, dtype=jnp.int32)
    tile_expert = jnp.minimum(jnp.sum((t[:, None] >= tile_end[None, :]).astype(jnp.int32), axis=1),
                              N_EXPERTS - 1)
    tile_used = (t < tile_end[-1]).astype(jnp.int32)
    return pos.reshape(-1), tile_expert.astype(jnp.int32), tile_used


def _row_copy_wait(src_ref, dst_ref, sem, rows):
    pltpu.make_async_copy(src_ref.at[pl.ds(0, rows), :], dst_ref.at[pl.ds(0, rows), :], sem).wait()


def _dispatch_kernel(pos_ref, x_ref, xs_init_ref, xs_ref, sem, *, tm):
    del xs_init_ref
    base = pl.program_id(0) * (2 * tm)

    def issue(r, _):
        src = x_ref.at[pl.ds(r, 1), :]
        for c in range(2):
            dst = xs_ref.at[pl.ds(pos_ref[base + 2 * r + c], 1), :]
            pltpu.make_async_copy(src, dst, sem).start()
        return 0
    lax.fori_loop(0, tm, issue, 0)
    for _ in range(2):
        _row_copy_wait(x_ref, xs_ref, sem, tm)


def _dispatch(pos, xn, n_rows, tm):
    n = xn.shape[0]
    xs_init = jnp.zeros((n_rows, D_MODEL), F32)
    return pl.pallas_call(
        functools.partial(_dispatch_kernel, tm=tm),
        grid_spec=pltpu.PrefetchScalarGridSpec(
            num_scalar_prefetch=1,
            grid=(n // tm,),
            in_specs=[pl.BlockSpec((tm, D_MODEL), lambda i, pos: (i, 0)),
                      pl.BlockSpec(memory_space=pl.ANY)],
            out_specs=pl.BlockSpec(memory_space=pl.ANY),
            scratch_shapes=[pltpu.SemaphoreType.DMA(())]),
        out_shape=jax.ShapeDtypeStruct((n_rows, D_MODEL), F32),
        input_output_aliases={2: 0},
        compiler_params=_cparams("arbitrary"),
        name="moe_dispatch",
    )(pos, xn, xs_init)


def _experts_kernel(expert_ref, used_ref, xs_ref, wg_ref, wu_ref, wd_ref, y_ref):
    del expert_ref
    t = pl.program_id(0)

    @pl.when(used_ref[t] != 0)
    def _():
        x = xs_ref[...].astype(BF16)
        hg = jnp.dot(x, wg_ref[0].astype(BF16), preferred_element_type=F32)
        hu = jnp.dot(x, wu_ref[0].astype(BF16), preferred_element_type=F32)
        act = hg * jax.nn.sigmoid(hg) * hu
        y_ref[...] = jnp.dot(act.astype(BF16), wd_ref[0].astype(BF16), preferred_element_type=F32)

    @pl.when(used_ref[t] == 0)
    def _():
        y_ref[...] = jnp.zeros_like(y_ref)


def _experts(tile_expert, tile_used, xs, w_gate, w_up, w_down):
    n_rows = xs.shape[0]
    return pl.pallas_call(
        _experts_kernel,
        grid_spec=pltpu.PrefetchScalarGridSpec(
            num_scalar_prefetch=2,
            grid=(n_rows // MOE_ROWS,),
            in_specs=[pl.BlockSpec((MOE_ROWS, D_MODEL), lambda t, ex, used: (t, 0)),
                      pl.BlockSpec((1, D_MODEL, D_EXPERT), lambda t, ex, used: (ex[t], 0, 0)),
                      pl.BlockSpec((1, D_MODEL, D_EXPERT), lambda t, ex, used: (ex[t], 0, 0)),
                      pl.BlockSpec((1, D_EXPERT, D_MODEL), lambda t, ex, used: (ex[t], 0, 0))],
            out_specs=pl.BlockSpec((MOE_ROWS, D_MODEL), lambda t, ex, used: (t, 0))),
        out_shape=jax.ShapeDtypeStruct((n_rows, D_MODEL), F32),
        compiler_params=_cparams("arbitrary"),
        name="moe_experts",
    )(tile_expert, tile_used, xs, w_gate, w_up, w_down)


def _combine_kernel(pos_ref, h_ref, route_ref, nf_ref, ys_ref, o_ref, y1_ref, y2_ref, sem, *, tm):
    base = pl.program_id(0) * (2 * tm)

    def issue(r, _):
        for c, buf in enumerate((y1_ref, y2_ref)):
            src = ys_ref.at[pl.ds(pos_ref[base + 2 * r + c], 1), :]
            pltpu.make_async_copy(src, buf.at[pl.ds(r, 1), :], sem).start()
        return 0
    lax.fori_loop(0, tm, issue, 0)
    for buf in (y1_ref, y2_ref):
        _row_copy_wait(ys_ref, buf, sem, tm)

    route = route_ref[...]
    col = lax.broadcasted_iota(jnp.int32, route.shape, 1)
    w1 = jnp.sum(jnp.where(col == 2, route, 0.0), axis=-1, keepdims=True)
    w2 = jnp.sum(jnp.where(col == 3, route, 0.0), axis=-1, keepdims=True)
    h = h_ref[...] + w1 * y1_ref[...] + w2 * y2_ref[...]
    o_ref[...] = h * lax.rsqrt(jnp.mean(h * h, axis=-1, keepdims=True) + RMS_EPS) * nf_ref[...]


def _combine(pos, h, route, nf_g, ys, tm):
    n = h.shape[0]
    return pl.pallas_call(
        functools.partial(_combine_kernel, tm=tm),
        grid_spec=pltpu.PrefetchScalarGridSpec(
            num_scalar_prefetch=1,
            grid=(n // tm,),
            in_specs=[pl.BlockSpec((tm, D_MODEL), lambda i, pos: (i, 0)),
                      pl.BlockSpec((tm, ROUTE_COLS), lambda i, pos: (i, 0)),
                      pl.BlockSpec((1, D_MODEL), lambda i, pos: (0, 0)),
                      pl.BlockSpec(memory_space=pl.ANY)],
            out_specs=pl.BlockSpec((tm, D_MODEL), lambda i, pos: (i, 0)),
            scratch_shapes=[pltpu.VMEM((tm, D_MODEL), F32), pltpu.VMEM((tm, D_MODEL), F32),
                            pltpu.SemaphoreType.DMA(())]),
        out_shape=jax.ShapeDtypeStruct((n, D_MODEL), F32),
        compiler_params=_cparams("arbitrary"),
        name="moe_combine",
    )(pos, h, route, nf_g, ys)


def _moe(xn, route, cnt, h, w_gate, w_up, w_down, nf_g, tm):
    pos, tile_expert, tile_used = _moe_plan(route, cnt, tm)
    xs = _dispatch(pos, xn, tile_expert.shape[0] * MOE_ROWS, tm)
    ys = _experts(tile_expert, tile_used, xs, w_gate, w_up, w_down)
    return _combine(pos, h, route, nf_g, ys, tm)


def _layer(x, conv_buf, shift_buf, wkv_state, lw, *, tm, conv_seqs, conv_rows, prep_rows,
           wkv_tblk):
    b, t, _ = x.shape
    n = b * t
    n_grp = b // WKV_BATCH
    x2d = x.reshape(n, D_MODEL)

    h_glu, z = _inproj(x2d, lw["norm1_g"], lw["w_in"], tm)

    h3 = h_glu.reshape(b, t, C_CONV)
    hist = jnp.concatenate([jnp.zeros((b, CONV_PAD - CONV_WIDTH + 1, C_CONV), F32), conv_buf], axis=1)
    n_hist = CONV_WIDTH - 1
    new_conv = h3[:, t - n_hist:] if t >= n_hist else jnp.concatenate([conv_buf[:, t:], h3], axis=1)
    conv_out = _conv(h3, hist, lw["conv_w"], lw["conv_b"], lw["conv_ln_g"], lw["conv_ln_b"],
                     conv_seqs, conv_rows)

    new_shift = z.reshape(b, t, D_SHIFT)[:, -1][:, INV_PERM_Z]
    shift_buf = shift_buf[:, PERM_Z]
    if prep_rows > t:
        period, tiles_per_seq = t, 1
        bnd = jnp.repeat(shift_buf, t, axis=0).reshape(n // prep_rows, prep_rows, D_SHIFT)
    else:
        period, tiles_per_seq = prep_rows, t // prep_rows
        bnd = shift_buf.reshape(b, 1, D_SHIFT)
    r, w, k, v, nkk, bb, bonus, g = _prep(z, bnd, lw["prep"], prep_rows, period, tiles_per_seq)

    y_cols, s_l = _wkv(w, nkk, bb, k, r, v, _state_to_lanes(wkv_state, n_grp),
                       t_blk=wkv_tblk, carry_state=(n_grp == 1))
    new_wkv = _state_from_lanes(s_l, n_grp)

    h, xn2, route, cnt = _outproj(x2d, conv_out.reshape(n, C_CONV), y_cols, bonus, g,
                                  lw["outproj"], tm)
    out = _moe(xn2, route, cnt, h, lw["w_gate"], lw["w_up"], lw["w_down"], lw["norm_f_g"], tm)
    return out.reshape(b, t, D_MODEL), new_conv, new_shift, new_wkv


def _prepare_weights(norm1_g, w_in, conv_w, conv_b, conv_ln_g, conv_ln_b, mu_shift, w0,
                     w_decay_up, a0, w_aaa_up, w_gate_up, k_k, k_a, r_k, gn_g, gn_b, w_out,
                     norm2_g, w_router_group, b_router_group, w_router_expert, b_router_expert,
                     w_exp_gate, w_exp_up, w_exp_down, norm_f_g):
    row = lambda a: a.reshape(1, -1)
    head = jnp.arange(C_RWKV) % N_HEADS
    ones_bd = (head[:, None] == head[None, :]).astype(BF16)
    w_in_perm = jnp.concatenate([w_in[:, :2 * C_CONV], w_in[:, 2 * C_CONV:][:, PERM_Z]], axis=1)
    zeros_lora = jnp.zeros((LANES - w_decay_up.shape[0], C_RWKV), F32)
    wd_pad = jnp.concatenate([w_decay_up[:, PERM_KEY], zeros_lora], axis=0).astype(BF16)
    wa_pad = jnp.concatenate([zeros_lora, w_aaa_up[:, PERM_KEY]], axis=0).astype(BF16)
    pad_cols = ROUTE_COLS - N_EXPERT_GROUPS - N_EXPERTS
    w_route = jnp.concatenate([w_router_group, w_router_expert,
                               jnp.zeros((D_MODEL, pad_cols), F32)], axis=1)
    wr_hi, wr_lo = _split_bf16(w_route)
    b_route = jnp.concatenate([b_router_group, b_router_expert, jnp.zeros((pad_cols,), F32)])
    return {
        "norm1_g": row(norm1_g), "w_in": w_in_perm.astype(BF16),
        "conv_w": conv_w, "conv_b": row(conv_b), "conv_ln_g": row(conv_ln_g),
        "conv_ln_b": row(conv_ln_b),
        "prep": (row(mu_shift[PERM_Z]), row(w0[PERM_KEY]), wd_pad, row(a0[PERM_KEY]), wa_pad,
                 w_gate_up[:, PERM_VAL].astype(BF16), row(k_k[PERM_KEY]), row(k_a[PERM_KEY]),
                 row(r_k[PERM_KEY]), ones_bd),
        "outproj": (row(gn_g[PERM_VAL]), row(gn_b[PERM_VAL]), ones_bd, w_out[:C_CONV].astype(BF16),
                    w_out[C_CONV:][PERM_VAL].astype(BF16), row(norm2_g), wr_hi, wr_lo, row(b_route)),
        "w_gate": w_exp_gate.reshape(N_EXPERTS, D_MODEL, D_EXPERT),
        "w_up": w_exp_up.reshape(N_EXPERTS, D_MODEL, D_EXPERT),
        "w_down": w_exp_down.reshape(N_EXPERTS, D_EXPERT, D_MODEL),
        "norm_f_g": row(norm_f_g),
    }


def kernel(x_prompt, x_sample, state_conv, state_shift, state_wkv, norm1_g, w_in, conv_w, conv_b, conv_ln_g, conv_ln_b, mu_shift, w0, w_decay_up, a0, w_aaa_up, w_gate_up, k_k, k_a, r_k, gn_g, gn_b, w_out, norm2_g, w_router_group, b_router_group, w_router_expert, b_router_expert, w_exp_gate, w_exp_up, w_exp_down, norm_f_g):
    depth = norm1_g.shape[0]
    assert depth == 1, "single-layer trunk"
    b = x_prompt.shape[0]
    assert b == WKV_BATCH and x_prompt.shape[1] % LOAD_ROWS == 0
    assert x_sample.shape[0] * x_sample.shape[1] == WKV_BATCH * LOAD_ROWS
    lw = _prepare_weights(norm1_g[0], w_in[0], conv_w[0], conv_b[0], conv_ln_g[0], conv_ln_b[0],
                          mu_shift[0], w0[0], w_decay_up[0], a0[0], w_aaa_up[0], w_gate_up[0],
                          k_k[0], k_a[0], r_k[0], gn_g[0], gn_b[0], w_out[0], norm2_g[0],
                          w_router_group[0], b_router_group[0], w_router_expert[0],
                          b_router_expert[0], w_exp_gate[0], w_exp_up[0], w_exp_down[0], norm_f_g)
    dt = x_prompt.dtype
    y_p, conv_p, shift_p, wkv_p = _layer(
        x_prompt,
        jnp.zeros((b, CONV_WIDTH - 1, C_CONV), dt),
        jnp.zeros((b, D_SHIFT), dt),
        jnp.zeros((b, N_HEADS, HEAD_DIM, HEAD_DIM), dt), lw,
        tm=512, conv_seqs=1, conv_rows=128, prep_rows=256, wkv_tblk=LOAD_ROWS)
    y_s, conv_s, shift_s, wkv_s = _layer(
        x_sample, state_conv[0], state_shift[0], state_wkv[0], lw,
        tm=512, conv_seqs=16, conv_rows=8, prep_rows=256, wkv_tblk=x_sample.shape[1])
    return (y_p, y_s, conv_p[None], shift_p[None], wkv_p[None],
            conv_s[None], shift_s[None], wkv_s[None])
```

```python
import functools

import numpy as np
import jax
import jax.numpy as jnp
from jax import lax
from jax.experimental import pallas as pl
from jax.experimental.pallas import tpu as pltpu

F32 = jnp.float32
BF16 = jnp.bfloat16

D_MODEL = 1024
C_CONV = 512
C_RWKV = 512
HEAD_DIM = 64
N_HEADS = 8
CONV_WIDTH = 31
D_SHIFT = 1792
N_EXPERT_GROUPS = 4
EXPERTS_PER_GROUP = 8
N_EXPERTS = N_EXPERT_GROUPS * EXPERTS_PER_GROUP
D_EXPERT = 256
RMS_EPS = 1e-6
LN_EPS = 1e-5
GN_EPS = 64e-5

LANES = 128
SUBLANES = 8
CONV_PAD = 32
WKV_BATCH = 8
BH_PER_GROUP = WKV_BATCH * N_HEADS
K_HALF = HEAD_DIM // 2
WKV_PARTS = 2
LOAD_ROWS = LANES
ROUTE_COLS = LANES
EXPERT_COL0 = N_EXPERT_GROUPS
MOE_ROWS = 512
DMA_ISSUE_UNROLL = 8
VMEM_LIMIT = 56 * 1024 * 1024


def _key_perm():
    kp, kh, h = np.meshgrid(np.arange(K_HALF), np.arange(2), np.arange(N_HEADS), indexing="ij")
    return (h * HEAD_DIM + kh * K_HALF + kp).reshape(-1)


def _value_perm():
    v, h = np.meshgrid(np.arange(HEAD_DIM), np.arange(N_HEADS), indexing="ij")
    return (h * HEAD_DIM + v).reshape(-1)


PERM_KEY = _key_perm()
PERM_VAL = _value_perm()
PERM_Z = np.concatenate([PERM_KEY, C_RWKV + PERM_KEY, 2 * C_RWKV + PERM_VAL,
                         np.arange(3 * C_RWKV, D_SHIFT)])
INV_PERM_Z = np.argsort(PERM_Z)


def _cparams(*sem):
    return pltpu.CompilerParams(dimension_semantics=sem, vmem_limit_bytes=VMEM_LIMIT)


def _split_bf16(x):
    hi = x.astype(BF16)
    lo = (x - hi.astype(F32)).astype(BF16)
    return hi, lo


def _segsum(x, same_head):
    hi, lo = _split_bf16(x)
    s = (jnp.dot(hi, same_head, preferred_element_type=F32)
         + jnp.dot(lo, same_head, preferred_element_type=F32))
    return jnp.concatenate([s] * (x.shape[1] // LANES), axis=1)


def _inproj_kernel(x_ref, g_ref, w_ref, h_ref, z_ref):
    x = x_ref[...]
    xn = x * lax.rsqrt(jnp.mean(x * x, axis=-1, keepdims=True) + RMS_EPS) * g_ref[...]
    p = jnp.dot(xn.astype(BF16), w_ref[...], preferred_element_type=F32)
    h_ref[...] = p[:, :C_CONV] * jax.nn.sigmoid(p[:, C_CONV:2 * C_CONV])
    z_ref[...] = p[:, 2 * C_CONV:]


def _inproj(x, g, w_bf, tm):
    n = x.shape[0]
    d_in = w_bf.shape[1]
    return pl.pallas_call(
        _inproj_kernel,
        grid=(n // tm,),
        in_specs=[pl.BlockSpec((tm, D_MODEL), lambda i: (i, 0)),
                  pl.BlockSpec((1, D_MODEL), lambda i: (0, 0)),
                  pl.BlockSpec((D_MODEL, d_in), lambda i: (0, 0))],
        out_specs=[pl.BlockSpec((tm, C_CONV), lambda i: (i, 0)),
                   pl.BlockSpec((tm, D_SHIFT), lambda i: (i, 0))],
        out_shape=[jax.ShapeDtypeStruct((n, C_CONV), F32),
                   jax.ShapeDtypeStruct((n, D_SHIFT), F32)],
        compiler_params=_cparams("parallel"),
        name="inproj",
    )(x, g, w_bf)


def _conv_kernel(h_ref, hist_ref, w_ref, b_ref, lg_ref, lb_ref, o_ref, win_ref, sh_ref,
                 *, rows, sub_rows, n_chunks, n_seq):
    shifted_rows = rows + CONV_PAD - SUBLANES

    def seq_body(s, _):
        def chunk_body(c, _):
            t0 = pl.multiple_of(c * rows, SUBLANES)
            if n_chunks == 1:
                win_ref[pl.ds(0, CONV_PAD), :] = hist_ref[s]
            else:
                prev0 = pl.multiple_of(jnp.maximum(t0 - CONV_PAD, 0), SUBLANES)
                win_ref[pl.ds(0, CONV_PAD), :] = jnp.where(
                    c == 0, hist_ref[s], h_ref[s, pl.ds(prev0, CONV_PAD), :])
            win_ref[pl.ds(CONV_PAD, rows), :] = h_ref[s, pl.ds(t0, rows), :]
            sh_ref[0] = win_ref[...]
            for sft in range(1, SUBLANES):
                sh_ref[sft, pl.ds(0, shifted_rows), :] = win_ref[pl.ds(sft, shifted_rows), :]
            for blk in range(rows // sub_rows):
                r0 = blk * sub_rows
                acc = jnp.zeros((sub_rows, C_CONV), F32) + b_ref[...]
                for j in range(CONV_WIDTH):
                    off = j + 2
                    acc = acc + (sh_ref[off % SUBLANES, pl.ds(off - off % SUBLANES + r0, sub_rows), :]
                                 * w_ref[pl.ds(j, 1), :])
                mu = jnp.mean(acc, axis=-1, keepdims=True)
                d = acc - mu
                var = jnp.mean(d * d, axis=-1, keepdims=True)
                of = d * lax.rsqrt(var + LN_EPS) * lg_ref[...] + lb_ref[...]
                o_ref[s, pl.ds(t0 + r0, sub_rows), :] = (of * jax.nn.sigmoid(of)).astype(o_ref.dtype)
            return 0
        lax.fori_loop(0, n_chunks, chunk_body, 0)
        return 0
    lax.fori_loop(0, n_seq, seq_body, 0)


def _conv(h, hist, conv_w, conv_b, ln_g, ln_b, n_seq, rows):
    b, t, _ = h.shape
    sub_rows = min(rows, 32)
    kern = functools.partial(_conv_kernel, rows=rows, sub_rows=sub_rows, n_chunks=t // rows,
                             n_seq=n_seq)
    vec = pl.BlockSpec((1, C_CONV), lambda i: (0, 0))
    return pl.pallas_call(
        kern,
        grid=(b // n_seq,),
        in_specs=[pl.BlockSpec((n_seq, t, C_CONV), lambda i: (i, 0, 0)),
                  pl.BlockSpec((n_seq, CONV_PAD, C_CONV), lambda i: (i, 0, 0)),
                  pl.BlockSpec((CONV_WIDTH, C_CONV), lambda i: (0, 0)),
                  vec, vec, vec],
        out_specs=pl.BlockSpec((n_seq, t, C_CONV), lambda i: (i, 0, 0)),
        out_shape=jax.ShapeDtypeStruct((b, t, C_CONV), BF16),
        scratch_shapes=[pltpu.VMEM((rows + CONV_PAD, C_CONV), F32),
                        pltpu.VMEM((SUBLANES, rows + CONV_PAD, C_CONV), F32)],
        compiler_params=_cparams("parallel"),
        name="conv",
    )(h, hist, conv_w, conv_b, ln_g, ln_b)


def _prep_kernel(z_ref, bnd_ref, halo_ref, mu_ref, w0_ref, wd_ref, a0_ref, wa_ref, wg_ref, kk_ref,
                 ka_ref, rk_ref, ones_ref,
                 r_ref, w_ref, k_ref, v_ref, nkk_ref, b_ref, bonus_ref, g_ref,
                 *, period, tiles_per_seq):
    z = z_ref[...]
    row_id = lax.broadcasted_iota(jnp.int32, z.shape, 0)
    bnd = bnd_ref[...]
    if tiles_per_seq > 1:
        first_tile = pl.program_id(0) % tiles_per_seq == 0
        bnd = jnp.where(first_tile, bnd, halo_ref[pl.ds(SUBLANES - 1, 1), :])
    z_prev = jnp.where(row_id % period == 0, bnd, pltpu.roll(z, 1, axis=0))
    zs = z + (z_prev - z) * mu_ref[...]
    r = zs[:, :C_RWKV]
    k = zs[:, C_RWKV:2 * C_RWKV]
    v = zs[:, 2 * C_RWKV:3 * C_RWKV]
    wa = zs[:, 3 * C_RWKV:3 * C_RWKV + LANES]
    gl = zs[:, 3 * C_RWKV + LANES:]
    ones_bd = ones_ref[...]

    lw = w0_ref[...] + jnp.dot(jnp.tanh(wa).astype(BF16), wd_ref[...], preferred_element_type=F32)
    w_log = -(jnp.maximum(-lw, 0.0) + jnp.log(1.0 + jnp.exp(-jnp.abs(lw)))) - 0.5
    decay = jnp.exp(-jnp.exp(w_log))
    a = jax.nn.sigmoid(a0_ref[...] + jnp.dot(wa.astype(BF16), wa_ref[...],
                                             preferred_element_type=F32))
    g = jnp.dot(jax.nn.sigmoid(gl).astype(BF16), wg_ref[...], preferred_element_type=F32)

    kk = k * kk_ref[...]
    norm = jnp.maximum(jnp.sqrt(_segsum(kk * kk, ones_bd)), 1e-12)
    kk = kk / norm
    k_mod = k * (1.0 + (a - 1.0) * ka_ref[...])
    bonus = _segsum(r * k_mod * rk_ref[...], ones_bd) * v

    def store_cols(ref, x):
        xt = x.T
        ntok = ref.shape[2]
        for q in range(ref.shape[0]):
            ref[q] = xt[:, q * ntok:(q + 1) * ntok]

    store_cols(r_ref, r)
    store_cols(w_ref, decay)
    store_cols(k_ref, k_mod)
    store_cols(v_ref, v)
    store_cols(nkk_ref, -kk)
    store_cols(b_ref, kk * a)
    bonus_ref[...] = bonus
    g_ref[...] = g


def _col_major_spec(rows, length):
    if rows <= length:
        per_b = length // rows
        return pl.BlockSpec((1, C_RWKV, rows), lambda i: (i // per_b, 0, i % per_b))
    return pl.BlockSpec((rows // length, C_RWKV, length), lambda i: (i, 0, 0))


def _prep(z2d, bnd, params, rows, period, tiles_per_seq):
    n = z2d.shape[0]
    length = n // WKV_BATCH
    (mu, w0, wd_pad, a0, wa_pad, wg, k_k, k_a, r_k, ones_bd) = params
    kern = functools.partial(_prep_kernel, period=period, tiles_per_seq=tiles_per_seq)
    halo_blocks = rows // SUBLANES
    vec = lambda c: pl.BlockSpec((1, c), lambda i: (0, 0))
    mat = lambda a: pl.BlockSpec(a.shape, lambda i: (0, 0))
    row_spec = pl.BlockSpec((rows, C_RWKV), lambda i: (i, 0))
    row_sd = jax.ShapeDtypeStruct((n, C_RWKV), F32)
    col_spec = _col_major_spec(rows, length)
    col_sd = jax.ShapeDtypeStruct((WKV_BATCH, C_RWKV, length), F32)
    return pl.pallas_call(
        kern,
        grid=(n // rows,),
        in_specs=[pl.BlockSpec((rows, D_SHIFT), lambda i: (i, 0)),
                  pl.BlockSpec((None, bnd.shape[1], D_SHIFT), lambda i: (i // tiles_per_seq, 0, 0)),
                  pl.BlockSpec((SUBLANES, D_SHIFT),
                               lambda i: (jnp.maximum(i * halo_blocks - 1, 0), 0)),
                  vec(D_SHIFT), vec(C_RWKV), mat(wd_pad), vec(C_RWKV), mat(wa_pad), mat(wg),
                  vec(C_RWKV), vec(C_RWKV), vec(C_RWKV), mat(ones_bd)],
        out_specs=[col_spec] * 6 + [row_spec] * 2,
        out_shape=[col_sd] * 6 + [row_sd] * 2,
        compiler_params=_cparams("parallel"),
        name="rwkv_prep",
    )(z2d, bnd, z2d, mu, w0, wd_pad, a0, wa_pad, wg, k_k, k_a, r_k, ones_bd)


def _fold_halves(x):
    return x + pltpu.roll(x, BH_PER_GROUP, axis=x.ndim - 1)


def _key_cols_to_lanes(src_ref, dst_ref):
    chunk = 2 * N_HEADS
    for kp in range(K_HALF):
        pieces = [src_ref[b8, pl.ds(kp * chunk + half * N_HEADS, N_HEADS), :]
                  for half in range(2) for b8 in range(WKV_BATCH)]
        dst_ref[kp, :LOAD_ROWS, :] = jnp.concatenate(pieces, axis=0).T


def _value_cols_to_lanes(src_ref, dst_ref):
    for v in range(HEAD_DIM):
        pieces = [src_ref[b8, pl.ds(v * N_HEADS, N_HEADS), :]
                  for _ in range(2) for b8 in range(WKV_BATCH)]
        dst_ref[pl.ds(v, LOAD_ROWS, stride=HEAD_DIM), :] = jnp.concatenate(pieces, axis=0).T


def _lanes_to_value_cols(y_s, yt_ref):
    for v in range(HEAD_DIM):
        yt = y_s[pl.ds(v, LOAD_ROWS, stride=HEAD_DIM), :].T
        folded = yt[:BH_PER_GROUP] + yt[BH_PER_GROUP:]
        for b8 in range(WKV_BATCH):
            yt_ref[b8, pl.ds(v * N_HEADS, N_HEADS), :] = folded[b8 * N_HEADS:(b8 + 1) * N_HEADS]


def _wkv_kernel(w_ref, nkk_ref, b_ref, k_ref, r_ref, v_ref, s0_ref, y_ref, sout_ref,
                w_s, nkk_s, b_s, k_s, r_s, v_s, y_s, s_ref,
                *, t_blk, steps_per_load, carry_state, n_parts):
    j = pl.program_id(0)
    sub = j % steps_per_load
    part = HEAD_DIM // n_parts
    parts = [slice(i * part, (i + 1) * part) for i in range(n_parts)]

    @pl.when(sub == 0)
    def _():
        for src, dst in ((w_ref, w_s), (nkk_ref, nkk_s), (b_ref, b_s), (k_ref, k_s), (r_ref, r_s)):
            _key_cols_to_lanes(src, dst)
        _value_cols_to_lanes(v_ref, v_s)

    if carry_state:
        @pl.when(j == 0)
        def _():
            s_ref[...] = s0_ref[0]
    else:
        s_ref[...] = s0_ref[0]

    tok0 = sub * t_blk
    acc = jnp.zeros((HEAD_DIM, LANES), F32)
    for kp in range(K_HALF):
        acc = acc + s_ref[kp] * nkk_s[kp, pl.ds(tok0, 1), :]
    sa_init = tuple(_fold_halves(acc[p, :]) for p in parts[:-1]) + (acc[parts[-1], :],)

    def part_step(tok, nxt_tok, rows, sa):
        cur = pl.ds(tok, 1)
        vrows = pl.ds(pl.multiple_of(tok * HEAD_DIM, HEAD_DIM) + rows.start, part)
        vv = v_s[vrows, :]
        yacc = jnp.zeros((part, LANES), F32)
        sacc = jnp.zeros((part, LANES), F32)
        for kp in range(K_HALF):
            s = (s_ref[kp, rows, :] * w_s[kp, cur, :] + sa * b_s[kp, cur, :]
                 + vv * k_s[kp, cur, :])
            s_ref[kp, rows, :] = s
            yacc = yacc + s * r_s[kp, cur, :]
            sacc = sacc + s * nkk_s[kp, pl.ds(nxt_tok, 1), :]
        y_s[vrows, :] = yacc
        return sacc

    def step(t, sa_parts):
        tok = tok0 + t
        nxt_tok = tok0 + jnp.minimum(t + 1, t_blk - 1)
        sa_last = _fold_halves(sa_parts[-1])
        nxt = [_fold_halves(part_step(tok, nxt_tok, rows, sa))
               for rows, sa in zip(parts[:-1], sa_parts[:-1])]
        return tuple(nxt) + (part_step(tok, nxt_tok, parts[-1], sa_last),)

    lax.fori_loop(0, t_blk, step, sa_init)

    if carry_state:
        @pl.when(j == pl.num_programs(0) - 1)
        def _():
            sout_ref[0] = s_ref[...]
    else:
        sout_ref[0] = s_ref[...]

    @pl.when(sub == steps_per_load - 1)
    def _():
        _lanes_to_value_cols(y_s, y_ref)


def _wkv(w, nkk, b, k, r, v, s0, *, t_blk, carry_state):
    length = w.shape[2]
    steps_per_load = LOAD_ROWS // t_blk
    n_steps = length // t_blk
    tok_spec = pl.BlockSpec((WKV_BATCH, C_RWKV, LOAD_ROWS), lambda j: (0, 0, j // steps_per_load))
    state_blk = (1, K_HALF, HEAD_DIM, LANES)
    if carry_state:
        sspec = pl.BlockSpec(state_blk, lambda j: (0, 0, 0, 0))
    else:
        sspec = pl.BlockSpec(state_blk, lambda j: (j, 0, 0, 0))
    key_rows = pltpu.VMEM((K_HALF, LOAD_ROWS + SUBLANES, LANES), F32)
    val_rows = pltpu.VMEM((LOAD_ROWS * HEAD_DIM, LANES), F32)
    return pl.pallas_call(
        functools.partial(_wkv_kernel, t_blk=t_blk, steps_per_load=steps_per_load,
                          carry_state=carry_state, n_parts=WKV_PARTS),
        grid=(n_steps,),
        in_specs=[tok_spec] * 6 + [sspec],
        out_specs=[tok_spec, sspec],
        out_shape=[jax.ShapeDtypeStruct(w.shape, F32), jax.ShapeDtypeStruct(s0.shape, F32)],
        scratch_shapes=[key_rows] * 5 + [val_rows, val_rows,
                                         pltpu.VMEM((K_HALF, HEAD_DIM, LANES), F32)],
        compiler_params=_cparams("arbitrary"),
        name="wkv",
    )(w, nkk, b, k, r, v, s0)


def _state_to_lanes(s, n_grp):
    s = s.reshape(WKV_BATCH, n_grp, N_HEADS, HEAD_DIM, 2, K_HALF)
    return s.transpose(1, 5, 3, 4, 0, 2).reshape(n_grp, K_HALF, HEAD_DIM, LANES)


def _state_from_lanes(s, n_grp):
    s = s.reshape(n_grp, K_HALF, HEAD_DIM, 2, WKV_BATCH, N_HEADS)
    return s.transpose(4, 0, 5, 2, 3, 1).reshape(n_grp * WKV_BATCH, N_HEADS, HEAD_DIM, HEAD_DIM)


def _outproj_kernel(x_ref, c_ref, y_ref, bonus_ref, g_ref, gng_ref, gnb_ref, ones_ref,
                    wo_c_ref, wo_r_ref, n2_ref, wr_hi_ref, wr_lo_ref, br_ref,
                    h_ref, xn_ref, route_ref, cnt_ref):
    ones_bd = ones_ref[...]
    y = jnp.concatenate([y_ref[q].T for q in range(y_ref.shape[0])], axis=0)
    inv_n = 1.0 / HEAD_DIM
    mu = _segsum(y, ones_bd) * inv_n
    d = y - mu
    var = _segsum(d * d, ones_bd) * inv_n
    yn = d * lax.rsqrt(var + GN_EPS) * gng_ref[...] + gnb_ref[...]
    rw = (yn + bonus_ref[...]) * g_ref[...]
    h = (x_ref[...]
         + jnp.dot(c_ref[...], wo_c_ref[...], preferred_element_type=F32)
         + jnp.dot(rw.astype(BF16), wo_r_ref[...], preferred_element_type=F32))
    h_ref[...] = h
    xn = h * lax.rsqrt(jnp.mean(h * h, axis=-1, keepdims=True) + RMS_EPS) * n2_ref[...]
    xn_ref[...] = xn

    x_hi, x_lo = _split_bf16(xn)
    logits = (jnp.dot(x_hi, wr_hi_ref[...], preferred_element_type=F32)
              + jnp.dot(x_lo, wr_hi_ref[...], preferred_element_type=F32)
              + jnp.dot(x_hi, wr_lo_ref[...], preferred_element_type=F32)) + br_ref[...]

    col = lax.broadcasted_iota(jnp.int32, logits.shape, 1)
    neg = jnp.float32(-jnp.inf)
    big = jnp.int32(ROUTE_COLS)
    is_g = col < N_EXPERT_GROUPS
    g_logit = jnp.where(is_g, logits, neg)
    g_max = jnp.max(g_logit, axis=-1, keepdims=True)
    g_idx = jnp.min(jnp.where(g_logit == g_max, col, big), axis=-1, keepdims=True)
    p_group = 1.0 / jnp.sum(jnp.where(is_g, jnp.exp(logits - g_max), 0.0), axis=-1, keepdims=True)
    lo_col = EXPERT_COL0 + g_idx * EXPERTS_PER_GROUP
    in_grp = (col >= lo_col) & (col < lo_col + EXPERTS_PER_GROUP)
    e1 = jnp.where(in_grp, logits, neg)
    m1 = jnp.max(e1, axis=-1, keepdims=True)
    i1 = jnp.min(jnp.where(e1 == m1, col, big), axis=-1, keepdims=True)
    e2 = jnp.where(col == i1, neg, e1)
    m2 = jnp.max(e2, axis=-1, keepdims=True)
    i2 = jnp.min(jnp.where(e2 == m2, col, big), axis=-1, keepdims=True)
    ex = jnp.exp(m2 - m1)
    w1 = p_group / (1.0 + ex)
    w2 = p_group * ex / (1.0 + ex)

    exp1 = i1 - EXPERT_COL0
    exp2 = i2 - EXPERT_COL0
    oh1 = jnp.where(col == exp1, 1.0, 0.0)
    oh2 = jnp.where(col == exp2, 1.0, 0.0)
    tm = logits.shape[0]
    earlier = jnp.where(lax.broadcasted_iota(jnp.int32, (tm, tm), 0)
                        > lax.broadcasted_iota(jnp.int32, (tm, tm), 1), 1.0, 0.0).astype(BF16)
    before1 = jnp.dot(earlier, oh1.astype(BF16), preferred_element_type=F32)
    before2 = jnp.dot(earlier, oh2.astype(BF16), preferred_element_type=F32)
    cnt1 = jnp.sum(oh1, axis=0, keepdims=True)
    cnt2 = jnp.sum(oh2, axis=0, keepdims=True)
    rank1 = jnp.sum(before1 * oh1, axis=-1, keepdims=True)
    rank2 = jnp.sum((before2 + cnt1) * oh2, axis=-1, keepdims=True)
    fields = (exp1.astype(F32), exp2.astype(F32), w1, w2, rank1, rank2)
    route = jnp.zeros_like(logits)
    for c, val in enumerate(fields):
        route = jnp.where(col == c, val, route)
    route_ref[...] = route
    cnt_ref[...] = jnp.broadcast_to(cnt1 + cnt2, cnt_ref.shape)


def _outproj(x, conv_out, y, bonus, g, params, tm):
    n = x.shape[0]
    (gn_g, gn_b, ones_bd, wo_c, wo_r, n2_g, wr_hi, wr_lo, br) = params
    row = lambda c: pl.BlockSpec((tm, c), lambda i: (i, 0))
    vec = lambda c: pl.BlockSpec((1, c), lambda i: (0, 0))
    mat = lambda a: pl.BlockSpec(a.shape, lambda i: (0, 0))
    return pl.pallas_call(
        _outproj_kernel,
        grid=(n // tm,),
        in_specs=[row(D_MODEL), row(C_CONV), _col_major_spec(tm, n // WKV_BATCH), row(C_RWKV),
                  row(C_RWKV),
                  vec(C_RWKV), vec(C_RWKV), mat(ones_bd), mat(wo_c), mat(wo_r), vec(D_MODEL),
                  mat(wr_hi), mat(wr_lo), vec(ROUTE_COLS)],
        out_specs=[row(D_MODEL), row(D_MODEL), row(ROUTE_COLS),
                   pl.BlockSpec((None, SUBLANES, ROUTE_COLS), lambda i: (i, 0, 0))],
        out_shape=[jax.ShapeDtypeStruct((n, D_MODEL), F32),
                   jax.ShapeDtypeStruct((n, D_MODEL), F32),
                   jax.ShapeDtypeStruct((n, ROUTE_COLS), F32),
                   jax.ShapeDtypeStruct((n // tm, SUBLANES, ROUTE_COLS), F32)],
        compiler_params=_cparams("parallel"),
        name="outproj_route",
    )(x, conv_out, y, bonus, g, gn_g, gn_b, ones_bd, wo_c, wo_r, n2_g, wr_hi, wr_lo, br)


def _moe_plan(route, cnt, tm, moe_rows):
    n = route.shape[0]
    ids = route[:, 0:2].astype(jnp.int32)
    ranks = route[:, 4:6].astype(jnp.int32)
    tile_cnt = cnt[:, 0, :N_EXPERTS].astype(jnp.int32)
    before_tile = jnp.cumsum(tile_cnt, axis=0) - tile_cnt
    total = jnp.sum(tile_cnt, axis=0)
    tiles_e = (total + moe_rows - 1) // moe_rows
    tile_end = jnp.cumsum(tiles_e)
    row_start = (tile_end - tiles_e) * moe_rows
    first_row = (row_start[None, :] + before_tile)[:, None, None, :]
    one_hot = ids.reshape(-1, tm, 2, 1) == jnp.arange(N_EXPERTS, dtype=jnp.int32)
    pos = jnp.sum(jnp.where(one_hot, first_row, 0), axis=-1).reshape(n, 2) + ranks
    n_row_tiles = 2 * n // moe_rows + N_EXPERTS
    t = jnp.arange(n_row_tiles, dtype=jnp.int32)
    tile_expert = jnp.minimum(jnp.sum((t[:, None] >= tile_end[None, :]).astype(jnp.int32), axis=1),
                              N_EXPERTS - 1)
    tile_used = (t < tile_end[-1]).astype(jnp.int32)
    return pos.reshape(-1), tile_expert.astype(jnp.int32), tile_used


def _row_copy_wait(src_ref, dst_ref, sem, rows):
    pltpu.make_async_copy(src_ref.at[pl.ds(0, rows), :], dst_ref.at[pl.ds(0, rows), :], sem).wait()


def _dispatch_kernel(pos_ref, x_ref, xs_init_ref, xs_ref, sem, *, tm):
    del xs_init_ref
    base = pl.program_id(0) * (2 * tm)

    def issue(r, _):
        src = x_ref.at[pl.ds(r, 1), :]
        for c in range(2):
            dst = xs_ref.at[pl.ds(pos_ref[base + 2 * r + c], 1), :]
            pltpu.make_async_copy(src, dst, sem).start()
        return 0
    lax.fori_loop(0, tm, issue, 0, unroll=DMA_ISSUE_UNROLL)
    for _ in range(2):
        _row_copy_wait(x_ref, xs_ref, sem, tm)


def _dispatch(pos, xn, n_rows, tm):
    n = xn.shape[0]
    xs_init = jnp.zeros((n_rows, D_MODEL), F32)
    return pl.pallas_call(
        functools.partial(_dispatch_kernel, tm=tm),
        grid_spec=pltpu.PrefetchScalarGridSpec(
            num_scalar_prefetch=1,
            grid=(n // tm,),
            in_specs=[pl.BlockSpec((tm, D_MODEL), lambda i, pos: (i, 0)),
                      pl.BlockSpec(memory_space=pl.ANY)],
            out_specs=pl.BlockSpec(memory_space=pl.ANY),
            scratch_shapes=[pltpu.SemaphoreType.DMA(())]),
        out_shape=jax.ShapeDtypeStruct((n_rows, D_MODEL), F32),
        input_output_aliases={2: 0},
        compiler_params=_cparams("arbitrary"),
        name="moe_dispatch",
    )(pos, xn, xs_init)


def _experts_kernel(expert_ref, used_ref, xs_ref, wg_ref, wu_ref, wd_ref, y_ref,
                    wg_bf, wu_bf, wd_bf):
    t = pl.program_id(0)

    @pl.when((t == 0) | (expert_ref[t] != expert_ref[jnp.maximum(t - 1, 0)]))
    def _():
        wg_bf[...] = wg_ref[0].astype(BF16)
        wu_bf[...] = wu_ref[0].astype(BF16)
        wd_bf[...] = wd_ref[0].astype(BF16)

    @pl.when(used_ref[t] != 0)
    def _():
        x = xs_ref[...].astype(BF16)
        hg = jnp.dot(x, wg_bf[...], preferred_element_type=F32)
        hu = jnp.dot(x, wu_bf[...], preferred_element_type=F32)
        act = hg * jax.nn.sigmoid(hg) * hu
        y_ref[...] = jnp.dot(act.astype(BF16), wd_bf[...], preferred_element_type=F32)

    @pl.when(used_ref[t] == 0)
    def _():
        y_ref[...] = jnp.zeros_like(y_ref)


def _experts(tile_expert, tile_used, xs, w_gate, w_up, w_down):
    n_rows = xs.shape[0]
    moe_rows = n_rows // tile_expert.shape[0]
    return pl.pallas_call(
        _experts_kernel,
        grid_spec=pltpu.PrefetchScalarGridSpec(
            num_scalar_prefetch=2,
            grid=(n_rows // moe_rows,),
            in_specs=[pl.BlockSpec((moe_rows, D_MODEL), lambda t, ex, used: (t, 0)),
                      pl.BlockSpec((1, D_MODEL, D_EXPERT), lambda t, ex, used: (ex[t], 0, 0)),
                      pl.BlockSpec((1, D_MODEL, D_EXPERT), lambda t, ex, used: (ex[t], 0, 0)),
                      pl.BlockSpec((1, D_EXPERT, D_MODEL), lambda t, ex, used: (ex[t], 0, 0))],
            out_specs=pl.BlockSpec((moe_rows, D_MODEL), lambda t, ex, used: (t, 0)),
            scratch_shapes=[pltpu.VMEM((D_MODEL, D_EXPERT), BF16), pltpu.VMEM((D_MODEL, D_EXPERT), BF16),
                            pltpu.VMEM((D_EXPERT, D_MODEL), BF16)]),
        out_shape=jax.ShapeDtypeStruct((n_rows, D_MODEL), F32),
        compiler_params=_cparams("arbitrary"),
        name="moe_experts",
    )(tile_expert, tile_used, xs, w_gate, w_up, w_down)


def _combine_kernel(pos_ref, h_ref, route_ref, nf_ref, ys_ref, o_ref, y1_ref, y2_ref, sem, *, tm):
    base = pl.program_id(0) * (2 * tm)

    def issue(r, _):
        for c, buf in enumerate((y1_ref, y2_ref)):
            src = ys_ref.at[pl.ds(pos_ref[base + 2 * r + c], 1), :]
            pltpu.make_async_copy(src, buf.at[pl.ds(r, 1), :], sem).start()
        return 0
    lax.fori_loop(0, tm, issue, 0, unroll=DMA_ISSUE_UNROLL)
    for buf in (y1_ref, y2_ref):
        _row_copy_wait(ys_ref, buf, sem, tm)

    route = route_ref[...]
    col = lax.broadcasted_iota(jnp.int32, route.shape, 1)
    w1 = jnp.sum(jnp.where(col == 2, route, 0.0), axis=-1, keepdims=True)
    w2 = jnp.sum(jnp.where(col == 3, route, 0.0), axis=-1, keepdims=True)
    h = h_ref[...] + w1 * y1_ref[...] + w2 * y2_ref[...]
    o_ref[...] = h * lax.rsqrt(jnp.mean(h * h, axis=-1, keepdims=True) + RMS_EPS) * nf_ref[...]


def _combine(pos, h, route, nf_g, ys, tm):
    n = h.shape[0]
    return pl.pallas_call(
        functools.partial(_combine_kernel, tm=tm),
        grid_spec=pltpu.PrefetchScalarGridSpec(
            num_scalar_prefetch=1,
            grid=(n // tm,),
            in_specs=[pl.BlockSpec((tm, D_MODEL), lambda i, pos: (i, 0)),
                      pl.BlockSpec((tm, ROUTE_COLS), lambda i, pos: (i, 0)),
                      pl.BlockSpec((1, D_MODEL), lambda i, pos: (0, 0)),
                      pl.BlockSpec(memory_space=pl.ANY)],
            out_specs=pl.BlockSpec((tm, D_MODEL), lambda i, pos: (i, 0)),
            scratch_shapes=[pltpu.VMEM((tm, D_MODEL), F32), pltpu.VMEM((tm, D_MODEL), F32),
                            pltpu.SemaphoreType.DMA(())]),
        out_shape=jax.ShapeDtypeStruct((n, D_MODEL), F32),
        compiler_params=_cparams("arbitrary"),
        name="moe_combine",
    )(pos, h, route, nf_g, ys)


def _moe(xn, route, cnt, h, w_gate, w_up, w_down, nf_g, tm):
    moe_rows = min(MOE_ROWS, max(LANES, 2 * xn.shape[0] // N_EXPERTS))
    pos, tile_expert, tile_used = _moe_plan(route, cnt, tm, moe_rows)
    xs = _dispatch(pos, xn, tile_expert.shape[0] * moe_rows, tm)
    ys = _experts(tile_expert, tile_used, xs, w_gate, w_up, w_down)
    return _combine(pos, h, route, nf_g, ys, tm)


def _layer(x, conv_buf, shift_buf, wkv_state, lw, *, tm, conv_seqs, conv_rows, prep_rows,
           wkv_tblk):
    b, t, _ = x.shape
    n = b * t
    n_grp = b // WKV_BATCH
    x2d = x.reshape(n, D_MODEL)

    h_glu, z = _inproj(x2d, lw["norm1_g"], lw["w_in"], tm)

    h3 = h_glu.reshape(b, t, C_CONV)
    hist = jnp.concatenate([jnp.zeros((b, CONV_PAD - CONV_WIDTH + 1, C_CONV), F32), conv_buf], axis=1)
    n_hist = CONV_WIDTH - 1
    new_conv = h3[:, t - n_hist:] if t >= n_hist else jnp.concatenate([conv_buf[:, t:], h3], axis=1)
    conv_out = _conv(h3, hist, lw["conv_w"], lw["conv_b"], lw["conv_ln_g"], lw["conv_ln_b"],
                     conv_seqs, conv_rows)

    new_shift = z.reshape(b, t, D_SHIFT)[:, -1][:, INV_PERM_Z]
    shift_buf = shift_buf[:, PERM_Z]
    if prep_rows > t:
        period, tiles_per_seq = t, 1
        bnd = jnp.repeat(shift_buf, t, axis=0).reshape(n // prep_rows, prep_rows, D_SHIFT)
    else:
        period, tiles_per_seq = prep_rows, t // prep_rows
        bnd = shift_buf.reshape(b, 1, D_SHIFT)
    r, w, k, v, nkk, bb, bonus, g = _prep(z, bnd, lw["prep"], prep_rows, period, tiles_per_seq)

    y_cols, s_l = _wkv(w, nkk, bb, k, r, v, _state_to_lanes(wkv_state, n_grp),
                       t_blk=wkv_tblk, carry_state=(n_grp == 1))
    new_wkv = _state_from_lanes(s_l, n_grp)

    h, xn2, route, cnt = _outproj(x2d, conv_out.reshape(n, C_CONV), y_cols, bonus, g,
                                  lw["outproj"], tm)
    out = _moe(xn2, route, cnt, h, lw["w_gate"], lw["w_up"], lw["w_down"], lw["norm_f_g"], tm)
    return out.reshape(b, t, D_MODEL), new_conv, new_shift, new_wkv


def _prepare_weights(norm1_g, w_in, conv_w, conv_b, conv_ln_g, conv_ln_b, mu_shift, w0,
                     w_decay_up, a0, w_aaa_up, w_gate_up, k_k, k_a, r_k, gn_g, gn_b, w_out,
                     norm2_g, w_router_group, b_router_group, w_router_expert, b_router_expert,
                     w_exp_gate, w_exp_up, w_exp_down, norm_f_g):
    row = lambda a: a.reshape(1, -1)
    head = jnp.arange(C_RWKV) % N_HEADS
    ones_bd = (head[:, None] == head[None, :LANES]).astype(BF16)
    w_in_perm = jnp.concatenate([w_in[:, :2 * C_CONV], w_in[:, 2 * C_CONV:][:, PERM_Z]], axis=1)
    zeros_lora = jnp.zeros((LANES - w_decay_up.shape[0], C_RWKV), F32)
    wd_pad = jnp.concatenate([w_decay_up[:, PERM_KEY], zeros_lora], axis=0).astype(BF16)
    wa_pad = jnp.concatenate([zeros_lora, w_aaa_up[:, PERM_KEY]], axis=0).astype(BF16)
    pad_cols = ROUTE_COLS - N_EXPERT_GROUPS - N_EXPERTS
    w_route = jnp.concatenate([w_router_group, w_router_expert,
                               jnp.zeros((D_MODEL, pad_cols), F32)], axis=1)
    wr_hi, wr_lo = _split_bf16(w_route)
    b_route = jnp.concatenate([b_router_group, b_router_expert, jnp.zeros((pad_cols,), F32)])
    return {
        "norm1_g": row(norm1_g), "w_in": w_in_perm.astype(BF16),
        "conv_w": conv_w, "conv_b": row(conv_b), "conv_ln_g": row(conv_ln_g),
        "conv_ln_b": row(conv_ln_b),
        "prep": (row(mu_shift[PERM_Z]), row(w0[PERM_KEY]), wd_pad, row(a0[PERM_KEY]), wa_pad,
                 w_gate_up[:, PERM_VAL].astype(BF16), row(k_k[PERM_KEY]), row(k_a[PERM_KEY]),
                 row(r_k[PERM_KEY]), ones_bd),
        "outproj": (row(gn_g[PERM_VAL]), row(gn_b[PERM_VAL]), ones_bd, w_out[:C_CONV].astype(BF16),
                    w_out[C_CONV:][PERM_VAL].astype(BF16), row(norm2_g), wr_hi, wr_lo, row(b_route)),
        "w_gate": w_exp_gate.reshape(N_EXPERTS, D_MODEL, D_EXPERT),
        "w_up": w_exp_up.reshape(N_EXPERTS, D_MODEL, D_EXPERT),
        "w_down": w_exp_down.reshape(N_EXPERTS, D_EXPERT, D_MODEL),
        "norm_f_g": row(norm_f_g),
    }


def kernel(x_prompt, x_sample, state_conv, state_shift, state_wkv, norm1_g, w_in, conv_w, conv_b, conv_ln_g, conv_ln_b, mu_shift, w0, w_decay_up, a0, w_aaa_up, w_gate_up, k_k, k_a, r_k, gn_g, gn_b, w_out, norm2_g, w_router_group, b_router_group, w_router_expert, b_router_expert, w_exp_gate, w_exp_up, w_exp_down, norm_f_g):
    depth = norm1_g.shape[0]
    assert depth == 1, "single-layer trunk"
    b = x_prompt.shape[0]
    assert b == WKV_BATCH and x_prompt.shape[1] % LOAD_ROWS == 0
    assert x_sample.shape[0] * x_sample.shape[1] == WKV_BATCH * LOAD_ROWS
    lw = _prepare_weights(norm1_g[0], w_in[0], conv_w[0], conv_b[0], conv_ln_g[0], conv_ln_b[0],
                          mu_shift[0], w0[0], w_decay_up[0], a0[0], w_aaa_up[0], w_gate_up[0],
                          k_k[0], k_a[0], r_k[0], gn_g[0], gn_b[0], w_out[0], norm2_g[0],
                          w_router_group[0], b_router_group[0], w_router_expert[0],
                          b_router_expert[0], w_exp_gate[0], w_exp_up[0], w_exp_down[0], norm_f_g)
    dt = x_prompt.dtype
    y_p, conv_p, shift_p, wkv_p = _layer(
        x_prompt,
        jnp.zeros((b, CONV_WIDTH - 1, C_CONV), dt),
        jnp.zeros((b, D_SHIFT), dt),
        jnp.zeros((b, N_HEADS, HEAD_DIM, HEAD_DIM), dt), lw,
        tm=512, conv_seqs=1, conv_rows=128, prep_rows=256, wkv_tblk=LOAD_ROWS)
    y_s, conv_s, shift_s, wkv_s = _layer(
        x_sample, state_conv[0], state_shift[0], state_wkv[0], lw,
        tm=512, conv_seqs=16, conv_rows=8, prep_rows=256, wkv_tblk=x_sample.shape[1])
    return (y_p, y_s, conv_p[None], shift_p[None], wkv_p[None],
            conv_s[None], shift_s[None], wkv_s[None])
```

```python
import functools

import numpy as np
import jax
import jax.numpy as jnp
from jax import lax
from jax.experimental import pallas as pl
from jax.experimental.pallas import tpu as pltpu

F32 = jnp.float32
BF16 = jnp.bfloat16

D_MODEL = 1024
C_CONV = 512
C_RWKV = 512
HEAD_DIM = 64
N_HEADS = 8
CONV_WIDTH = 31
D_SHIFT = 1792
N_EXPERT_GROUPS = 4
EXPERTS_PER_GROUP = 8
N_EXPERTS = N_EXPERT_GROUPS * EXPERTS_PER_GROUP
D_EXPERT = 256
RMS_EPS = 1e-6
LN_EPS = 1e-5
GN_EPS = 64e-5

LANES = 128
SUBLANES = 8
CONV_PAD = 32
WKV_BATCH = 8
BH_PER_GROUP = WKV_BATCH * N_HEADS
K_HALF = HEAD_DIM // 2
WKV_PARTS = 2
LOAD_ROWS = LANES
ROUTE_COLS = LANES
EXPERT_COL0 = N_EXPERT_GROUPS
MOE_ROWS = 512
DMA_ISSUE_UNROLL = 8
VMEM_LIMIT = 56 * 1024 * 1024


def _key_perm():
    kp, kh, h = np.meshgrid(np.arange(K_HALF), np.arange(2), np.arange(N_HEADS), indexing="ij")
    return (h * HEAD_DIM + kh * K_HALF + kp).reshape(-1)


def _value_perm():
    v, h = np.meshgrid(np.arange(HEAD_DIM), np.arange(N_HEADS), indexing="ij")
    return (h * HEAD_DIM + v).reshape(-1)


PERM_KEY = _key_perm()
PERM_VAL = _value_perm()
PERM_Z = np.concatenate([PERM_KEY, C_RWKV + PERM_KEY, 2 * C_RWKV + PERM_VAL,
                         np.arange(3 * C_RWKV, D_SHIFT)])
INV_PERM_Z = np.argsort(PERM_Z)


def _cparams(*sem):
    return pltpu.CompilerParams(dimension_semantics=sem, vmem_limit_bytes=VMEM_LIMIT)


def _split_bf16(x):
    hi = x.astype(BF16)
    lo = (x - hi.astype(F32)).astype(BF16)
    return hi, lo


def _segsum(x, same_head):
    hi, lo = _split_bf16(x)
    s = (jnp.dot(hi, same_head, preferred_element_type=F32)
         + jnp.dot(lo, same_head, preferred_element_type=F32))
    return jnp.concatenate([s] * (x.shape[1] // LANES), axis=1)


def _inproj_kernel(x_ref, g_ref, w_ref, h_ref, z_ref):
    x = x_ref[...]
    xn = x * lax.rsqrt(jnp.mean(x * x, axis=-1, keepdims=True) + RMS_EPS) * g_ref[...]
    p = jnp.dot(xn.astype(BF16), w_ref[...], preferred_element_type=F32)
    h_ref[...] = p[:, :C_CONV] * jax.nn.sigmoid(p[:, C_CONV:2 * C_CONV])
    z_ref[...] = p[:, 2 * C_CONV:]


def _inproj(x, g, w_bf, tm):
    n = x.shape[0]
    d_in = w_bf.shape[1]
    return pl.pallas_call(
        _inproj_kernel,
        grid=(n // tm,),
        in_specs=[pl.BlockSpec((tm, D_MODEL), lambda i: (i, 0)),
                  pl.BlockSpec((1, D_MODEL), lambda i: (0, 0)),
                  pl.BlockSpec((D_MODEL, d_in), lambda i: (0, 0))],
        out_specs=[pl.BlockSpec((tm, C_CONV), lambda i: (i, 0)),
                   pl.BlockSpec((tm, D_SHIFT), lambda i: (i, 0))],
        out_shape=[jax.ShapeDtypeStruct((n, C_CONV), F32),
                   jax.ShapeDtypeStruct((n, D_SHIFT), F32)],
        compiler_params=_cparams("parallel"),
        name="inproj",
    )(x, g, w_bf)


def _conv_kernel(h_ref, hist_ref, w_ref, b_ref, lg_ref, lb_ref, o_ref, win_ref, sh_ref,
                 *, rows, sub_rows, n_chunks, n_seq):
    shifted_rows = rows + CONV_PAD - SUBLANES

    def seq_body(s, _):
        def chunk_body(c, _):
            t0 = pl.multiple_of(c * rows, SUBLANES)
            if n_chunks == 1:
                win_ref[pl.ds(0, CONV_PAD), :] = hist_ref[s]
            else:
                prev0 = pl.multiple_of(jnp.maximum(t0 - CONV_PAD, 0), SUBLANES)
                win_ref[pl.ds(0, CONV_PAD), :] = jnp.where(
                    c == 0, hist_ref[s], h_ref[s, pl.ds(prev0, CONV_PAD), :])
            win_ref[pl.ds(CONV_PAD, rows), :] = h_ref[s, pl.ds(t0, rows), :]
            sh_ref[0] = win_ref[...]
            for sft in range(1, SUBLANES):
                sh_ref[sft, pl.ds(0, shifted_rows), :] = win_ref[pl.ds(sft, shifted_rows), :]
            for blk in range(rows // sub_rows):
                r0 = blk * sub_rows
                acc = jnp.zeros((sub_rows, C_CONV), F32) + b_ref[...]
                for j in range(CONV_WIDTH):
                    off = j + 2
                    acc = acc + (sh_ref[off % SUBLANES, pl.ds(off - off % SUBLANES + r0, sub_rows), :]
                                 * w_ref[pl.ds(j, 1), :])
                mu = jnp.mean(acc, axis=-1, keepdims=True)
                d = acc - mu
                var = jnp.mean(d * d, axis=-1, keepdims=True)
                of = d * lax.rsqrt(var + LN_EPS) * lg_ref[...] + lb_ref[...]
                o_ref[s, pl.ds(t0 + r0, sub_rows), :] = (of * jax.nn.sigmoid(of)).astype(o_ref.dtype)
            return 0
        lax.fori_loop(0, n_chunks, chunk_body, 0)
        return 0
    lax.fori_loop(0, n_seq, seq_body, 0)


def _conv(h, hist, conv_w, conv_b, ln_g, ln_b, n_seq, rows):
    b, t, _ = h.shape
    sub_rows = min(rows, 32)
    kern = functools.partial(_conv_kernel, rows=rows, sub_rows=sub_rows, n_chunks=t // rows,
                             n_seq=n_seq)
    vec = pl.BlockSpec((1, C_CONV), lambda i: (0, 0))
    return pl.pallas_call(
        kern,
        grid=(b // n_seq,),
        in_specs=[pl.BlockSpec((n_seq, t, C_CONV), lambda i: (i, 0, 0)),
                  pl.BlockSpec((n_seq, CONV_PAD, C_CONV), lambda i: (i, 0, 0)),
                  pl.BlockSpec((CONV_WIDTH, C_CONV), lambda i: (0, 0)),
                  vec, vec, vec],
        out_specs=pl.BlockSpec((n_seq, t, C_CONV), lambda i: (i, 0, 0)),
        out_shape=jax.ShapeDtypeStruct((b, t, C_CONV), BF16),
        scratch_shapes=[pltpu.VMEM((rows + CONV_PAD, C_CONV), F32),
                        pltpu.VMEM((SUBLANES, rows + CONV_PAD, C_CONV), F32)],
        compiler_params=_cparams("parallel"),
        name="conv",
    )(h, hist, conv_w, conv_b, ln_g, ln_b)


def _prep_kernel(z_ref, bnd_ref, halo_ref, mu_ref, w0_ref, wd_ref, a0_ref, wa_ref, wg_ref, kk_ref,
                 ka_ref, rk_ref, ones_ref,
                 r_ref, w_ref, k_ref, v_ref, nkk_ref, b_ref, bonus_ref, g_ref,
                 *, period, tiles_per_seq):
    z = z_ref[...]
    row_id = lax.broadcasted_iota(jnp.int32, z.shape, 0)
    bnd = bnd_ref[...]
    if tiles_per_seq > 1:
        first_tile = pl.program_id(0) % tiles_per_seq == 0
        bnd = jnp.where(first_tile, bnd, halo_ref[pl.ds(SUBLANES - 1, 1), :])
    z_prev = jnp.where(row_id % period == 0, bnd, pltpu.roll(z, 1, axis=0))
    zs = z + (z_prev - z) * mu_ref[...]
    r = zs[:, :C_RWKV]
    k = zs[:, C_RWKV:2 * C_RWKV]
    v = zs[:, 2 * C_RWKV:3 * C_RWKV]
    wa = zs[:, 3 * C_RWKV:3 * C_RWKV + LANES]
    gl = zs[:, 3 * C_RWKV + LANES:]
    ones_bd = ones_ref[...]

    lw = w0_ref[...] + jnp.dot(jnp.tanh(wa).astype(BF16), wd_ref[...], preferred_element_type=F32)
    w_log = -(jnp.maximum(-lw, 0.0) + jnp.log(1.0 + jnp.exp(-jnp.abs(lw)))) - 0.5
    decay = jnp.exp(-jnp.exp(w_log))
    a = jax.nn.sigmoid(a0_ref[...] + jnp.dot(wa.astype(BF16), wa_ref[...],
                                             preferred_element_type=F32))
    g = jnp.dot(jax.nn.sigmoid(gl).astype(BF16), wg_ref[...], preferred_element_type=F32)

    kk = k * kk_ref[...]
    norm = jnp.maximum(jnp.sqrt(_segsum(kk * kk, ones_bd)), 1e-12)
    kk = kk / norm
    k_mod = k * (1.0 + (a - 1.0) * ka_ref[...])
    bonus = _segsum(r * k_mod * rk_ref[...], ones_bd) * v

    def store_cols(ref, x):
        xt = x.T
        ntok = ref.shape[2]
        for q in range(ref.shape[0]):
            ref[q] = xt[:, q * ntok:(q + 1) * ntok]

    store_cols(r_ref, r)
    store_cols(w_ref, decay)
    store_cols(k_ref, k_mod)
    store_cols(v_ref, v)
    store_cols(nkk_ref, -kk)
    store_cols(b_ref, kk * a)
    bonus_ref[...] = bonus
    g_ref[...] = g


def _col_major_spec(rows, length):
    if rows <= length:
        per_b = length // rows
        return pl.BlockSpec((1, C_RWKV, rows), lambda i: (i // per_b, 0, i % per_b))
    return pl.BlockSpec((rows // length, C_RWKV, length), lambda i: (i, 0, 0))


def _prep(z2d, bnd, params, rows, period, tiles_per_seq):
    n = z2d.shape[0]
    length = n // WKV_BATCH
    (mu, w0, wd_pad, a0, wa_pad, wg, k_k, k_a, r_k, ones_bd) = params
    kern = functools.partial(_prep_kernel, period=period, tiles_per_seq=tiles_per_seq)
    halo_blocks = rows // SUBLANES
    vec = lambda c: pl.BlockSpec((1, c), lambda i: (0, 0))
    mat = lambda a: pl.BlockSpec(a.shape, lambda i: (0, 0))
    row_spec = pl.BlockSpec((rows, C_RWKV), lambda i: (i, 0))
    row_sd = jax.ShapeDtypeStruct((n, C_RWKV), F32)
    col_spec = _col_major_spec(rows, length)
    col_sd = jax.ShapeDtypeStruct((WKV_BATCH, C_RWKV, length), F32)
    return pl.pallas_call(
        kern,
        grid=(n // rows,),
        in_specs=[pl.BlockSpec((rows, D_SHIFT), lambda i: (i, 0)),
                  pl.BlockSpec((None, bnd.shape[1], D_SHIFT), lambda i: (i // tiles_per_seq, 0, 0)),
                  pl.BlockSpec((SUBLANES, D_SHIFT),
                               lambda i: (jnp.maximum(i * halo_blocks - 1, 0), 0)),
                  vec(D_SHIFT), vec(C_RWKV), mat(wd_pad), vec(C_RWKV), mat(wa_pad), mat(wg),
                  vec(C_RWKV), vec(C_RWKV), vec(C_RWKV), mat(ones_bd)],
        out_specs=[col_spec] * 6 + [row_spec] * 2,
        out_shape=[col_sd] * 6 + [row_sd] * 2,
        compiler_params=_cparams("parallel"),
        name="rwkv_prep",
    )(z2d, bnd, z2d, mu, w0, wd_pad, a0, wa_pad, wg, k_k, k_a, r_k, ones_bd)


def _fold_halves(x):
    return x + pltpu.roll(x, BH_PER_GROUP, axis=x.ndim - 1)


def _key_cols_to_lanes(src_ref, dst_ref):
    chunk = 2 * N_HEADS
    for kp in range(K_HALF):
        pieces = [src_ref[b8, pl.ds(kp * chunk + half * N_HEADS, N_HEADS), :]
                  for half in range(2) for b8 in range(WKV_BATCH)]
        dst_ref[kp, :LOAD_ROWS, :] = jnp.concatenate(pieces, axis=0).T


def _value_cols_to_lanes(src_ref, dst_ref):
    for v in range(HEAD_DIM):
        pieces = [src_ref[b8, pl.ds(v * N_HEADS, N_HEADS), :]
                  for _ in range(2) for b8 in range(WKV_BATCH)]
        dst_ref[pl.ds(v, LOAD_ROWS, stride=HEAD_DIM), :] = jnp.concatenate(pieces, axis=0).T


def _lanes_to_value_cols(y_s, yt_ref):
    for v in range(HEAD_DIM):
        yt = y_s[pl.ds(v, LOAD_ROWS, stride=HEAD_DIM), :].T
        folded = yt[:BH_PER_GROUP] + yt[BH_PER_GROUP:]
        for b8 in range(WKV_BATCH):
            yt_ref[b8, pl.ds(v * N_HEADS, N_HEADS), :] = folded[b8 * N_HEADS:(b8 + 1) * N_HEADS]


def _wkv_kernel(w_ref, nkk_ref, b_ref, k_ref, r_ref, v_ref, s0_ref, y_ref, sout_ref,
                w_s, nkk_s, b_s, k_s, r_s, v_s, y_s, s_ref,
                *, t_blk, steps_per_load, carry_state, n_parts):
    j = pl.program_id(0)
    sub = j % steps_per_load
    part = HEAD_DIM // n_parts
    parts = [slice(i * part, (i + 1) * part) for i in range(n_parts)]

    @pl.when(sub == 0)
    def _():
        for src, dst in ((w_ref, w_s), (nkk_ref, nkk_s), (b_ref, b_s), (k_ref, k_s), (r_ref, r_s)):
            _key_cols_to_lanes(src, dst)
        _value_cols_to_lanes(v_ref, v_s)

    if carry_state:
        @pl.when(j == 0)
        def _():
            s_ref[...] = s0_ref[0]
    else:
        s_ref[...] = s0_ref[0]

    tok0 = sub * t_blk
    acc = jnp.zeros((HEAD_DIM, LANES), F32)
    for kp in range(K_HALF):
        acc = acc + s_ref[kp] * nkk_s[kp, pl.ds(tok0, 1), :]
    sa_init = tuple(_fold_halves(acc[p, :]) for p in parts[:-1]) + (acc[parts[-1], :],)

    def part_step(tok, nxt_tok, rows, sa):
        cur = pl.ds(tok, 1)
        vrows = pl.ds(pl.multiple_of(tok * HEAD_DIM, HEAD_DIM) + rows.start, part)
        vv = v_s[vrows, :]
        yacc = jnp.zeros((part, LANES), F32)
        sacc = jnp.zeros((part, LANES), F32)
        for kp in range(K_HALF):
            s = (s_ref[kp, rows, :] * w_s[kp, cur, :] + sa * b_s[kp, cur, :]
                 + vv * k_s[kp, cur, :])
            s_ref[kp, rows, :] = s
            yacc = yacc + s * r_s[kp, cur, :]
            sacc = sacc + s * nkk_s[kp, pl.ds(nxt_tok, 1), :]
        y_s[vrows, :] = yacc
        return sacc

    def step(t, sa_parts):
        tok = tok0 + t
        nxt_tok = tok0 + jnp.minimum(t + 1, t_blk - 1)
        sa_last = _fold_halves(sa_parts[-1])
        nxt = [_fold_halves(part_step(tok, nxt_tok, rows, sa))
               for rows, sa in zip(parts[:-1], sa_parts[:-1])]
        return tuple(nxt) + (part_step(tok, nxt_tok, parts[-1], sa_last),)

    lax.fori_loop(0, t_blk, step, sa_init)

    if carry_state:
        @pl.when(j == pl.num_programs(0) - 1)
        def _():
            sout_ref[0] = s_ref[...]
    else:
        sout_ref[0] = s_ref[...]

    @pl.when(sub == steps_per_load - 1)
    def _():
        _lanes_to_value_cols(y_s, y_ref)


def _wkv(w, nkk, b, k, r, v, s0, *, t_blk, carry_state):
    length = w.shape[2]
    steps_per_load = LOAD_ROWS // t_blk
    n_steps = length // t_blk
    tok_spec = pl.BlockSpec((WKV_BATCH, C_RWKV, LOAD_ROWS), lambda j: (0, 0, j // steps_per_load))
    state_blk = (1, K_HALF, HEAD_DIM, LANES)
    if carry_state:
        sspec = pl.BlockSpec(state_blk, lambda j: (0, 0, 0, 0))
    else:
        sspec = pl.BlockSpec(state_blk, lambda j: (j, 0, 0, 0))
    key_rows = pltpu.VMEM((K_HALF, LOAD_ROWS + SUBLANES, LANES), F32)
    val_rows = pltpu.VMEM((LOAD_ROWS * HEAD_DIM, LANES), F32)
    return pl.pallas_call(
        functools.partial(_wkv_kernel, t_blk=t_blk, steps_per_load=steps_per_load,
                          carry_state=carry_state, n_parts=WKV_PARTS),
        grid=(n_steps,),
        in_specs=[tok_spec] * 6 + [sspec],
        out_specs=[tok_spec, sspec],
        out_shape=[jax.ShapeDtypeStruct(w.shape, F32), jax.ShapeDtypeStruct(s0.shape, F32)],
        scratch_shapes=[key_rows] * 5 + [val_rows, val_rows,
                                         pltpu.VMEM((K_HALF, HEAD_DIM, LANES), F32)],
        compiler_params=_cparams("arbitrary"),
        name="wkv",
    )(w, nkk, b, k, r, v, s0)


def _state_to_lanes(s, n_grp):
    s = s.reshape(WKV_BATCH, n_grp, N_HEADS, HEAD_DIM, 2, K_HALF)
    return s.transpose(1, 5, 3, 4, 0, 2).reshape(n_grp, K_HALF, HEAD_DIM, LANES)


def _state_from_lanes(s, n_grp):
    s = s.reshape(n_grp, K_HALF, HEAD_DIM, 2, WKV_BATCH, N_HEADS)
    return s.transpose(4, 0, 5, 2, 3, 1).reshape(n_grp * WKV_BATCH, N_HEADS, HEAD_DIM, HEAD_DIM)


def _outproj_kernel(x_ref, c_ref, y_ref, bonus_ref, g_ref, gng_ref, gnb_ref, ones_ref,
                    wo_c_ref, wo_r_ref, n2_ref, wr_both_ref, br_ref,
                    h_ref, xn_ref, route_ref, cnt_ref):
    ones_bd = ones_ref[...]
    y = jnp.concatenate([y_ref[q].T for q in range(y_ref.shape[0])], axis=0)
    inv_n = 1.0 / HEAD_DIM
    mu = _segsum(y, ones_bd) * inv_n
    d = y - mu
    var = _segsum(d * d, ones_bd) * inv_n
    yn = d * lax.rsqrt(var + GN_EPS) * gng_ref[...] + gnb_ref[...]
    rw = (yn + bonus_ref[...]) * g_ref[...]
    h = (x_ref[...]
         + jnp.dot(c_ref[...], wo_c_ref[...], preferred_element_type=F32)
         + jnp.dot(rw.astype(BF16), wo_r_ref[...], preferred_element_type=F32))
    h_ref[...] = h
    xn = h * lax.rsqrt(jnp.mean(h * h, axis=-1, keepdims=True) + RMS_EPS) * n2_ref[...]
    xn_ref[...] = xn

    x_hi, x_lo = _split_bf16(xn)
    hi_both = jnp.dot(x_hi, wr_both_ref[...], preferred_element_type=F32)
    logits = (hi_both[:, :ROUTE_COLS]
              + jnp.dot(x_lo, wr_both_ref[:, :ROUTE_COLS], preferred_element_type=F32)
              + hi_both[:, ROUTE_COLS:]) + br_ref[...]

    col = lax.broadcasted_iota(jnp.int32, logits.shape, 1)
    neg = jnp.float32(-jnp.inf)
    big = jnp.int32(ROUTE_COLS)
    is_g = col < N_EXPERT_GROUPS
    g_logit = jnp.where(is_g, logits, neg)
    g_max = jnp.max(g_logit, axis=-1, keepdims=True)
    g_idx = jnp.min(jnp.where(g_logit == g_max, col, big), axis=-1, keepdims=True)
    p_group = 1.0 / jnp.sum(jnp.where(is_g, jnp.exp(logits - g_max), 0.0), axis=-1, keepdims=True)
    lo_col = EXPERT_COL0 + g_idx * EXPERTS_PER_GROUP
    in_grp = (col >= lo_col) & (col < lo_col + EXPERTS_PER_GROUP)
    e1 = jnp.where(in_grp, logits, neg)
    m1 = jnp.max(e1, axis=-1, keepdims=True)
    i1 = jnp.min(jnp.where(e1 == m1, col, big), axis=-1, keepdims=True)
    e2 = jnp.where(col == i1, neg, e1)
    m2 = jnp.max(e2, axis=-1, keepdims=True)
    i2 = jnp.min(jnp.where(e2 == m2, col, big), axis=-1, keepdims=True)
    ex = jnp.exp(m2 - m1)
    w1 = p_group / (1.0 + ex)
    w2 = p_group * ex / (1.0 + ex)

    exp1 = i1 - EXPERT_COL0
    exp2 = i2 - EXPERT_COL0
    oh1 = jnp.where(col == exp1, 1.0, 0.0)
    oh2 = jnp.where(col == exp2, 1.0, 0.0)
    tm = logits.shape[0]
    earlier = jnp.where(lax.broadcasted_iota(jnp.int32, (tm, tm), 0)
                        > lax.broadcasted_iota(jnp.int32, (tm, tm), 1), 1.0, 0.0).astype(BF16)
    before1 = jnp.dot(earlier, oh1.astype(BF16), preferred_element_type=F32)
    before2 = jnp.dot(earlier, oh2.astype(BF16), preferred_element_type=F32)
    cnt1 = jnp.sum(oh1, axis=0, keepdims=True)
    cnt2 = jnp.sum(oh2, axis=0, keepdims=True)
    rank1 = jnp.sum(before1 * oh1, axis=-1, keepdims=True)
    rank2 = jnp.sum((before2 + cnt1) * oh2, axis=-1, keepdims=True)
    fields = (exp1.astype(F32), exp2.astype(F32), w1, w2, rank1, rank2)
    route = jnp.zeros_like(logits)
    for c, val in enumerate(fields):
        route = jnp.where(col == c, val, route)
    route_ref[...] = route
    cnt_ref[...] = jnp.broadcast_to(cnt1 + cnt2, cnt_ref.shape)


def _outproj(x, conv_out, y, bonus, g, params, tm):
    n = x.shape[0]
    (gn_g, gn_b, ones_bd, wo_c, wo_r, n2_g, wr_both, br) = params
    row = lambda c: pl.BlockSpec((tm, c), lambda i: (i, 0))
    vec = lambda c: pl.BlockSpec((1, c), lambda i: (0, 0))
    mat = lambda a: pl.BlockSpec(a.shape, lambda i: (0, 0))
    return pl.pallas_call(
        _outproj_kernel,
        grid=(n // tm,),
        in_specs=[row(D_MODEL), row(C_CONV), _col_major_spec(tm, n // WKV_BATCH), row(C_RWKV),
                  row(C_RWKV),
                  vec(C_RWKV), vec(C_RWKV), mat(ones_bd), mat(wo_c), mat(wo_r), vec(D_MODEL),
                  mat(wr_both), vec(ROUTE_COLS)],
        out_specs=[row(D_MODEL), row(D_MODEL), row(ROUTE_COLS),
                   pl.BlockSpec((None, SUBLANES, ROUTE_COLS), lambda i: (i, 0, 0))],
        out_shape=[jax.ShapeDtypeStruct((n, D_MODEL), F32),
                   jax.ShapeDtypeStruct((n, D_MODEL), F32),
                   jax.ShapeDtypeStruct((n, ROUTE_COLS), F32),
                   jax.ShapeDtypeStruct((n // tm, SUBLANES, ROUTE_COLS), F32)],
        compiler_params=_cparams("parallel"),
        name="outproj_route",
    )(x, conv_out, y, bonus, g, gn_g, gn_b, ones_bd, wo_c, wo_r, n2_g, wr_both, br)


def _moe_plan(route, cnt, tm, moe_rows):
    n = route.shape[0]
    ids = route[:, 0:2].astype(jnp.int32)
    ranks = route[:, 4:6].astype(jnp.int32)
    tile_cnt = cnt[:, 0, :N_EXPERTS].astype(jnp.int32)
    before_tile = jnp.cumsum(tile_cnt, axis=0) - tile_cnt
    total = jnp.sum(tile_cnt, axis=0)
    tiles_e = (total + moe_rows - 1) // moe_rows
    tile_end = jnp.cumsum(tiles_e)
    row_start = (tile_end - tiles_e) * moe_rows
    first_row = (row_start[None, :] + before_tile)[:, None, None, :]
    one_hot = ids.reshape(-1, tm, 2, 1) == jnp.arange(N_EXPERTS, dtype=jnp.int32)
    pos = jnp.sum(jnp.where(one_hot, first_row, 0), axis=-1).reshape(n, 2) + ranks
    n_row_tiles = 2 * n // moe_rows + N_EXPERTS
    t = jnp.arange(n_row_tiles, dtype=jnp.int32)
    tile_expert = jnp.minimum(jnp.sum((t[:, None] >= tile_end[None, :]).astype(jnp.int32), axis=1),
                              N_EXPERTS - 1)
    tile_used = (t < tile_end[-1]).astype(jnp.int32)
    last_tile = jnp.maximum(tile_end[-1:] - 1, 0).astype(jnp.int32)
    is_last = jnp.any((t[:, None] == tile_end[None, :] - 1) & (tiles_e[None, :] > 0), axis=1)
    tile_fill = (is_last | (tile_used == 0)).astype(jnp.int32)
    return pos.reshape(-1), tile_fill, tile_expert.astype(jnp.int32), tile_used, last_tile


def _row_copy_wait(src_ref, dst_ref, sem, rows):
    pltpu.make_async_copy(src_ref.at[pl.ds(0, rows), :], dst_ref.at[pl.ds(0, rows), :], sem).wait()


def _dispatch_kernel(pos_ref, fill_ref, x_ref, xs_ref, zero_ref, sem, *, tm):
    base = pl.program_id(0) * (2 * tm)
    fill_rows = zero_ref.shape[0]

    @pl.when(pl.program_id(0) == 0)
    def _():
        zero_ref[...] = jnp.zeros_like(zero_ref)
        n_row_tiles = xs_ref.shape[0] // fill_rows

        def fill(t, _):
            @pl.when(fill_ref[t] != 0)
            def _():
                start = pl.multiple_of(t * fill_rows, fill_rows)
                pltpu.make_async_copy(zero_ref, xs_ref.at[pl.ds(start, fill_rows), :], sem).start()
            return 0
        lax.fori_loop(0, n_row_tiles, fill, 0)

        def fill_wait(t, _):
            @pl.when(fill_ref[t] != 0)
            def _():
                pltpu.make_async_copy(zero_ref, xs_ref.at[pl.ds(0, fill_rows), :], sem).wait()
            return 0
        lax.fori_loop(0, n_row_tiles, fill_wait, 0)

    def issue(r, _):
        src = x_ref.at[pl.ds(r, 1), :]
        for c in range(2):
            dst = xs_ref.at[pl.ds(pos_ref[base + 2 * r + c], 1), :]
            pltpu.make_async_copy(src, dst, sem).start()
        return 0
    lax.fori_loop(0, tm, issue, 0, unroll=DMA_ISSUE_UNROLL)
    for _ in range(2):
        _row_copy_wait(x_ref, xs_ref, sem, tm)


def _dispatch(pos, tile_fill, xn, n_rows, moe_rows, tm):
    n = xn.shape[0]
    return pl.pallas_call(
        functools.partial(_dispatch_kernel, tm=tm),
        grid_spec=pltpu.PrefetchScalarGridSpec(
            num_scalar_prefetch=2,
            grid=(n // tm,),
            in_specs=[pl.BlockSpec((tm, D_MODEL), lambda i, pos, fill: (i, 0))],
            out_specs=pl.BlockSpec(memory_space=pl.ANY),
            scratch_shapes=[pltpu.VMEM((moe_rows, D_MODEL), F32), pltpu.SemaphoreType.DMA(())]),
        out_shape=jax.ShapeDtypeStruct((n_rows, D_MODEL), F32),
        compiler_params=_cparams("arbitrary"),
        name="moe_dispatch",
    )(pos, tile_fill, xn)


def _experts_kernel(expert_ref, used_ref, last_ref, xs_ref, wg_ref, wu_ref, wd_ref, y_ref,
                    wg_bf, wu_bf, wd_bf):
    del last_ref
    t = pl.program_id(0)

    @pl.when((t == 0) | (expert_ref[t] != expert_ref[jnp.maximum(t - 1, 0)]))
    def _():
        wg_bf[...] = wg_ref[0].astype(BF16)
        wu_bf[...] = wu_ref[0].astype(BF16)
        wd_bf[...] = wd_ref[0].astype(BF16)

    @pl.when(used_ref[t] != 0)
    def _():
        x = xs_ref[...].astype(BF16)
        hg = jnp.dot(x, wg_bf[...], preferred_element_type=F32)
        hu = jnp.dot(x, wu_bf[...], preferred_element_type=F32)
        act = hg * jax.nn.sigmoid(hg) * hu
        y_ref[...] = jnp.dot(act.astype(BF16), wd_bf[...], preferred_element_type=F32)

    @pl.when(used_ref[t] == 0)
    def _():
        y_ref[...] = jnp.zeros_like(y_ref)


def _experts(tile_expert, tile_used, last_tile, xs, w_gate, w_up, w_down):
    n_rows = xs.shape[0]
    moe_rows = n_rows // tile_expert.shape[0]
    row_tile = lambda t, ex, used, last: (jnp.where(used[t] != 0, t, last[0]), 0)
    weight = lambda t, ex, used, last: (ex[t], 0, 0)
    return pl.pallas_call(
        _experts_kernel,
        grid_spec=pltpu.PrefetchScalarGridSpec(
            num_scalar_prefetch=3,
            grid=(n_rows // moe_rows,),
            in_specs=[pl.BlockSpec((moe_rows, D_MODEL), row_tile),
                      pl.BlockSpec((1, D_MODEL, D_EXPERT), weight),
                      pl.BlockSpec((1, D_MODEL, D_EXPERT), weight),
                      pl.BlockSpec((1, D_EXPERT, D_MODEL), weight)],
            out_specs=pl.BlockSpec((moe_rows, D_MODEL), lambda t, ex, used, last: (t, 0)),
            scratch_shapes=[pltpu.VMEM((D_MODEL, D_EXPERT), BF16), pltpu.VMEM((D_MODEL, D_EXPERT), BF16),
                            pltpu.VMEM((D_EXPERT, D_MODEL), BF16)]),
        out_shape=jax.ShapeDtypeStruct((n_rows, D_MODEL), F32),
        compiler_params=_cparams("arbitrary"),
        name="moe_experts",
    )(tile_expert, tile_used, last_tile, xs, w_gate, w_up, w_down)


def _combine_kernel(pos_ref, h_ref, route_ref, nf_ref, ys_ref, o_ref, y_buf, sem, *, tm):
    i = pl.program_id(0)
    last = pl.num_programs(0) - 1

    def issue_tile(base, slot):
        def issue(r, _):
            for c in range(2):
                src = ys_ref.at[pl.ds(pos_ref[base + 2 * r + c], 1), :]
                pltpu.make_async_copy(src, y_buf.at[slot, c, pl.ds(r, 1), :], sem.at[slot]).start()
            return 0
        lax.fori_loop(0, tm, issue, 0, unroll=DMA_ISSUE_UNROLL)

    def wait(slot):
        for c in range(2):
            _row_copy_wait(ys_ref, y_buf.at[slot, c], sem.at[slot], tm)

    @pl.when(i == 0)
    def _():
        issue_tile(0, 0)

    next_base = jnp.minimum(i + 1, last) * (2 * tm)

    def run(slot):
        other = 1 - slot
        issue_tile(next_base, other)
        wait(slot)
        route = route_ref[...]
        col = lax.broadcasted_iota(jnp.int32, route.shape, 1)
        w1 = jnp.sum(jnp.where(col == 2, route, 0.0), axis=-1, keepdims=True)
        w2 = jnp.sum(jnp.where(col == 3, route, 0.0), axis=-1, keepdims=True)
        h = h_ref[...] + w1 * y_buf[slot, 0] + w2 * y_buf[slot, 1]
        o_ref[...] = h * lax.rsqrt(jnp.mean(h * h, axis=-1, keepdims=True) + RMS_EPS) * nf_ref[...]

        @pl.when(i == last)
        def _():
            wait(other)

    for slot in range(2):
        pl.when(i % 2 == slot)(functools.partial(run, slot))


def _combine(pos, h, route, nf_g, ys, tm):
    n = h.shape[0]
    return pl.pallas_call(
        functools.partial(_combine_kernel, tm=tm),
        grid_spec=pltpu.PrefetchScalarGridSpec(
            num_scalar_prefetch=1,
            grid=(n // tm,),
            in_specs=[pl.BlockSpec((tm, D_MODEL), lambda i, pos: (i, 0)),
                      pl.BlockSpec((tm, ROUTE_COLS), lambda i, pos: (i, 0)),
                      pl.BlockSpec((1, D_MODEL), lambda i, pos: (0, 0)),
                      pl.BlockSpec(memory_space=pl.ANY)],
            out_specs=pl.BlockSpec((tm, D_MODEL), lambda i, pos: (i, 0)),
            scratch_shapes=[pltpu.VMEM((2, 2, tm, D_MODEL), F32), pltpu.SemaphoreType.DMA((2,))]),
        out_shape=jax.ShapeDtypeStruct((n, D_MODEL), F32),
        compiler_params=_cparams("arbitrary"),
        name="moe_combine",
    )(pos, h, route, nf_g, ys)


def _moe(xn, route, cnt, h, w_gate, w_up, w_down, nf_g, tm):
    moe_rows = min(MOE_ROWS, max(LANES, 2 * xn.shape[0] // N_EXPERTS))
    pos, tile_fill, tile_expert, tile_used, last_tile = _moe_plan(route, cnt, tm, moe_rows)
    xs = _dispatch(pos, tile_fill, xn, tile_expert.shape[0] * moe_rows, moe_rows, tm)
    ys = _experts(tile_expert, tile_used, last_tile, xs, w_gate, w_up, w_down)
    return _combine(pos, h, route, nf_g, ys, tm)


def _layer(x, conv_buf, shift_buf, wkv_state, lw, *, tm, conv_seqs, conv_rows, prep_rows,
           wkv_tblk):
    b, t, _ = x.shape
    n = b * t
    n_grp = b // WKV_BATCH
    x2d = x.reshape(n, D_MODEL)

    h_glu, z = _inproj(x2d, lw["norm1_g"], lw["w_in"], tm)

    h3 = h_glu.reshape(b, t, C_CONV)
    hist = jnp.concatenate([jnp.zeros((b, CONV_PAD - CONV_WIDTH + 1, C_CONV), F32), conv_buf], axis=1)
    n_hist = CONV_WIDTH - 1
    new_conv = h3[:, t - n_hist:] if t >= n_hist else jnp.concatenate([conv_buf[:, t:], h3], axis=1)
    conv_out = _conv(h3, hist, lw["conv_w"], lw["conv_b"], lw["conv_ln_g"], lw["conv_ln_b"],
                     conv_seqs, conv_rows)

    new_shift = z.reshape(b, t, D_SHIFT)[:, -1][:, INV_PERM_Z]
    shift_buf = shift_buf[:, PERM_Z]
    if prep_rows > t:
        period, tiles_per_seq = t, 1
        bnd = jnp.repeat(shift_buf, t, axis=0).reshape(n // prep_rows, prep_rows, D_SHIFT)
    else:
        period, tiles_per_seq = prep_rows, t // prep_rows
        bnd = shift_buf.reshape(b, 1, D_SHIFT)
    r, w, k, v, nkk, bb, bonus, g = _prep(z, bnd, lw["prep"], prep_rows, period, tiles_per_seq)

    y_cols, s_l = _wkv(w, nkk, bb, k, r, v, _state_to_lanes(wkv_state, n_grp),
                       t_blk=wkv_tblk, carry_state=(n_grp == 1))
    new_wkv = _state_from_lanes(s_l, n_grp)

    h, xn2, route, cnt = _outproj(x2d, conv_out.reshape(n, C_CONV), y_cols, bonus, g,
                                  lw["outproj"], tm)
    out = _moe(xn2, route, cnt, h, lw["w_gate"], lw["w_up"], lw["w_down"], lw["norm_f_g"], tm)
    return out.reshape(b, t, D_MODEL), new_conv, new_shift, new_wkv


def _prepare_weights(norm1_g, w_in, conv_w, conv_b, conv_ln_g, conv_ln_b, mu_shift, w0,
                     w_decay_up, a0, w_aaa_up, w_gate_up, k_k, k_a, r_k, gn_g, gn_b, w_out,
                     norm2_g, w_router_group, b_router_group, w_router_expert, b_router_expert,
                     w_exp_gate, w_exp_up, w_exp_down, norm_f_g):
    row = lambda a: a.reshape(1, -1)
    head = jnp.arange(C_RWKV) % N_HEADS
    ones_bd = (head[:, None] == head[None, :LANES]).astype(BF16)
    w_in_perm = jnp.concatenate([w_in[:, :2 * C_CONV], w_in[:, 2 * C_CONV:][:, PERM_Z]], axis=1)
    zeros_lora = jnp.zeros((LANES - w_decay_up.shape[0], C_RWKV), F32)
    wd_pad = jnp.concatenate([w_decay_up[:, PERM_KEY], zeros_lora], axis=0).astype(BF16)
    wa_pad = jnp.concatenate([zeros_lora, w_aaa_up[:, PERM_KEY]], axis=0).astype(BF16)
    pad_cols = ROUTE_COLS - N_EXPERT_GROUPS - N_EXPERTS
    w_route = jnp.concatenate([w_router_group, w_router_expert,
                               jnp.zeros((D_MODEL, pad_cols), F32)], axis=1)
    wr_both = jnp.concatenate(_split_bf16(w_route), axis=1)
    b_route = jnp.concatenate([b_router_group, b_router_expert, jnp.zeros((pad_cols,), F32)])
    return {
        "norm1_g": row(norm1_g), "w_in": w_in_perm.astype(BF16),
        "conv_w": conv_w, "conv_b": row(conv_b), "conv_ln_g": row(conv_ln_g),
        "conv_ln_b": row(conv_ln_b),
        "prep": (row(mu_shift[PERM_Z]), row(w0[PERM_KEY]), wd_pad, row(a0[PERM_KEY]), wa_pad,
                 w_gate_up[:, PERM_VAL].astype(BF16), row(k_k[PERM_KEY]), row(k_a[PERM_KEY]),
                 row(r_k[PERM_KEY]), ones_bd),
        "outproj": (row(gn_g[PERM_VAL]), row(gn_b[PERM_VAL]), ones_bd, w_out[:C_CONV].astype(BF16),
                    w_out[C_CONV:][PERM_VAL].astype(BF16), row(norm2_g), wr_both, row(b_route)),
        "w_gate": w_exp_gate.reshape(N_EXPERTS, D_MODEL, D_EXPERT),
        "w_up": w_exp_up.reshape(N_EXPERTS, D_MODEL, D_EXPERT),
        "w_down": w_exp_down.reshape(N_EXPERTS, D_EXPERT, D_MODEL),
        "norm_f_g": row(norm_f_g),
    }


def kernel(x_prompt, x_sample, state_conv, state_shift, state_wkv, norm1_g, w_in, conv_w, conv_b, conv_ln_g, conv_ln_b, mu_shift, w0, w_decay_up, a0, w_aaa_up, w_gate_up, k_k, k_a, r_k, gn_g, gn_b, w_out, norm2_g, w_router_group, b_router_group, w_router_expert, b_router_expert, w_exp_gate, w_exp_up, w_exp_down, norm_f_g):
    depth = norm1_g.shape[0]
    assert depth == 1, "single-layer trunk"
    b = x_prompt.shape[0]
    assert b == WKV_BATCH and x_prompt.shape[1] % LOAD_ROWS == 0
    assert x_sample.shape[0] * x_sample.shape[1] == WKV_BATCH * LOAD_ROWS
    lw = _prepare_weights(norm1_g[0], w_in[0], conv_w[0], conv_b[0], conv_ln_g[0], conv_ln_b[0],
                          mu_shift[0], w0[0], w_decay_up[0], a0[0], w_aaa_up[0], w_gate_up[0],
                          k_k[0], k_a[0], r_k[0], gn_g[0], gn_b[0], w_out[0], norm2_g[0],
                          w_router_group[0], b_router_group[0], w_router_expert[0],
                          b_router_expert[0], w_exp_gate[0], w_exp_up[0], w_exp_down[0], norm_f_g)
    dt = x_prompt.dtype
    y_p, conv_p, shift_p, wkv_p = _layer(
        x_prompt,
        jnp.zeros((b, CONV_WIDTH - 1, C_CONV), dt),
        jnp.zeros((b, D_SHIFT), dt),
        jnp.zeros((b, N_HEADS, HEAD_DIM, HEAD_DIM), dt), lw,
        tm=512, conv_seqs=1, conv_rows=128, prep_rows=256, wkv_tblk=LOAD_ROWS)
    y_s, conv_s, shift_s, wkv_s = _layer(
        x_sample, state_conv[0], state_shift[0], state_wkv[0], lw,
        tm=512, conv_seqs=16, conv_rows=8, prep_rows=256, wkv_tblk=x_sample.shape[1])
    return (y_p, y_s, conv_p[None], shift_p[None], wkv_p[None],
            conv_s[None], shift_s[None], wkv_s[None])
```

```python
import functools

import numpy as np
import jax
import jax.numpy as jnp
from jax import lax
from jax.experimental import pallas as pl
from jax.experimental.pallas import tpu as pltpu

F32 = jnp.float32
BF16 = jnp.bfloat16

D_MODEL = 1024
C_CONV = 512
C_RWKV = 512
HEAD_DIM = 64
N_HEADS = 8
CONV_WIDTH = 31
D_SHIFT = 1792
N_EXPERT_GROUPS = 4
EXPERTS_PER_GROUP = 8
N_EXPERTS = N_EXPERT_GROUPS * EXPERTS_PER_GROUP
D_EXPERT = 256
RMS_EPS = 1e-6
LN_EPS = 1e-5
GN_EPS = 64e-5

LANES = 128
SUBLANES = 8
CONV_PAD = 32
WKV_BATCH = 8
BH_PER_GROUP = WKV_BATCH * N_HEADS
K_HALF = HEAD_DIM // 2
WKV_PARTS = 2
LOAD_ROWS = LANES
ROUTE_COLS = LANES
EXPERT_COL0 = N_EXPERT_GROUPS
MOE_ROWS = 512
DMA_ISSUE_UNROLL = 8
VMEM_LIMIT = 56 * 1024 * 1024


def _key_perm():
    kp, kh, h = np.meshgrid(np.arange(K_HALF), np.arange(2), np.arange(N_HEADS), indexing="ij")
    return (h * HEAD_DIM + kh * K_HALF + kp).reshape(-1)


def _value_perm():
    v, h = np.meshgrid(np.arange(HEAD_DIM), np.arange(N_HEADS), indexing="ij")
    return (h * HEAD_DIM + v).reshape(-1)


PERM_KEY = _key_perm()
PERM_VAL = _value_perm()
PERM_Z = np.concatenate([PERM_KEY, C_RWKV + PERM_KEY, 2 * C_RWKV + PERM_VAL,
                         np.arange(3 * C_RWKV, D_SHIFT)])


def _reorder_last(x, split, order):
    lead = x.shape[:-1]
    k = len(lead)
    x = x.reshape(lead + split).transpose(tuple(range(k)) + tuple(k + o for o in order))
    return x.reshape(lead + (-1,))


def _key_order(x):
    return _reorder_last(x, (N_HEADS, 2, K_HALF), (2, 1, 0))


def _value_order(x):
    return _reorder_last(x, (N_HEADS, HEAD_DIM), (1, 0))


def _shift_order(x, inverse=False):
    if inverse:
        key = lambda a: _reorder_last(a, (K_HALF, 2, N_HEADS), (2, 1, 0))
        val = lambda a: _reorder_last(a, (HEAD_DIM, N_HEADS), (1, 0))
    else:
        key, val = _key_order, _value_order
    return jnp.concatenate([key(x[..., :C_RWKV]), key(x[..., C_RWKV:2 * C_RWKV]),
                            val(x[..., 2 * C_RWKV:3 * C_RWKV]), x[..., 3 * C_RWKV:]], axis=-1)


def _cparams(*sem):
    return pltpu.CompilerParams(dimension_semantics=sem, vmem_limit_bytes=VMEM_LIMIT)


def _split_bf16(x):
    hi = x.astype(BF16)
    lo = (x - hi.astype(F32)).astype(BF16)
    return hi, lo


def _segsum(x, same_head):
    hi, lo = _split_bf16(x)
    s = (jnp.dot(hi, same_head, preferred_element_type=F32)
         + jnp.dot(lo, same_head, preferred_element_type=F32))
    return jnp.concatenate([s] * (x.shape[1] // LANES), axis=1)


def _inproj_kernel(x_ref, g_ref, w_ref, h_ref, z_ref):
    x = x_ref[...]
    xn = x * lax.rsqrt(jnp.mean(x * x, axis=-1, keepdims=True) + RMS_EPS) * g_ref[...]
    p = jnp.dot(xn.astype(BF16), w_ref[...], preferred_element_type=F32)
    h_ref[...] = p[:, :C_CONV] * jax.nn.sigmoid(p[:, C_CONV:2 * C_CONV])
    z_ref[...] = p[:, 2 * C_CONV:]


def _inproj(x, g, w_bf, tm):
    n = x.shape[0]
    d_in = w_bf.shape[1]
    return pl.pallas_call(
        _inproj_kernel,
        grid=(n // tm,),
        in_specs=[pl.BlockSpec((tm, D_MODEL), lambda i: (i, 0)),
                  pl.BlockSpec((1, D_MODEL), lambda i: (0, 0)),
                  pl.BlockSpec((D_MODEL, d_in), lambda i: (0, 0))],
        out_specs=[pl.BlockSpec((tm, C_CONV), lambda i: (i, 0)),
                   pl.BlockSpec((tm, D_SHIFT), lambda i: (i, 0))],
        out_shape=[jax.ShapeDtypeStruct((n, C_CONV), F32),
                   jax.ShapeDtypeStruct((n, D_SHIFT), F32)],
        compiler_params=_cparams("parallel"),
        name="inproj",
    )(x, g, w_bf)


def _conv_kernel(h_ref, hist_ref, w_ref, b_ref, lg_ref, lb_ref, o_ref, win_ref, sh_ref,
                 *, rows, sub_rows, n_chunks, n_seq):
    shifted_rows = rows + CONV_PAD - SUBLANES

    def seq_body(s, _):
        def chunk_body(c, _):
            t0 = pl.multiple_of(c * rows, SUBLANES)
            if n_chunks == 1:
                win_ref[pl.ds(0, CONV_PAD), :] = hist_ref[s]
            else:
                prev0 = pl.multiple_of(jnp.maximum(t0 - CONV_PAD, 0), SUBLANES)
                win_ref[pl.ds(0, CONV_PAD), :] = jnp.where(
                    c == 0, hist_ref[s], h_ref[s, pl.ds(prev0, CONV_PAD), :])
            win_ref[pl.ds(CONV_PAD, rows), :] = h_ref[s, pl.ds(t0, rows), :]
            sh_ref[0] = win_ref[...]
            for sft in range(1, SUBLANES):
                sh_ref[sft, pl.ds(0, shifted_rows), :] = win_ref[pl.ds(sft, shifted_rows), :]
            for blk in range(rows // sub_rows):
                r0 = blk * sub_rows
                acc = jnp.zeros((sub_rows, C_CONV), F32) + b_ref[...]
                for j in range(CONV_WIDTH):
                    off = j + 2
                    acc = acc + (sh_ref[off % SUBLANES, pl.ds(off - off % SUBLANES + r0, sub_rows), :]
                                 * w_ref[pl.ds(j, 1), :])
                mu = jnp.mean(acc, axis=-1, keepdims=True)
                d = acc - mu
                var = jnp.mean(d * d, axis=-1, keepdims=True)
                of = d * lax.rsqrt(var + LN_EPS) * lg_ref[...] + lb_ref[...]
                o_ref[s, pl.ds(t0 + r0, sub_rows), :] = (of * jax.nn.sigmoid(of)).astype(o_ref.dtype)
            return 0
        lax.fori_loop(0, n_chunks, chunk_body, 0)
        return 0
    lax.fori_loop(0, n_seq, seq_body, 0)


def _conv(h, hist, conv_w, conv_b, ln_g, ln_b, n_seq, rows):
    b, t, _ = h.shape
    sub_rows = min(rows, 32)
    kern = functools.partial(_conv_kernel, rows=rows, sub_rows=sub_rows, n_chunks=t // rows,
                             n_seq=n_seq)
    vec = pl.BlockSpec((1, C_CONV), lambda i: (0, 0))
    return pl.pallas_call(
        kern,
        grid=(b // n_seq,),
        in_specs=[pl.BlockSpec((n_seq, t, C_CONV), lambda i: (i, 0, 0)),
                  pl.BlockSpec((n_seq, CONV_PAD, C_CONV), lambda i: (i, 0, 0)),
                  pl.BlockSpec((CONV_WIDTH, C_CONV), lambda i: (0, 0)),
                  vec, vec, vec],
        out_specs=pl.BlockSpec((n_seq, t, C_CONV), lambda i: (i, 0, 0)),
        out_shape=jax.ShapeDtypeStruct((b, t, C_CONV), BF16),
        scratch_shapes=[pltpu.VMEM((rows + CONV_PAD, C_CONV), F32),
                        pltpu.VMEM((SUBLANES, rows + CONV_PAD, C_CONV), F32)],
        compiler_params=_cparams("parallel"),
        name="conv",
    )(h, hist, conv_w, conv_b, ln_g, ln_b)


def _prep_kernel(z_ref, bnd_ref, halo_ref, mu_ref, w0_ref, wd_ref, a0_ref, wa_ref, wg_ref, kk_ref,
                 ka_ref, rk_ref, ones_ref,
                 r_ref, w_ref, k_ref, v_ref, nkk_ref, b_ref, bonus_ref, g_ref,
                 *, period, tiles_per_seq):
    z = z_ref[...]
    row_id = lax.broadcasted_iota(jnp.int32, z.shape, 0)
    bnd = bnd_ref[...]
    if tiles_per_seq > 1:
        first_tile = pl.program_id(0) % tiles_per_seq == 0
        bnd = jnp.where(first_tile, bnd, halo_ref[pl.ds(SUBLANES - 1, 1), :])
    z_prev = jnp.where(row_id % period == 0, bnd, pltpu.roll(z, 1, axis=0))
    zs = z + (z_prev - z) * mu_ref[...]
    r = zs[:, :C_RWKV]
    k = zs[:, C_RWKV:2 * C_RWKV]
    v = zs[:, 2 * C_RWKV:3 * C_RWKV]
    wa = zs[:, 3 * C_RWKV:3 * C_RWKV + LANES]
    gl = zs[:, 3 * C_RWKV + LANES:]
    ones_bd = ones_ref[...]

    lw = w0_ref[...] + jnp.dot(jnp.tanh(wa).astype(BF16), wd_ref[...], preferred_element_type=F32)
    w_log = -(jnp.maximum(-lw, 0.0) + jnp.log(1.0 + jnp.exp(-jnp.abs(lw)))) - 0.5
    decay = jnp.exp(-jnp.exp(w_log))
    a = jax.nn.sigmoid(a0_ref[...] + jnp.dot(wa.astype(BF16), wa_ref[...],
                                             preferred_element_type=F32))
    g = jnp.dot(jax.nn.sigmoid(gl).astype(BF16), wg_ref[...], preferred_element_type=F32)

    kk = k * kk_ref[...]
    norm = jnp.maximum(jnp.sqrt(_segsum(kk * kk, ones_bd)), 1e-12)
    kk = kk / norm
    k_mod = k * (1.0 + (a - 1.0) * ka_ref[...])
    bonus = _segsum(r * k_mod * rk_ref[...], ones_bd) * v

    def store_cols(ref, x):
        xt = x.T
        ntok = ref.shape[2]
        for q in range(ref.shape[0]):
            ref[q] = xt[:, q * ntok:(q + 1) * ntok]

    store_cols(r_ref, r)
    store_cols(w_ref, decay)
    store_cols(k_ref, k_mod)
    store_cols(v_ref, v)
    store_cols(nkk_ref, -kk)
    store_cols(b_ref, kk * a)
    bonus_ref[...] = bonus
    g_ref[...] = g


def _col_major_spec(rows, length):
    if rows <= length:
        per_b = length // rows
        return pl.BlockSpec((1, C_RWKV, rows), lambda i: (i // per_b, 0, i % per_b))
    return pl.BlockSpec((rows // length, C_RWKV, length), lambda i: (i, 0, 0))


def _prep(z2d, bnd, params, rows, period, tiles_per_seq):
    n = z2d.shape[0]
    length = n // WKV_BATCH
    (mu, w0, wd_pad, a0, wa_pad, wg, k_k, k_a, r_k, ones_bd) = params
    kern = functools.partial(_prep_kernel, period=period, tiles_per_seq=tiles_per_seq)
    halo_blocks = rows // SUBLANES
    vec = lambda c: pl.BlockSpec((1, c), lambda i: (0, 0))
    mat = lambda a: pl.BlockSpec(a.shape, lambda i: (0, 0))
    row_spec = pl.BlockSpec((rows, C_RWKV), lambda i: (i, 0))
    row_sd = jax.ShapeDtypeStruct((n, C_RWKV), F32)
    col_spec = _col_major_spec(rows, length)
    col_sd = jax.ShapeDtypeStruct((WKV_BATCH, C_RWKV, length), F32)
    return pl.pallas_call(
        kern,
        grid=(n // rows,),
        in_specs=[pl.BlockSpec((rows, D_SHIFT), lambda i: (i, 0)),
                  pl.BlockSpec((None, bnd.shape[1], D_SHIFT), lambda i: (i // tiles_per_seq, 0, 0)),
                  pl.BlockSpec((SUBLANES, D_SHIFT),
                               lambda i: (jnp.maximum(i * halo_blocks - 1, 0), 0)),
                  vec(D_SHIFT), vec(C_RWKV), mat(wd_pad), vec(C_RWKV), mat(wa_pad), mat(wg),
                  vec(C_RWKV), vec(C_RWKV), vec(C_RWKV), mat(ones_bd)],
        out_specs=[col_spec] * 6 + [row_spec] * 2,
        out_shape=[col_sd] * 6 + [row_sd] * 2,
        compiler_params=_cparams("parallel"),
        name="rwkv_prep",
    )(z2d, bnd, z2d, mu, w0, wd_pad, a0, wa_pad, wg, k_k, k_a, r_k, ones_bd)


def _fold_halves(x):
    return x + pltpu.roll(x, BH_PER_GROUP, axis=x.ndim - 1)


def _key_cols_to_lanes(src_ref, dst_ref):
    chunk = 2 * N_HEADS
    for kp in range(K_HALF):
        pieces = [src_ref[b8, pl.ds(kp * chunk + half * N_HEADS, N_HEADS), :]
                  for half in range(2) for b8 in range(WKV_BATCH)]
        dst_ref[kp, :LOAD_ROWS, :] = jnp.concatenate(pieces, axis=0).T


def _value_cols_to_lanes(src_ref, dst_ref):
    for v in range(HEAD_DIM):
        pieces = [src_ref[b8, pl.ds(v * N_HEADS, N_HEADS), :]
                  for _ in range(2) for b8 in range(WKV_BATCH)]
        dst_ref[pl.ds(v, LOAD_ROWS, stride=HEAD_DIM), :] = jnp.concatenate(pieces, axis=0).T


def _lanes_to_value_cols(y_s, yt_ref):
    for v in range(HEAD_DIM):
        yt = y_s[pl.ds(v, LOAD_ROWS, stride=HEAD_DIM), :].T
        folded = yt[:BH_PER_GROUP] + yt[BH_PER_GROUP:]
        for b8 in range(WKV_BATCH):
            yt_ref[b8, pl.ds(v * N_HEADS, N_HEADS), :] = folded[b8 * N_HEADS:(b8 + 1) * N_HEADS]


def _wkv_kernel(w_ref, nkk_ref, b_ref, k_ref, r_ref, v_ref, s0_ref, y_ref, sout_ref,
                w_s, nkk_s, b_s, k_s, r_s, v_s, y_s, s_ref,
                *, t_blk, steps_per_load, carry_state, n_parts):
    j = pl.program_id(0)
    sub = j % steps_per_load
    part = HEAD_DIM // n_parts
    parts = [slice(i * part, (i + 1) * part) for i in range(n_parts)]

    @pl.when(sub == 0)
    def _():
        for src, dst in ((w_ref, w_s), (nkk_ref, nkk_s), (b_ref, b_s), (k_ref, k_s), (r_ref, r_s)):
            _key_cols_to_lanes(src, dst)
        _value_cols_to_lanes(v_ref, v_s)

    if carry_state:
        @pl.when(j == 0)
        def _():
            s_ref[...] = s0_ref[0]
    else:
        s_ref[...] = s0_ref[0]

    tok0 = sub * t_blk
    acc = jnp.zeros((HEAD_DIM, LANES), F32)
    for kp in range(K_HALF):
        acc = acc + s_ref[kp] * nkk_s[kp, pl.ds(tok0, 1), :]
    sa_init = tuple(_fold_halves(acc[p, :]) for p in parts[:-1]) + (acc[parts[-1], :],)

    def part_step(tok, nxt_tok, rows, sa):
        cur = pl.ds(tok, 1)
        vrows = pl.ds(pl.multiple_of(tok * HEAD_DIM, HEAD_DIM) + rows.start, part)
        vv = v_s[vrows, :]
        yacc = jnp.zeros((part, LANES), F32)
        sacc = jnp.zeros((part, LANES), F32)
        for kp in range(K_HALF):
            s = (s_ref[kp, rows, :] * w_s[kp, cur, :] + sa * b_s[kp, cur, :]
                 + vv * k_s[kp, cur, :])
            s_ref[kp, rows, :] = s
            yacc = yacc + s * r_s[kp, cur, :]
            sacc = sacc + s * nkk_s[kp, pl.ds(nxt_tok, 1), :]
        y_s[vrows, :] = yacc
        return sacc

    def step(t, sa_parts):
        tok = tok0 + t
        nxt_tok = tok0 + jnp.minimum(t + 1, t_blk - 1)
        sa_last = _fold_halves(sa_parts[-1])
        nxt = [_fold_halves(part_step(tok, nxt_tok, rows, sa))
               for rows, sa in zip(parts[:-1], sa_parts[:-1])]
        return tuple(nxt) + (part_step(tok, nxt_tok, parts[-1], sa_last),)

    lax.fori_loop(0, t_blk, step, sa_init)

    if carry_state:
        @pl.when(j == pl.num_programs(0) - 1)
        def _():
            sout_ref[0] = s_ref[...]
    else:
        sout_ref[0] = s_ref[...]

    @pl.when(sub == steps_per_load - 1)
    def _():
        _lanes_to_value_cols(y_s, y_ref)


def _wkv(w, nkk, b, k, r, v, s0, *, t_blk, carry_state):
    length = w.shape[2]
    steps_per_load = LOAD_ROWS // t_blk
    n_steps = length // t_blk
    tok_spec = pl.BlockSpec((WKV_BATCH, C_RWKV, LOAD_ROWS), lambda j: (0, 0, j // steps_per_load))
    state_blk = (1, K_HALF, HEAD_DIM, LANES)
    if carry_state:
        sspec = pl.BlockSpec(state_blk, lambda j: (0, 0, 0, 0))
    else:
        sspec = pl.BlockSpec(state_blk, lambda j: (j, 0, 0, 0))
    key_rows = pltpu.VMEM((K_HALF, LOAD_ROWS + SUBLANES, LANES), F32)
    val_rows = pltpu.VMEM((LOAD_ROWS * HEAD_DIM, LANES), F32)
    return pl.pallas_call(
        functools.partial(_wkv_kernel, t_blk=t_blk, steps_per_load=steps_per_load,
                          carry_state=carry_state, n_parts=WKV_PARTS),
        grid=(n_steps,),
        in_specs=[tok_spec] * 6 + [sspec],
        out_specs=[tok_spec, sspec],
        out_shape=[jax.ShapeDtypeStruct(w.shape, F32), jax.ShapeDtypeStruct(s0.shape, F32)],
        scratch_shapes=[key_rows] * 5 + [val_rows, val_rows,
                                         pltpu.VMEM((K_HALF, HEAD_DIM, LANES), F32)],
        compiler_params=_cparams("arbitrary"),
        name="wkv",
    )(w, nkk, b, k, r, v, s0)


def _state_to_lanes(s, n_grp):
    s = s.reshape(WKV_BATCH, n_grp, N_HEADS, HEAD_DIM, 2, K_HALF)
    return s.transpose(1, 5, 3, 4, 0, 2).reshape(n_grp, K_HALF, HEAD_DIM, LANES)


def _state_from_lanes(s, n_grp):
    s = s.reshape(n_grp, K_HALF, HEAD_DIM, 2, WKV_BATCH, N_HEADS)
    return s.transpose(4, 0, 5, 2, 3, 1).reshape(n_grp * WKV_BATCH, N_HEADS, HEAD_DIM, HEAD_DIM)


def _outproj_kernel(x_ref, c_ref, y_ref, bonus_ref, g_ref, gng_ref, gnb_ref, ones_ref,
                    wo_c_ref, wo_r_ref, n2_ref, wr_both_ref, br_ref,
                    h_ref, xn_ref, route_ref, cnt_ref):
    ones_bd = ones_ref[...]
    y = jnp.concatenate([y_ref[q].T for q in range(y_ref.shape[0])], axis=0)
    inv_n = 1.0 / HEAD_DIM
    mu = _segsum(y, ones_bd) * inv_n
    d = y - mu
    var = _segsum(d * d, ones_bd) * inv_n
    yn = d * lax.rsqrt(var + GN_EPS) * gng_ref[...] + gnb_ref[...]
    rw = (yn + bonus_ref[...]) * g_ref[...]
    h = (x_ref[...]
         + jnp.dot(c_ref[...], wo_c_ref[...], preferred_element_type=F32)
         + jnp.dot(rw.astype(BF16), wo_r_ref[...], preferred_element_type=F32))
    h_ref[...] = h
    xn = h * lax.rsqrt(jnp.mean(h * h, axis=-1, keepdims=True) + RMS_EPS) * n2_ref[...]
    xn_ref[...] = xn

    x_hi, x_lo = _split_bf16(xn)
    hi_both = jnp.dot(x_hi, wr_both_ref[...], preferred_element_type=F32)
    logits = (hi_both[:, :ROUTE_COLS]
              + jnp.dot(x_lo, wr_both_ref[:, :ROUTE_COLS], preferred_element_type=F32)
              + hi_both[:, ROUTE_COLS:]) + br_ref[...]

    col = lax.broadcasted_iota(jnp.int32, logits.shape, 1)
    neg = jnp.float32(-jnp.inf)
    big = jnp.int32(ROUTE_COLS)
    is_g = col < N_EXPERT_GROUPS
    g_logit = jnp.where(is_g, logits, neg)
    g_max = jnp.max(g_logit, axis=-1, keepdims=True)
    g_idx = jnp.min(jnp.where(g_logit == g_max, col, big), axis=-1, keepdims=True)
    p_group = 1.0 / jnp.sum(jnp.where(is_g, jnp.exp(logits - g_max), 0.0), axis=-1, keepdims=True)
    lo_col = EXPERT_COL0 + g_idx * EXPERTS_PER_GROUP
    in_grp = (col >= lo_col) & (col < lo_col + EXPERTS_PER_GROUP)
    e1 = jnp.where(in_grp, logits, neg)
    m1 = jnp.max(e1, axis=-1, keepdims=True)
    i1 = jnp.min(jnp.where(e1 == m1, col, big), axis=-1, keepdims=True)
    e2 = jnp.where(col == i1, neg, e1)
    m2 = jnp.max(e2, axis=-1, keepdims=True)
    i2 = jnp.min(jnp.where(e2 == m2, col, big), axis=-1, keepdims=True)
    ex = jnp.exp(m2 - m1)
    w1 = p_group / (1.0 + ex)
    w2 = p_group * ex / (1.0 + ex)

    exp1 = i1 - EXPERT_COL0
    exp2 = i2 - EXPERT_COL0
    oh1 = jnp.where(col == exp1, 1.0, 0.0)
    oh2 = jnp.where(col == exp2, 1.0, 0.0)
    tm = logits.shape[0]
    earlier = jnp.where(lax.broadcasted_iota(jnp.int32, (tm, tm), 0)
                        > lax.broadcasted_iota(jnp.int32, (tm, tm), 1), 1.0, 0.0).astype(BF16)
    before1 = jnp.dot(earlier, oh1.astype(BF16), preferred_element_type=F32)
    before2 = jnp.dot(earlier, oh2.astype(BF16), preferred_element_type=F32)
    cnt1 = jnp.sum(oh1, axis=0, keepdims=True)
    cnt2 = jnp.sum(oh2, axis=0, keepdims=True)
    rank1 = jnp.sum(before1 * oh1, axis=-1, keepdims=True)
    rank2 = jnp.sum((before2 + cnt1) * oh2, axis=-1, keepdims=True)
    fields = (exp1.astype(F32), exp2.astype(F32), w1, w2, rank1, rank2)
    route = jnp.zeros_like(logits)
    for c, val in enumerate(fields):
        route = jnp.where(col == c, val, route)
    route_ref[...] = route
    cnt_ref[...] = jnp.broadcast_to(cnt1 + cnt2, cnt_ref.shape)


def _outproj(x, conv_out, y, bonus, g, params, tm):
    n = x.shape[0]
    (gn_g, gn_b, ones_bd, wo_c, wo_r, n2_g, wr_both, br) = params
    row = lambda c: pl.BlockSpec((tm, c), lambda i: (i, 0))
    vec = lambda c: pl.BlockSpec((1, c), lambda i: (0, 0))
    mat = lambda a: pl.BlockSpec(a.shape, lambda i: (0, 0))
    return pl.pallas_call(
        _outproj_kernel,
        grid=(n // tm,),
        in_specs=[row(D_MODEL), row(C_CONV), _col_major_spec(tm, n // WKV_BATCH), row(C_RWKV),
                  row(C_RWKV),
                  vec(C_RWKV), vec(C_RWKV), mat(ones_bd), mat(wo_c), mat(wo_r), vec(D_MODEL),
                  mat(wr_both), vec(ROUTE_COLS)],
        out_specs=[row(D_MODEL), row(D_MODEL), row(ROUTE_COLS),
                   pl.BlockSpec((None, SUBLANES, ROUTE_COLS), lambda i: (i, 0, 0))],
        out_shape=[jax.ShapeDtypeStruct((n, D_MODEL), F32),
                   jax.ShapeDtypeStruct((n, D_MODEL), F32),
                   jax.ShapeDtypeStruct((n, ROUTE_COLS), F32),
                   jax.ShapeDtypeStruct((n // tm, SUBLANES, ROUTE_COLS), F32)],
        compiler_params=_cparams("parallel"),
        name="outproj_route",
    )(x, conv_out, y, bonus, g, gn_g, gn_b, ones_bd, wo_c, wo_r, n2_g, wr_both, br)


def _moe_plan(route, cnt, tm, moe_rows):
    n = route.shape[0]
    ids = route[:, 0:2].astype(jnp.int32)
    ranks = route[:, 4:6].astype(jnp.int32)
    tile_cnt = cnt[:, 0, :N_EXPERTS].astype(jnp.int32)
    before_tile = jnp.cumsum(tile_cnt, axis=0) - tile_cnt
    total = jnp.sum(tile_cnt, axis=0)
    tiles_e = (total + moe_rows - 1) // moe_rows
    tile_end = jnp.cumsum(tiles_e)
    row_start = (tile_end - tiles_e) * moe_rows
    first_row = (row_start[None, :] + before_tile)[:, None, None, :]
    one_hot = ids.reshape(-1, tm, 2, 1) == jnp.arange(N_EXPERTS, dtype=jnp.int32)
    pos = jnp.sum(jnp.where(one_hot, first_row, 0), axis=-1).reshape(n, 2) + ranks
    n_row_tiles = 2 * n // moe_rows + N_EXPERTS
    t = jnp.arange(n_row_tiles, dtype=jnp.int32)
    tile_expert = jnp.minimum(jnp.sum((t[:, None] >= tile_end[None, :]).astype(jnp.int32), axis=1),
                              N_EXPERTS - 1)
    tile_used = (t < tile_end[-1]).astype(jnp.int32)
    last_tile = jnp.maximum(tile_end[-1:] - 1, 0).astype(jnp.int32)
    is_last = jnp.any((t[:, None] == tile_end[None, :] - 1) & (tiles_e[None, :] > 0), axis=1)
    tile_fill = (is_last | (tile_used == 0)).astype(jnp.int32)
    return pos.reshape(-1), tile_fill, tile_expert.astype(jnp.int32), tile_used, last_tile


def _row_copy_wait(src_ref, dst_ref, sem, rows):
    pltpu.make_async_copy(src_ref.at[pl.ds(0, rows), :], dst_ref.at[pl.ds(0, rows), :], sem).wait()


def _dispatch_kernel(pos_ref, fill_ref, *refs, tm, group_tiles):
    x_refs, (xs_ref, zero_ref, sem) = refs[:len(group_tiles)], refs[len(group_tiles):]
    base = pl.program_id(0) * (2 * tm)
    fill_rows = zero_ref.shape[0]

    @pl.when(pl.program_id(0) == 0)
    def _():
        zero_ref[...] = jnp.zeros_like(zero_ref)
        n_row_tiles = xs_ref.shape[0] // fill_rows

        def fill(t, _):
            @pl.when(fill_ref[t] != 0)
            def _():
                start = pl.multiple_of(t * fill_rows, fill_rows)
                pltpu.make_async_copy(zero_ref, xs_ref.at[pl.ds(start, fill_rows), :], sem).start()
            return 0
        lax.fori_loop(0, n_row_tiles, fill, 0)

        def fill_wait(t, _):
            @pl.when(fill_ref[t] != 0)
            def _():
                pltpu.make_async_copy(zero_ref, xs_ref.at[pl.ds(0, fill_rows), :], sem).wait()
            return 0
        lax.fori_loop(0, n_row_tiles, fill_wait, 0)

    for x_ref, in_group in zip(x_refs, _group_preds(pl.program_id(0), group_tiles)):
        @pl.when(in_group)
        def _(x_ref=x_ref):
            def issue(r, _):
                src = x_ref.at[pl.ds(r, 1), :]
                for c in range(2):
                    dst = xs_ref.at[pl.ds(pos_ref[base + 2 * r + c], 1), :]
                    pltpu.make_async_copy(src, dst, sem).start()
                return 0
            lax.fori_loop(0, tm, issue, 0, unroll=DMA_ISSUE_UNROLL)
    for _ in range(2):
        _row_copy_wait(x_refs[0], xs_ref, sem, tm)


def _group_preds(i, group_tiles):
    preds, start = [], 0
    for tiles in group_tiles:
        preds.append((i >= start) & (i < start + tiles))
        start += tiles
    return preds


def _group_specs(block, group_tiles):
    specs, start = [], 0
    for tiles in group_tiles:
        specs.append(pl.BlockSpec(
            block, lambda i, *_, start=start, tiles=tiles: (jnp.clip(i - start, 0, tiles - 1), 0)))
        start += tiles
    return specs


def _dispatch(pos, tile_fill, xns, n_rows, moe_rows, tm):
    group_tiles = tuple(x.shape[0] // tm for x in xns)
    return pl.pallas_call(
        functools.partial(_dispatch_kernel, tm=tm, group_tiles=group_tiles),
        grid_spec=pltpu.PrefetchScalarGridSpec(
            num_scalar_prefetch=2,
            grid=(sum(group_tiles),),
            in_specs=_group_specs((tm, D_MODEL), group_tiles),
            out_specs=pl.BlockSpec(memory_space=pl.ANY),
            scratch_shapes=[pltpu.VMEM((moe_rows, D_MODEL), F32), pltpu.SemaphoreType.DMA(())]),
        out_shape=jax.ShapeDtypeStruct((n_rows, D_MODEL), F32),
        compiler_params=_cparams("arbitrary"),
        name="moe_dispatch",
    )(pos, tile_fill, *xns)


def _experts_kernel(expert_ref, used_ref, last_ref, xs_ref, wg_ref, wu_ref, wd_ref, y_ref,
                    wg_bf, wu_bf, wd_bf):
    del last_ref
    t = pl.program_id(0)

    @pl.when((t == 0) | (expert_ref[t] != expert_ref[jnp.maximum(t - 1, 0)]))
    def _():
        wg_bf[...] = wg_ref[0].astype(BF16)
        wu_bf[...] = wu_ref[0].astype(BF16)
        wd_bf[...] = wd_ref[0].astype(BF16)

    @pl.when(used_ref[t] != 0)
    def _():
        x = xs_ref[...].astype(BF16)
        hg = jnp.dot(x, wg_bf[...], preferred_element_type=F32)
        hu = jnp.dot(x, wu_bf[...], preferred_element_type=F32)
        act = hg * jax.nn.sigmoid(hg) * hu
        y_ref[...] = jnp.dot(act.astype(BF16), wd_bf[...], preferred_element_type=F32)

    @pl.when(used_ref[t] == 0)
    def _():
        y_ref[...] = jnp.zeros_like(y_ref)


def _experts(tile_expert, tile_used, last_tile, xs, w_gate, w_up, w_down):
    n_rows = xs.shape[0]
    moe_rows = n_rows // tile_expert.shape[0]
    row_tile = lambda t, ex, used, last: (jnp.where(used[t] != 0, t, last[0]), 0)
    weight = lambda t, ex, used, last: (ex[t], 0, 0)
    return pl.pallas_call(
        _experts_kernel,
        grid_spec=pltpu.PrefetchScalarGridSpec(
            num_scalar_prefetch=3,
            grid=(n_rows // moe_rows,),
            in_specs=[pl.BlockSpec((moe_rows, D_MODEL), row_tile),
                      pl.BlockSpec((1, D_MODEL, D_EXPERT), weight),
                      pl.BlockSpec((1, D_MODEL, D_EXPERT), weight),
                      pl.BlockSpec((1, D_EXPERT, D_MODEL), weight)],
            out_specs=pl.BlockSpec((moe_rows, D_MODEL), lambda t, ex, used, last: (t, 0)),
            scratch_shapes=[pltpu.VMEM((D_MODEL, D_EXPERT), BF16), pltpu.VMEM((D_MODEL, D_EXPERT), BF16),
                            pltpu.VMEM((D_EXPERT, D_MODEL), BF16)]),
        out_shape=jax.ShapeDtypeStruct((n_rows, D_MODEL), F32),
        compiler_params=_cparams("arbitrary"),
        name="moe_experts",
    )(tile_expert, tile_used, last_tile, xs, w_gate, w_up, w_down)


def _combine_kernel(pos_ref, *refs, tm, group_tiles):
    n_grp = len(group_tiles)
    h_refs, (route_ref, nf_ref, ys_ref) = refs[:n_grp], refs[n_grp:n_grp + 3]
    o_refs, (y_buf, sem) = refs[n_grp + 3:2 * n_grp + 3], refs[2 * n_grp + 3:]
    i = pl.program_id(0)
    last = pl.num_programs(0) - 1
    in_group = _group_preds(i, group_tiles)

    def issue_tile(base, slot):
        def issue(r, _):
            for c in range(2):
                src = ys_ref.at[pl.ds(pos_ref[base + 2 * r + c], 1), :]
                pltpu.make_async_copy(src, y_buf.at[slot, c, pl.ds(r, 1), :], sem.at[slot]).start()
            return 0
        lax.fori_loop(0, tm, issue, 0, unroll=DMA_ISSUE_UNROLL)

    def wait(slot):
        for c in range(2):
            _row_copy_wait(ys_ref, y_buf.at[slot, c], sem.at[slot], tm)

    @pl.when(i == 0)
    def _():
        issue_tile(0, 0)

    next_base = jnp.minimum(i + 1, last) * (2 * tm)

    def run(slot):
        other = 1 - slot
        issue_tile(next_base, other)
        wait(slot)
        route = route_ref[...]
        col = lax.broadcasted_iota(jnp.int32, route.shape, 1)
        w1 = jnp.sum(jnp.where(col == 2, route, 0.0), axis=-1, keepdims=True)
        w2 = jnp.sum(jnp.where(col == 3, route, 0.0), axis=-1, keepdims=True)
        moe = w1 * y_buf[slot, 0] + w2 * y_buf[slot, 1]
        for h_ref, o_ref, mine in zip(h_refs, o_refs, in_group):
            @pl.when(mine)
            def _(h_ref=h_ref, o_ref=o_ref):
                h = h_ref[...] + moe
                o_ref[...] = (h * lax.rsqrt(jnp.mean(h * h, axis=-1, keepdims=True) + RMS_EPS)
                              * nf_ref[...])

        @pl.when(i == last)
        def _():
            wait(other)

    for slot in range(2):
        pl.when(i % 2 == slot)(functools.partial(run, slot))


def _combine(pos, hs, route, nf_g, ys, tm):
    group_tiles = tuple(h.shape[0] // tm for h in hs)
    row_specs = _group_specs((tm, D_MODEL), group_tiles)
    return pl.pallas_call(
        functools.partial(_combine_kernel, tm=tm, group_tiles=group_tiles),
        grid_spec=pltpu.PrefetchScalarGridSpec(
            num_scalar_prefetch=1,
            grid=(sum(group_tiles),),
            in_specs=row_specs + [pl.BlockSpec((tm, ROUTE_COLS), lambda i, pos: (i, 0)),
                                  pl.BlockSpec((1, D_MODEL), lambda i, pos: (0, 0)),
                                  pl.BlockSpec(memory_space=pl.ANY)],
            out_specs=row_specs,
            scratch_shapes=[pltpu.VMEM((2, 2, tm, D_MODEL), F32), pltpu.SemaphoreType.DMA((2,))]),
        out_shape=[jax.ShapeDtypeStruct(h.shape, F32) for h in hs],
        compiler_params=_cparams("arbitrary"),
        name="moe_combine",
    )(pos, *hs, route, nf_g, ys)


def _moe(xns, routes, cnts, hs, w_gate, w_up, w_down, nf_g, tm):
    route = jnp.concatenate(routes, axis=0)
    cnt = jnp.concatenate(cnts, axis=0)
    moe_rows = min(MOE_ROWS, max(LANES, 2 * route.shape[0] // N_EXPERTS))
    pos, tile_fill, tile_expert, tile_used, last_tile = _moe_plan(route, cnt, tm, moe_rows)
    xs = _dispatch(pos, tile_fill, xns, tile_expert.shape[0] * moe_rows, moe_rows, tm)
    ys = _experts(tile_expert, tile_used, last_tile, xs, w_gate, w_up, w_down)
    return _combine(pos, hs, route, nf_g, ys, tm)


def _layer(x, conv_buf, shift_buf, wkv_state, lw, *, tm, conv_seqs, conv_rows, prep_rows,
           wkv_tblk):
    b, t, _ = x.shape
    n = b * t
    n_grp = b // WKV_BATCH
    x2d = x.reshape(n, D_MODEL)

    h_glu, z = _inproj(x2d, lw["norm1_g"], lw["w_in"], tm)

    h3 = h_glu.reshape(b, t, C_CONV)
    hist = jnp.concatenate([jnp.zeros((b, CONV_PAD - CONV_WIDTH + 1, C_CONV), F32), conv_buf], axis=1)
    n_hist = CONV_WIDTH - 1
    new_conv = h3[:, t - n_hist:] if t >= n_hist else jnp.concatenate([conv_buf[:, t:], h3], axis=1)
    conv_out = _conv(h3, hist, lw["conv_w"], lw["conv_b"], lw["conv_ln_g"], lw["conv_ln_b"],
                     conv_seqs, conv_rows)

    new_shift = _shift_order(z.reshape(b, t, D_SHIFT)[:, -1], inverse=True)
    shift_buf = _shift_order(shift_buf)
    if prep_rows > t:
        period, tiles_per_seq = t, 1
        bnd = jnp.repeat(shift_buf, t, axis=0).reshape(n // prep_rows, prep_rows, D_SHIFT)
    else:
        period, tiles_per_seq = prep_rows, t // prep_rows
        bnd = shift_buf.reshape(b, 1, D_SHIFT)
    r, w, k, v, nkk, bb, bonus, g = _prep(z, bnd, lw["prep"], prep_rows, period, tiles_per_seq)

    y_cols, s_l = _wkv(w, nkk, bb, k, r, v, _state_to_lanes(wkv_state, n_grp),
                       t_blk=wkv_tblk, carry_state=(n_grp == 1))
    new_wkv = _state_from_lanes(s_l, n_grp)

    moe_in = _outproj(x2d, conv_out.reshape(n, C_CONV), y_cols, bonus, g, lw["outproj"], tm)
    return moe_in, new_conv, new_shift, new_wkv


def _prepare_weights(norm1_g, w_in, conv_w, conv_b, conv_ln_g, conv_ln_b, mu_shift, w0,
                     w_decay_up, a0, w_aaa_up, w_gate_up, k_k, k_a, r_k, gn_g, gn_b, w_out,
                     norm2_g, w_router_group, b_router_group, w_router_expert, b_router_expert,
                     w_exp_gate, w_exp_up, w_exp_down, norm_f_g):
    row = lambda a: a.reshape(1, -1)
    head = jnp.arange(C_RWKV) % N_HEADS
    ones_bd = (head[:, None] == head[None, :LANES]).astype(BF16)
    w_in_perm = jnp.concatenate([w_in[:, :2 * C_CONV], _shift_order(w_in[:, 2 * C_CONV:])], axis=1)
    zeros_lora = jnp.zeros((LANES - w_decay_up.shape[0], C_RWKV), F32)
    wd_pad = jnp.concatenate([_key_order(w_decay_up), zeros_lora], axis=0).astype(BF16)
    wa_pad = jnp.concatenate([zeros_lora, _key_order(w_aaa_up)], axis=0).astype(BF16)
    key_vecs = _key_order(jnp.stack([w0, a0, k_k, k_a, r_k]))
    val_vecs = _value_order(jnp.stack([gn_g, gn_b]))
    w_out_r = (w_out[C_CONV:].reshape(N_HEADS, HEAD_DIM, D_MODEL).transpose(1, 0, 2)
               .reshape(C_RWKV, D_MODEL))
    pad_cols = ROUTE_COLS - N_EXPERT_GROUPS - N_EXPERTS
    w_route = jnp.concatenate([w_router_group, w_router_expert,
                               jnp.zeros((D_MODEL, pad_cols), F32)], axis=1)
    wr_both = jnp.concatenate(_split_bf16(w_route), axis=1)
    b_route = jnp.concatenate([b_router_group, b_router_expert, jnp.zeros((pad_cols,), F32)])
    return {
        "norm1_g": row(norm1_g), "w_in": w_in_perm.astype(BF16),
        "conv_w": conv_w, "conv_b": row(conv_b), "conv_ln_g": row(conv_ln_g),
        "conv_ln_b": row(conv_ln_b),
        "prep": (row(_shift_order(mu_shift)), key_vecs[0:1], wd_pad, key_vecs[1:2], wa_pad,
                 _value_order(w_gate_up).astype(BF16), key_vecs[2:3], key_vecs[3:4], key_vecs[4:5],
                 ones_bd),
        "outproj": (val_vecs[0:1], val_vecs[1:2], ones_bd, w_out[:C_CONV].astype(BF16),
                    w_out_r.astype(BF16), row(norm2_g), wr_both, row(b_route)),
        "w_gate": w_exp_gate.reshape(N_EXPERTS, D_MODEL, D_EXPERT),
        "w_up": w_exp_up.reshape(N_EXPERTS, D_MODEL, D_EXPERT),
        "w_down": w_exp_down.reshape(N_EXPERTS, D_EXPERT, D_MODEL),
        "norm_f_g": row(norm_f_g),
    }


def kernel(x_prompt, x_sample, state_conv, state_shift, state_wkv, norm1_g, w_in, conv_w, conv_b, conv_ln_g, conv_ln_b, mu_shift, w0, w_decay_up, a0, w_aaa_up, w_gate_up, k_k, k_a, r_k, gn_g, gn_b, w_out, norm2_g, w_router_group, b_router_group, w_router_expert, b_router_expert, w_exp_gate, w_exp_up, w_exp_down, norm_f_g):
    depth = norm1_g.shape[0]
    assert depth == 1, "single-layer trunk"
    b = x_prompt.shape[0]
    assert b == WKV_BATCH and x_prompt.shape[1] % LOAD_ROWS == 0
    assert x_sample.shape[0] * x_sample.shape[1] == WKV_BATCH * LOAD_ROWS
    lw = _prepare_weights(norm1_g[0], w_in[0], conv_w[0], conv_b[0], conv_ln_g[0], conv_ln_b[0],
                          mu_shift[0], w0[0], w_decay_up[0], a0[0], w_aaa_up[0], w_gate_up[0],
                          k_k[0], k_a[0], r_k[0], gn_g[0], gn_b[0], w_out[0], norm2_g[0],
                          w_router_group[0], b_router_group[0], w_router_expert[0],
                          b_router_expert[0], w_exp_gate[0], w_exp_up[0], w_exp_down[0], norm_f_g)
    dt = x_prompt.dtype
    tm = 512
    moe_p, conv_p, shift_p, wkv_p = _layer(
        x_prompt,
        jnp.zeros((b, CONV_WIDTH - 1, C_CONV), dt),
        jnp.zeros((b, D_SHIFT), dt),
        jnp.zeros((b, N_HEADS, HEAD_DIM, HEAD_DIM), dt), lw,
        tm=tm, conv_seqs=1, conv_rows=128, prep_rows=256, wkv_tblk=LOAD_ROWS)
    moe_s, conv_s, shift_s, wkv_s = _layer(
        x_sample, state_conv[0], state_shift[0], state_wkv[0], lw,
        tm=tm, conv_seqs=16, conv_rows=8, prep_rows=256, wkv_tblk=x_sample.shape[1])
    hs, xns, routes, cnts = zip(moe_p, moe_s)
    y_p, y_s = _moe(xns, routes, cnts, hs, lw["w_gate"], lw["w_up"], lw["w_down"],
                    lw["norm_f_g"], tm)
    y_p = y_p.reshape(x_prompt.shape)
    y_s = y_s.reshape(x_sample.shape)
    return (y_p, y_s, conv_p[None], shift_p[None], wkv_p[None],
            conv_s[None], shift_s[None], wkv_s[None])
```

```python
import functools

import numpy as np
import jax
import jax.numpy as jnp
from jax import lax
from jax.experimental import pallas as pl
from jax.experimental.pallas import tpu as pltpu

F32 = jnp.float32
BF16 = jnp.bfloat16

D_MODEL = 1024
C_CONV = 512
C_RWKV = 512
HEAD_DIM = 64
N_HEADS = 8
CONV_WIDTH = 31
D_SHIFT = 1792
N_EXPERT_GROUPS = 4
EXPERTS_PER_GROUP = 8
N_EXPERTS = N_EXPERT_GROUPS * EXPERTS_PER_GROUP
D_EXPERT = 256
RMS_EPS = 1e-6
LN_EPS = 1e-5
GN_EPS = 64e-5

LANES = 128
SUBLANES = 8
CONV_PAD = 32
WKV_BATCH = 8
BH_PER_GROUP = WKV_BATCH * N_HEADS
K_HALF = HEAD_DIM // 2
WKV_PARTS = 2
LOAD_ROWS = LANES
ROUTE_COLS = LANES
EXPERT_COL0 = N_EXPERT_GROUPS
MOE_ROWS = 256
MOE_TOKEN_TILE = 1024
DMA_ISSUE_UNROLL = 8
VMEM_LIMIT = 56 * 1024 * 1024


def _key_perm():
    kp, kh, h = np.meshgrid(np.arange(K_HALF), np.arange(2), np.arange(N_HEADS), indexing="ij")
    return (h * HEAD_DIM + kh * K_HALF + kp).reshape(-1)


def _value_perm():
    v, h = np.meshgrid(np.arange(HEAD_DIM), np.arange(N_HEADS), indexing="ij")
    return (h * HEAD_DIM + v).reshape(-1)


PERM_KEY = _key_perm()
PERM_VAL = _value_perm()
PERM_Z = np.concatenate([PERM_KEY, C_RWKV + PERM_KEY, 2 * C_RWKV + PERM_VAL,
                         np.arange(3 * C_RWKV, D_SHIFT)])


def _reorder_last(x, split, order):
    lead = x.shape[:-1]
    k = len(lead)
    x = x.reshape(lead + split).transpose(tuple(range(k)) + tuple(k + o for o in order))
    return x.reshape(lead + (-1,))


def _key_order(x):
    return _reorder_last(x, (N_HEADS, 2, K_HALF), (2, 1, 0))


def _value_order(x):
    return _reorder_last(x, (N_HEADS, HEAD_DIM), (1, 0))


def _shift_order(x, inverse=False):
    if inverse:
        key = lambda a: _reorder_last(a, (K_HALF, 2, N_HEADS), (2, 1, 0))
        val = lambda a: _reorder_last(a, (HEAD_DIM, N_HEADS), (1, 0))
    else:
        key, val = _key_order, _value_order
    return jnp.concatenate([key(x[..., :C_RWKV]), key(x[..., C_RWKV:2 * C_RWKV]),
                            val(x[..., 2 * C_RWKV:3 * C_RWKV]), x[..., 3 * C_RWKV:]], axis=-1)


def _cparams(*sem):
    return pltpu.CompilerParams(dimension_semantics=sem, vmem_limit_bytes=VMEM_LIMIT)


def _split_bf16(x):
    hi = x.astype(BF16)
    lo = (x - hi.astype(F32)).astype(BF16)
    return hi, lo


def _segsum(x, same_head):
    hi, lo = _split_bf16(x)
    s = (jnp.dot(hi, same_head, preferred_element_type=F32)
         + jnp.dot(lo, same_head, preferred_element_type=F32))
    return jnp.concatenate([s] * (x.shape[1] // LANES), axis=1)


def _inproj_kernel(x_ref, g_ref, w_ref, h_ref, z_ref):
    x = x_ref[...]
    xn = x * lax.rsqrt(jnp.mean(x * x, axis=-1, keepdims=True) + RMS_EPS) * g_ref[...]
    p = jnp.dot(xn.astype(BF16), w_ref[...], preferred_element_type=F32)
    h_ref[...] = p[:, :C_CONV] * jax.nn.sigmoid(p[:, C_CONV:2 * C_CONV])
    z_ref[...] = p[:, 2 * C_CONV:]


def _inproj(x, g, w_bf, tm):
    n = x.shape[0]
    d_in = w_bf.shape[1]
    return pl.pallas_call(
        _inproj_kernel,
        grid=(n // tm,),
        in_specs=[pl.BlockSpec((tm, D_MODEL), lambda i: (i, 0)),
                  pl.BlockSpec((1, D_MODEL), lambda i: (0, 0)),
                  pl.BlockSpec((D_MODEL, d_in), lambda i: (0, 0))],
        out_specs=[pl.BlockSpec((tm, C_CONV), lambda i: (i, 0)),
                   pl.BlockSpec((tm, D_SHIFT), lambda i: (i, 0))],
        out_shape=[jax.ShapeDtypeStruct((n, C_CONV), F32),
                   jax.ShapeDtypeStruct((n, D_SHIFT), F32)],
        compiler_params=_cparams("parallel"),
        name="inproj",
    )(x, g, w_bf)


def _conv_kernel(h_ref, hist_ref, w_ref, b_ref, lg_ref, lb_ref, o_ref, win_ref, sh_ref,
                 *, rows, sub_rows, n_chunks, n_seq, seq_unroll):
    shifted_rows = rows + CONV_PAD - SUBLANES

    def chunk(s, u, c):
        win, sh = win_ref.at[u], sh_ref.at[u]
        t0 = pl.multiple_of(c * rows, SUBLANES)
        if n_chunks == 1:
            win[pl.ds(0, CONV_PAD), :] = hist_ref[s]
        else:
            prev0 = pl.multiple_of(jnp.maximum(t0 - CONV_PAD, 0), SUBLANES)
            win[pl.ds(0, CONV_PAD), :] = jnp.where(
                c == 0, hist_ref[s], h_ref[s, pl.ds(prev0, CONV_PAD), :])
        win[pl.ds(CONV_PAD, rows), :] = h_ref[s, pl.ds(t0, rows), :]
        sh[0] = win[...]
        for sft in range(1, SUBLANES):
            sh[sft, pl.ds(0, shifted_rows), :] = win[pl.ds(sft, shifted_rows), :]
        for blk in range(rows // sub_rows):
            r0 = blk * sub_rows
            acc = jnp.zeros((sub_rows, C_CONV), F32) + b_ref[...]
            for j in range(CONV_WIDTH):
                off = j + 2
                acc = acc + (sh[off % SUBLANES, pl.ds(off - off % SUBLANES + r0, sub_rows), :]
                             * w_ref[pl.ds(j, 1), :])
            mu = jnp.mean(acc, axis=-1, keepdims=True)
            d = acc - mu
            var = jnp.mean(d * d, axis=-1, keepdims=True)
            of = d * lax.rsqrt(var + LN_EPS) * lg_ref[...] + lb_ref[...]
            o_ref[s, pl.ds(t0 + r0, sub_rows), :] = (of * jax.nn.sigmoid(of)).astype(o_ref.dtype)

    def seq_body(i, _):
        def chunk_body(c, _):
            for u in range(seq_unroll):
                chunk(i * seq_unroll + u, u, c)
            return 0
        lax.fori_loop(0, n_chunks, chunk_body, 0)
        return 0
    lax.fori_loop(0, n_seq // seq_unroll, seq_body, 0)


def _conv(h, hist, conv_w, conv_b, ln_g, ln_b, n_seq, rows):
    b, t, _ = h.shape
    sub_rows = min(rows, 32)
    seq_unroll = max(1, min(n_seq, 32 // t))
    kern = functools.partial(_conv_kernel, rows=rows, sub_rows=sub_rows, n_chunks=t // rows,
                             n_seq=n_seq, seq_unroll=seq_unroll)
    vec = pl.BlockSpec((1, C_CONV), lambda i: (0, 0))
    return pl.pallas_call(
        kern,
        grid=(b // n_seq,),
        in_specs=[pl.BlockSpec((n_seq, t, C_CONV), lambda i: (i, 0, 0)),
                  pl.BlockSpec((n_seq, CONV_PAD, C_CONV), lambda i: (i, 0, 0)),
                  pl.BlockSpec((CONV_WIDTH, C_CONV), lambda i: (0, 0)),
                  vec, vec, vec],
        out_specs=pl.BlockSpec((n_seq, t, C_CONV), lambda i: (i, 0, 0)),
        out_shape=jax.ShapeDtypeStruct((b, t, C_CONV), BF16),
        scratch_shapes=[pltpu.VMEM((seq_unroll, rows + CONV_PAD, C_CONV), F32),
                        pltpu.VMEM((seq_unroll, SUBLANES, rows + CONV_PAD, C_CONV), F32)],
        compiler_params=_cparams("parallel"),
        name="conv",
    )(h, hist, conv_w, conv_b, ln_g, ln_b)


def _prep_kernel(z_ref, bnd_ref, halo_ref, mu_ref, w0_ref, wd_ref, a0_ref, wa_ref, wg_ref, kk_ref,
                 ka_ref, rk_ref, ones_ref,
                 r_ref, w_ref, k_ref, v_ref, nkk_ref, b_ref, bonus_ref, g_ref,
                 *, period, tiles_per_seq):
    z = z_ref[...]
    row_id = lax.broadcasted_iota(jnp.int32, z.shape, 0)
    bnd = bnd_ref[...]
    if tiles_per_seq > 1:
        first_tile = pl.program_id(0) % tiles_per_seq == 0
        bnd = jnp.where(first_tile, bnd, halo_ref[pl.ds(SUBLANES - 1, 1), :])
    z_prev = jnp.where(row_id % period == 0, bnd, pltpu.roll(z, 1, axis=0))
    zs = z + (z_prev - z) * mu_ref[...]
    r = zs[:, :C_RWKV]
    k = zs[:, C_RWKV:2 * C_RWKV]
    v = zs[:, 2 * C_RWKV:3 * C_RWKV]
    wa = zs[:, 3 * C_RWKV:3 * C_RWKV + LANES]
    gl = zs[:, 3 * C_RWKV + LANES:]
    ones_bd = ones_ref[...]

    lw = w0_ref[...] + jnp.dot(jnp.tanh(wa).astype(BF16), wd_ref[...], preferred_element_type=F32)
    w_log = -(jnp.maximum(-lw, 0.0) + jnp.log(1.0 + jnp.exp(-jnp.abs(lw)))) - 0.5
    decay = jnp.exp(-jnp.exp(w_log))
    a = jax.nn.sigmoid(a0_ref[...] + jnp.dot(wa.astype(BF16), wa_ref[...],
                                             preferred_element_type=F32))
    g = jnp.dot(jax.nn.sigmoid(gl).astype(BF16), wg_ref[...], preferred_element_type=F32)

    kk = k * kk_ref[...]
    norm = jnp.maximum(jnp.sqrt(_segsum(kk * kk, ones_bd)), 1e-12)
    kk = kk / norm
    k_mod = k * (1.0 + (a - 1.0) * ka_ref[...])
    bonus = _segsum(r * k_mod * rk_ref[...], ones_bd) * v

    def store_cols(ref, x):
        xt = x.T
        ntok = ref.shape[2]
        for q in range(ref.shape[0]):
            ref[q] = xt[:, q * ntok:(q + 1) * ntok]

    store_cols(r_ref, r)
    store_cols(w_ref, decay)
    store_cols(k_ref, k_mod)
    store_cols(v_ref, v)
    store_cols(nkk_ref, -kk)
    store_cols(b_ref, kk * a)
    bonus_ref[...] = bonus
    g_ref[...] = g


def _col_major_spec(rows, length):
    if rows <= length:
        per_b = length // rows
        return pl.BlockSpec((1, C_RWKV, rows), lambda i: (i // per_b, 0, i % per_b))
    return pl.BlockSpec((rows // length, C_RWKV, length), lambda i: (i, 0, 0))


def _prep(z2d, bnd, params, rows, period, tiles_per_seq):
    n = z2d.shape[0]
    length = n // WKV_BATCH
    (mu, w0, wd_pad, a0, wa_pad, wg, k_k, k_a, r_k, ones_bd) = params
    kern = functools.partial(_prep_kernel, period=period, tiles_per_seq=tiles_per_seq)
    halo_blocks = rows // SUBLANES
    vec = lambda c: pl.BlockSpec((1, c), lambda i: (0, 0))
    mat = lambda a: pl.BlockSpec(a.shape, lambda i: (0, 0))
    row_spec = pl.BlockSpec((rows, C_RWKV), lambda i: (i, 0))
    row_sd = jax.ShapeDtypeStruct((n, C_RWKV), F32)
    col_spec = _col_major_spec(rows, length)
    col_sd = jax.ShapeDtypeStruct((WKV_BATCH, C_RWKV, length), F32)
    return pl.pallas_call(
        kern,
        grid=(n // rows,),
        in_specs=[pl.BlockSpec((rows, D_SHIFT), lambda i: (i, 0)),
                  pl.BlockSpec((None, bnd.shape[1], D_SHIFT), lambda i: (i // tiles_per_seq, 0, 0)),
                  pl.BlockSpec((SUBLANES, D_SHIFT),
                               lambda i: (jnp.maximum(i * halo_blocks - 1, 0), 0)),
                  vec(D_SHIFT), vec(C_RWKV), mat(wd_pad), vec(C_RWKV), mat(wa_pad), mat(wg),
                  vec(C_RWKV), vec(C_RWKV), vec(C_RWKV), mat(ones_bd)],
        out_specs=[col_spec] * 6 + [row_spec] * 2,
        out_shape=[col_sd] * 6 + [row_sd] * 2,
        compiler_params=_cparams("parallel"),
        name="rwkv_prep",
    )(z2d, bnd, z2d, mu, w0, wd_pad, a0, wa_pad, wg, k_k, k_a, r_k, ones_bd)


def _fold_halves(x):
    return x + pltpu.roll(x, BH_PER_GROUP, axis=x.ndim - 1)


def _key_cols_to_lanes(src_ref, dst_ref):
    chunk = 2 * N_HEADS
    for kp in range(K_HALF):
        pieces = [src_ref[b8, pl.ds(kp * chunk + half * N_HEADS, N_HEADS), :]
                  for half in range(2) for b8 in range(WKV_BATCH)]
        dst_ref[kp, :LOAD_ROWS, :] = jnp.concatenate(pieces, axis=0).T


def _value_cols_to_lanes(src_ref, dst_ref):
    for v in range(HEAD_DIM):
        pieces = [src_ref[b8, pl.ds(v * N_HEADS, N_HEADS), :]
                  for _ in range(2) for b8 in range(WKV_BATCH)]
        dst_ref[pl.ds(v, LOAD_ROWS, stride=HEAD_DIM), :] = jnp.concatenate(pieces, axis=0).T


def _lanes_to_value_cols(y_s, yt_ref):
    for v in range(HEAD_DIM):
        yt = y_s[pl.ds(v, LOAD_ROWS, stride=HEAD_DIM), :].T
        folded = yt[:BH_PER_GROUP] + yt[BH_PER_GROUP:]
        for b8 in range(WKV_BATCH):
            yt_ref[b8, pl.ds(v * N_HEADS, N_HEADS), :] = folded[b8 * N_HEADS:(b8 + 1) * N_HEADS]


def _wkv_kernel(w_ref, nkk_ref, b_ref, k_ref, r_ref, v_ref, s0_ref, y_ref, sout_ref,
                w_s, nkk_s, b_s, k_s, r_s, v_s, y_s, s_ref,
                *, t_blk, steps_per_load, carry_state, n_parts):
    j = pl.program_id(0)
    sub = j % steps_per_load
    part = HEAD_DIM // n_parts
    parts = [slice(i * part, (i + 1) * part) for i in range(n_parts)]

    @pl.when(sub == 0)
    def _():
        for src, dst in ((w_ref, w_s), (nkk_ref, nkk_s), (b_ref, b_s), (k_ref, k_s), (r_ref, r_s)):
            _key_cols_to_lanes(src, dst)
        _value_cols_to_lanes(v_ref, v_s)

    if carry_state:
        @pl.when(j == 0)
        def _():
            s_ref[...] = s0_ref[0]
    else:
        s_ref[...] = s0_ref[0]

    tok0 = sub * t_blk
    acc = jnp.zeros((HEAD_DIM, LANES), F32)
    for kp in range(K_HALF):
        acc = acc + s_ref[kp] * nkk_s[kp, pl.ds(tok0, 1), :]
    sa_init = tuple(_fold_halves(acc[p, :]) for p in parts[:-1]) + (acc[parts[-1], :],)

    def part_step(tok, nxt_tok, rows, sa):
        cur = pl.ds(tok, 1)
        vrows = pl.ds(pl.multiple_of(tok * HEAD_DIM, HEAD_DIM) + rows.start, part)
        vv = v_s[vrows, :]
        yacc = jnp.zeros((part, LANES), F32)
        sacc = jnp.zeros((part, LANES), F32)
        for kp in range(K_HALF):
            s = (s_ref[kp, rows, :] * w_s[kp, cur, :] + sa * b_s[kp, cur, :]
                 + vv * k_s[kp, cur, :])
            s_ref[kp, rows, :] = s
            yacc = yacc + s * r_s[kp, cur, :]
            sacc = sacc + s * nkk_s[kp, pl.ds(nxt_tok, 1), :]
        y_s[vrows, :] = yacc
        return sacc

    def step(t, sa_parts):
        tok = tok0 + t
        nxt_tok = tok0 + jnp.minimum(t + 1, t_blk - 1)
        sa_last = _fold_halves(sa_parts[-1])
        nxt = [_fold_halves(part_step(tok, nxt_tok, rows, sa))
               for rows, sa in zip(parts[:-1], sa_parts[:-1])]
        return tuple(nxt) + (part_step(tok, nxt_tok, parts[-1], sa_last),)

    lax.fori_loop(0, t_blk, step, sa_init)

    if carry_state:
        @pl.when(j == pl.num_programs(0) - 1)
        def _():
            sout_ref[0] = s_ref[...]
    else:
        sout_ref[0] = s_ref[...]

    @pl.when(sub == steps_per_load - 1)
    def _():
        _lanes_to_value_cols(y_s, y_ref)


def _wkv(w, nkk, b, k, r, v, s0, *, t_blk, carry_state):
    length = w.shape[2]
    steps_per_load = LOAD_ROWS // t_blk
    n_steps = length // t_blk
    tok_spec = pl.BlockSpec((WKV_BATCH, C_RWKV, LOAD_ROWS), lambda j: (0, 0, j // steps_per_load))
    state_blk = (1, K_HALF, HEAD_DIM, LANES)
    if carry_state:
        sspec = pl.BlockSpec(state_blk, lambda j: (0, 0, 0, 0))
    else:
        sspec = pl.BlockSpec(state_blk, lambda j: (j, 0, 0, 0))
    key_rows = pltpu.VMEM((K_HALF, LOAD_ROWS + SUBLANES, LANES), F32)
    val_rows = pltpu.VMEM((LOAD_ROWS * HEAD_DIM, LANES), F32)
    return pl.pallas_call(
        functools.partial(_wkv_kernel, t_blk=t_blk, steps_per_load=steps_per_load,
                          carry_state=carry_state, n_parts=WKV_PARTS),
        grid=(n_steps,),
        in_specs=[tok_spec] * 6 + [sspec],
        out_specs=[tok_spec, sspec],
        out_shape=[jax.ShapeDtypeStruct(w.shape, F32), jax.ShapeDtypeStruct(s0.shape, F32)],
        scratch_shapes=[key_rows] * 5 + [val_rows, val_rows,
                                         pltpu.VMEM((K_HALF, HEAD_DIM, LANES), F32)],
        compiler_params=_cparams("arbitrary"),
        name="wkv",
    )(w, nkk, b, k, r, v, s0)


def _state_to_lanes(s, n_grp):
    s = s.reshape(WKV_BATCH, n_grp, N_HEADS, HEAD_DIM, 2, K_HALF)
    return s.transpose(1, 5, 3, 4, 0, 2).reshape(n_grp, K_HALF, HEAD_DIM, LANES)


def _state_from_lanes(s, n_grp):
    s = s.reshape(n_grp, K_HALF, HEAD_DIM, 2, WKV_BATCH, N_HEADS)
    return s.transpose(4, 0, 5, 2, 3, 1).reshape(n_grp * WKV_BATCH, N_HEADS, HEAD_DIM, HEAD_DIM)


def _outproj_kernel(x_ref, c_ref, y_ref, bonus_ref, g_ref, gng_ref, gnb_ref, ones_ref,
                    wo_c_ref, wo_r_ref, n2_ref, wr_both_ref, br_ref,
                    h_ref, xn_ref, route_ref, cnt_ref):
    ones_bd = ones_ref[...]
    y = jnp.concatenate([y_ref[q].T for q in range(y_ref.shape[0])], axis=0)
    inv_n = 1.0 / HEAD_DIM
    mu = _segsum(y, ones_bd) * inv_n
    d = y - mu
    var = _segsum(d * d, ones_bd) * inv_n
    yn = d * lax.rsqrt(var + GN_EPS) * gng_ref[...] + gnb_ref[...]
    rw = (yn + bonus_ref[...]) * g_ref[...]
    h = (x_ref[...]
         + jnp.dot(c_ref[...], wo_c_ref[...], preferred_element_type=F32)
         + jnp.dot(rw.astype(BF16), wo_r_ref[...], preferred_element_type=F32))
    h_ref[...] = h
    xn = h * lax.rsqrt(jnp.mean(h * h, axis=-1, keepdims=True) + RMS_EPS) * n2_ref[...]
    xn_ref[...] = xn

    x_hi, x_lo = _split_bf16(xn)
    hi_both = jnp.dot(x_hi, wr_both_ref[...], preferred_element_type=F32)
    logits = (hi_both[:, :ROUTE_COLS]
              + jnp.dot(x_lo, wr_both_ref[:, :ROUTE_COLS], preferred_element_type=F32)
              + hi_both[:, ROUTE_COLS:]) + br_ref[...]

    col = lax.broadcasted_iota(jnp.int32, logits.shape, 1)
    neg = jnp.float32(-jnp.inf)
    big = jnp.int32(ROUTE_COLS)
    is_g = col < N_EXPERT_GROUPS
    g_logit = jnp.where(is_g, logits, neg)
    g_max = jnp.max(g_logit, axis=-1, keepdims=True)
    g_idx = jnp.min(jnp.where(g_logit == g_max, col, big), axis=-1, keepdims=True)
    p_group = 1.0 / jnp.sum(jnp.where(is_g, jnp.exp(logits - g_max), 0.0), axis=-1, keepdims=True)
    lo_col = EXPERT_COL0 + g_idx * EXPERTS_PER_GROUP
    in_grp = (col >= lo_col) & (col < lo_col + EXPERTS_PER_GROUP)
    e1 = jnp.where(in_grp, logits, neg)
    m1 = jnp.max(e1, axis=-1, keepdims=True)
    i1 = jnp.min(jnp.where(e1 == m1, col, big), axis=-1, keepdims=True)
    e2 = jnp.where(col == i1, neg, e1)
    m2 = jnp.max(e2, axis=-1, keepdims=True)
    i2 = jnp.min(jnp.where(e2 == m2, col, big), axis=-1, keepdims=True)
    ex = jnp.exp(m2 - m1)
    w1 = p_group / (1.0 + ex)
    w2 = p_group * ex / (1.0 + ex)

    exp1 = i1 - EXPERT_COL0
    exp2 = i2 - EXPERT_COL0
    oh1 = jnp.where(col == exp1, 1.0, 0.0)
    oh2 = jnp.where(col == exp2, 1.0, 0.0)
    tm = logits.shape[0]
    earlier = jnp.where(lax.broadcasted_iota(jnp.int32, (tm, tm), 0)
                        > lax.broadcasted_iota(jnp.int32, (tm, tm), 1), 1.0, 0.0).astype(BF16)
    before1 = jnp.dot(earlier, oh1.astype(BF16), preferred_element_type=F32)
    before2 = jnp.dot(earlier, oh2.astype(BF16), preferred_element_type=F32)
    cnt1 = jnp.sum(oh1, axis=0, keepdims=True)
    cnt2 = jnp.sum(oh2, axis=0, keepdims=True)
    rank1 = jnp.sum(before1 * oh1, axis=-1, keepdims=True)
    rank2 = jnp.sum((before2 + cnt1) * oh2, axis=-1, keepdims=True)
    fields = (exp1.astype(F32), exp2.astype(F32), w1, w2, rank1, rank2)
    route = jnp.zeros_like(logits)
    for c, val in enumerate(fields):
        route = jnp.where(col == c, val, route)
    route_ref[...] = route
    cnt_ref[...] = jnp.broadcast_to(cnt1 + cnt2, cnt_ref.shape)


def _outproj(x, conv_out, y, bonus, g, params, tm):
    n = x.shape[0]
    (gn_g, gn_b, ones_bd, wo_c, wo_r, n2_g, wr_both, br) = params
    row = lambda c: pl.BlockSpec((tm, c), lambda i: (i, 0))
    vec = lambda c: pl.BlockSpec((1, c), lambda i: (0, 0))
    mat = lambda a: pl.BlockSpec(a.shape, lambda i: (0, 0))
    return pl.pallas_call(
        _outproj_kernel,
        grid=(n // tm,),
        in_specs=[row(D_MODEL), row(C_CONV), _col_major_spec(tm, n // WKV_BATCH), row(C_RWKV),
                  row(C_RWKV),
                  vec(C_RWKV), vec(C_RWKV), mat(ones_bd), mat(wo_c), mat(wo_r), vec(D_MODEL),
                  mat(wr_both), vec(ROUTE_COLS)],
        out_specs=[row(D_MODEL), row(D_MODEL), row(ROUTE_COLS),
                   pl.BlockSpec((None, SUBLANES, ROUTE_COLS), lambda i: (i, 0, 0))],
        out_shape=[jax.ShapeDtypeStruct((n, D_MODEL), F32),
                   jax.ShapeDtypeStruct((n, D_MODEL), F32),
                   jax.ShapeDtypeStruct((n, ROUTE_COLS), F32),
                   jax.ShapeDtypeStruct((n // tm, SUBLANES, ROUTE_COLS), F32)],
        compiler_params=_cparams("parallel"),
        name="outproj_route",
    )(x, conv_out, y, bonus, g, gn_g, gn_b, ones_bd, wo_c, wo_r, n2_g, wr_both, br)


def _moe_plan(route, cnt, tm, moe_rows):
    n = route.shape[0]
    ids = route[:, 0:2].astype(jnp.int32)
    ranks = route[:, 4:6].astype(jnp.int32)
    tile_cnt = cnt[:, 0, :N_EXPERTS].astype(jnp.int32)
    before_tile = jnp.cumsum(tile_cnt, axis=0) - tile_cnt
    total = jnp.sum(tile_cnt, axis=0)
    tiles_e = (total + moe_rows - 1) // moe_rows
    tile_end = jnp.cumsum(tiles_e)
    row_start = (tile_end - tiles_e) * moe_rows
    first_row = (row_start[None, :] + before_tile)[:, None, None, :]
    one_hot = ids.reshape(-1, tm, 2, 1) == jnp.arange(N_EXPERTS, dtype=jnp.int32)
    pos = jnp.sum(jnp.where(one_hot, first_row, 0), axis=-1).reshape(n, 2) + ranks
    n_row_tiles = 2 * n // moe_rows + N_EXPERTS
    t = jnp.arange(n_row_tiles, dtype=jnp.int32)
    tile_expert = jnp.minimum(jnp.sum((t[:, None] >= tile_end[None, :]).astype(jnp.int32), axis=1),
                              N_EXPERTS - 1)
    tile_used = (t < tile_end[-1]).astype(jnp.int32)
    last_tile = jnp.maximum(tile_end[-1:] - 1, 0).astype(jnp.int32)
    is_last = jnp.any((t[:, None] == tile_end[None, :] - 1) & (tiles_e[None, :] > 0), axis=1)
    tile_fill = (is_last | (tile_used == 0)).astype(jnp.int32)
    return pos.reshape(-1), tile_fill, tile_expert.astype(jnp.int32), tile_used, last_tile


def _row_copy_wait(src_ref, dst_ref, sem, rows):
    pltpu.make_async_copy(src_ref.at[pl.ds(0, rows), :], dst_ref.at[pl.ds(0, rows), :], sem).wait()


def _dispatch_kernel(pos_ref, fill_ref, *refs, tm, group_tiles):
    x_refs, (xs_ref, zero_ref, sem) = refs[:len(group_tiles)], refs[len(group_tiles):]
    base = pl.program_id(0) * (2 * tm)
    fill_rows = zero_ref.shape[0]

    @pl.when(pl.program_id(0) == 0)
    def _():
        zero_ref[...] = jnp.zeros_like(zero_ref)
        n_row_tiles = xs_ref.shape[0] // fill_rows

        def fill(t, _):
            @pl.when(fill_ref[t] != 0)
            def _():
                start = pl.multiple_of(t * fill_rows, fill_rows)
                pltpu.make_async_copy(zero_ref, xs_ref.at[pl.ds(start, fill_rows), :], sem).start()
            return 0
        lax.fori_loop(0, n_row_tiles, fill, 0)

        def fill_wait(t, _):
            @pl.when(fill_ref[t] != 0)
            def _():
                pltpu.make_async_copy(zero_ref, xs_ref.at[pl.ds(0, fill_rows), :], sem).wait()
            return 0
        lax.fori_loop(0, n_row_tiles, fill_wait, 0)

    for x_ref, in_group in zip(x_refs, _group_preds(pl.program_id(0), group_tiles)):
        @pl.when(in_group)
        def _(x_ref=x_ref):
            def issue(r, _):
                src = x_ref.at[pl.ds(r, 1), :]
                for c in range(2):
                    dst = xs_ref.at[pl.ds(pos_ref[base + 2 * r + c], 1), :]
                    pltpu.make_async_copy(src, dst, sem).start()
                return 0
            lax.fori_loop(0, tm, issue, 0, unroll=DMA_ISSUE_UNROLL)
    for _ in range(2):
        _row_copy_wait(x_refs[0], xs_ref, sem, tm)


def _group_preds(i, group_tiles):
    preds, start = [], 0
    for tiles in group_tiles:
        preds.append((i >= start) & (i < start + tiles))
        start += tiles
    return preds


def _group_specs(block, group_tiles):
    specs, start = [], 0
    for tiles in group_tiles:
        specs.append(pl.BlockSpec(
            block, lambda i, *_, start=start, tiles=tiles: (jnp.clip(i - start, 0, tiles - 1), 0)))
        start += tiles
    return specs


def _dispatch(pos, tile_fill, xns, n_rows, moe_rows, tm):
    group_tiles = tuple(x.shape[0] // tm for x in xns)
    return pl.pallas_call(
        functools.partial(_dispatch_kernel, tm=tm, group_tiles=group_tiles),
        grid_spec=pltpu.PrefetchScalarGridSpec(
            num_scalar_prefetch=2,
            grid=(sum(group_tiles),),
            in_specs=_group_specs((tm, D_MODEL), group_tiles),
            out_specs=pl.BlockSpec(memory_space=pl.ANY),
            scratch_shapes=[pltpu.VMEM((moe_rows, D_MODEL), F32), pltpu.SemaphoreType.DMA(())]),
        out_shape=jax.ShapeDtypeStruct((n_rows, D_MODEL), F32),
        compiler_params=_cparams("arbitrary"),
        name="moe_dispatch",
    )(pos, tile_fill, *xns)


def _experts_kernel(expert_ref, used_ref, last_ref, xs_ref, wg_ref, wu_ref, wd_ref, y_ref,
                    wg_bf, wu_bf, wd_bf):
    del last_ref
    t = pl.program_id(0)

    @pl.when((t == 0) | (expert_ref[t] != expert_ref[jnp.maximum(t - 1, 0)]))
    def _():
        wg_bf[...] = wg_ref[0].astype(BF16)
        wu_bf[...] = wu_ref[0].astype(BF16)
        wd_bf[...] = wd_ref[0].astype(BF16)

    @pl.when(used_ref[t] != 0)
    def _():
        x = xs_ref[...].astype(BF16)
        hg = jnp.dot(x, wg_bf[...], preferred_element_type=F32)
        hu = jnp.dot(x, wu_bf[...], preferred_element_type=F32)
        act = hg * jax.nn.sigmoid(hg) * hu
        y_ref[...] = jnp.dot(act.astype(BF16), wd_bf[...], preferred_element_type=F32)

    @pl.when(used_ref[t] == 0)
    def _():
        y_ref[...] = jnp.zeros_like(y_ref)


def _experts(tile_expert, tile_used, last_tile, xs, w_gate, w_up, w_down):
    n_rows = xs.shape[0]
    moe_rows = n_rows // tile_expert.shape[0]
    row_tile = lambda t, ex, used, last: (jnp.where(used[t] != 0, t, last[0]), 0)
    weight = lambda t, ex, used, last: (ex[t], 0, 0)
    return pl.pallas_call(
        _experts_kernel,
        grid_spec=pltpu.PrefetchScalarGridSpec(
            num_scalar_prefetch=3,
            grid=(n_rows // moe_rows,),
            in_specs=[pl.BlockSpec((moe_rows, D_MODEL), row_tile),
                      pl.BlockSpec((1, D_MODEL, D_EXPERT), weight),
                      pl.BlockSpec((1, D_MODEL, D_EXPERT), weight),
                      pl.BlockSpec((1, D_EXPERT, D_MODEL), weight)],
            out_specs=pl.BlockSpec((moe_rows, D_MODEL), lambda t, ex, used, last: (t, 0)),
            scratch_shapes=[pltpu.VMEM((D_MODEL, D_EXPERT), BF16), pltpu.VMEM((D_MODEL, D_EXPERT), BF16),
                            pltpu.VMEM((D_EXPERT, D_MODEL), BF16)]),
        out_shape=jax.ShapeDtypeStruct((n_rows, D_MODEL), F32),
        compiler_params=_cparams("arbitrary"),
        name="moe_experts",
    )(tile_expert, tile_used, last_tile, xs, w_gate, w_up, w_down)


def _combine_kernel(pos_ref, *refs, tm, group_tiles):
    n_grp = len(group_tiles)
    h_refs, (route_ref, nf_ref, ys_ref) = refs[:n_grp], refs[n_grp:n_grp + 3]
    o_refs, (y_buf, sem) = refs[n_grp + 3:2 * n_grp + 3], refs[2 * n_grp + 3:]
    i = pl.program_id(0)
    last = pl.num_programs(0) - 1
    in_group = _group_preds(i, group_tiles)

    def issue_tile(base, slot):
        def issue(r, _):
            for c in range(2):
                src = ys_ref.at[pl.ds(pos_ref[base + 2 * r + c], 1), :]
                pltpu.make_async_copy(src, y_buf.at[slot, c, pl.ds(r, 1), :], sem.at[slot]).start()
            return 0
        lax.fori_loop(0, tm, issue, 0, unroll=DMA_ISSUE_UNROLL)

    def wait(slot):
        for c in range(2):
            _row_copy_wait(ys_ref, y_buf.at[slot, c], sem.at[slot], tm)

    @pl.when(i == 0)
    def _():
        issue_tile(0, 0)

    next_base = jnp.minimum(i + 1, last) * (2 * tm)

    def run(slot):
        other = 1 - slot
        issue_tile(next_base, other)
        wait(slot)
        route = route_ref[...]
        col = lax.broadcasted_iota(jnp.int32, route.shape, 1)
        w1 = jnp.sum(jnp.where(col == 2, route, 0.0), axis=-1, keepdims=True)
        w2 = jnp.sum(jnp.where(col == 3, route, 0.0), axis=-1, keepdims=True)
        moe = w1 * y_buf[slot, 0] + w2 * y_buf[slot, 1]
        for h_ref, o_ref, mine in zip(h_refs, o_refs, in_group):
            @pl.when(mine)
            def _(h_ref=h_ref, o_ref=o_ref):
                h = h_ref[...] + moe
                o_ref[...] = (h * lax.rsqrt(jnp.mean(h * h, axis=-1, keepdims=True) + RMS_EPS)
                              * nf_ref[...])

        @pl.when(i == last)
        def _():
            wait(other)

    for slot in range(2):
        pl.when(i % 2 == slot)(functools.partial(run, slot))


def _combine(pos, hs, route, nf_g, ys, tm):
    group_tiles = tuple(h.shape[0] // tm for h in hs)
    row_specs = _group_specs((tm, D_MODEL), group_tiles)
    return pl.pallas_call(
        functools.partial(_combine_kernel, tm=tm, group_tiles=group_tiles),
        grid_spec=pltpu.PrefetchScalarGridSpec(
            num_scalar_prefetch=1,
            grid=(sum(group_tiles),),
            in_specs=row_specs + [pl.BlockSpec((tm, ROUTE_COLS), lambda i, pos: (i, 0)),
                                  pl.BlockSpec((1, D_MODEL), lambda i, pos: (0, 0)),
                                  pl.BlockSpec(memory_space=pl.ANY)],
            out_specs=row_specs,
            scratch_shapes=[pltpu.VMEM((2, 2, tm, D_MODEL), F32), pltpu.SemaphoreType.DMA((2,))]),
        out_shape=[jax.ShapeDtypeStruct(h.shape, F32) for h in hs],
        compiler_params=_cparams("arbitrary"),
        name="moe_combine",
    )(pos, *hs, route, nf_g, ys)


def _moe(xns, routes, cnts, hs, w_gate, w_up, w_down, nf_g, tm):
    route = jnp.concatenate(routes, axis=0)
    cnt = jnp.concatenate(cnts, axis=0)
    moe_rows = min(MOE_ROWS, max(LANES, 2 * route.shape[0] // N_EXPERTS))
    pos, tile_fill, tile_expert, tile_used, last_tile = _moe_plan(route, cnt, tm, moe_rows)
    xs = _dispatch(pos, tile_fill, xns, tile_expert.shape[0] * moe_rows, moe_rows, MOE_TOKEN_TILE)
    ys = _experts(tile_expert, tile_used, last_tile, xs, w_gate, w_up, w_down)
    return _combine(pos, hs, route, nf_g, ys, MOE_TOKEN_TILE)


def _layer(x, conv_buf, shift_buf, wkv_state, lw, *, tm, conv_seqs, conv_rows, prep_rows,
           wkv_tblk):
    b, t, _ = x.shape
    n = b * t
    n_grp = b // WKV_BATCH
    x2d = x.reshape(n, D_MODEL)

    h_glu, z = _inproj(x2d, lw["norm1_g"], lw["w_in"], tm)

    h3 = h_glu.reshape(b, t, C_CONV)
    hist = jnp.concatenate([jnp.zeros((b, CONV_PAD - CONV_WIDTH + 1, C_CONV), F32), conv_buf], axis=1)
    n_hist = CONV_WIDTH - 1
    new_conv = h3[:, t - n_hist:] if t >= n_hist else jnp.concatenate([conv_buf[:, t:], h3], axis=1)
    conv_out = _conv(h3, hist, lw["conv_w"], lw["conv_b"], lw["conv_ln_g"], lw["conv_ln_b"],
                     conv_seqs, conv_rows)

    new_shift = _shift_order(z.reshape(b, t, D_SHIFT)[:, -1], inverse=True)
    shift_buf = _shift_order(shift_buf)
    if prep_rows > t:
        period, tiles_per_seq = t, 1
        bnd = jnp.repeat(shift_buf, t, axis=0).reshape(n // prep_rows, prep_rows, D_SHIFT)
    else:
        period, tiles_per_seq = prep_rows, t // prep_rows
        bnd = shift_buf.reshape(b, 1, D_SHIFT)
    r, w, k, v, nkk, bb, bonus, g = _prep(z, bnd, lw["prep"], prep_rows, period, tiles_per_seq)

    y_cols, s_l = _wkv(w, nkk, bb, k, r, v, _state_to_lanes(wkv_state, n_grp),
                       t_blk=wkv_tblk, carry_state=(n_grp == 1))
    new_wkv = _state_from_lanes(s_l, n_grp)

    moe_in = _outproj(x2d, conv_out.reshape(n, C_CONV), y_cols, bonus, g, lw["outproj"], tm)
    return moe_in, new_conv, new_shift, new_wkv


def _prepare_weights(norm1_g, w_in, conv_w, conv_b, conv_ln_g, conv_ln_b, mu_shift, w0,
                     w_decay_up, a0, w_aaa_up, w_gate_up, k_k, k_a, r_k, gn_g, gn_b, w_out,
                     norm2_g, w_router_group, b_router_group, w_router_expert, b_router_expert,
                     w_exp_gate, w_exp_up, w_exp_down, norm_f_g):
    row = lambda a: a.reshape(1, -1)
    head = jnp.arange(C_RWKV) % N_HEADS
    ones_bd = (head[:, None] == head[None, :LANES]).astype(BF16)
    w_in_perm = jnp.concatenate([w_in[:, :2 * C_CONV], _shift_order(w_in[:, 2 * C_CONV:])], axis=1)
    zeros_lora = jnp.zeros((LANES - w_decay_up.shape[0], C_RWKV), F32)
    wd_pad = jnp.concatenate([_key_order(w_decay_up), zeros_lora], axis=0).astype(BF16)
    wa_pad = jnp.concatenate([zeros_lora, _key_order(w_aaa_up)], axis=0).astype(BF16)
    key_vecs = _key_order(jnp.stack([w0, a0, k_k, k_a, r_k]))
    val_vecs = _value_order(jnp.stack([gn_g, gn_b]))
    w_out_r = (w_out[C_CONV:].reshape(N_HEADS, HEAD_DIM, D_MODEL).transpose(1, 0, 2)
               .reshape(C_RWKV, D_MODEL))
    pad_cols = ROUTE_COLS - N_EXPERT_GROUPS - N_EXPERTS
    w_route = jnp.concatenate([w_router_group, w_router_expert,
                               jnp.zeros((D_MODEL, pad_cols), F32)], axis=1)
    wr_both = jnp.concatenate(_split_bf16(w_route), axis=1)
    b_route = jnp.concatenate([b_router_group, b_router_expert, jnp.zeros((pad_cols,), F32)])
    return {
        "norm1_g": row(norm1_g), "w_in": w_in_perm.astype(BF16),
        "conv_w": conv_w, "conv_b": row(conv_b), "conv_ln_g": row(conv_ln_g),
        "conv_ln_b": row(conv_ln_b),
        "prep": (row(_shift_order(mu_shift)), key_vecs[0:1], wd_pad, key_vecs[1:2], wa_pad,
                 _value_order(w_gate_up).astype(BF16), key_vecs[2:3], key_vecs[3:4], key_vecs[4:5],
                 ones_bd),
        "outproj": (val_vecs[0:1], val_vecs[1:2], ones_bd, w_out[:C_CONV].astype(BF16),
                    w_out_r.astype(BF16), row(norm2_g), wr_both, row(b_route)),
        "w_gate": w_exp_gate.reshape(N_EXPERTS, D_MODEL, D_EXPERT),
        "w_up": w_exp_up.reshape(N_EXPERTS, D_MODEL, D_EXPERT),
        "w_down": w_exp_down.reshape(N_EXPERTS, D_EXPERT, D_MODEL),
        "norm_f_g": row(norm_f_g),
    }


def kernel(x_prompt, x_sample, state_conv, state_shift, state_wkv, norm1_g, w_in, conv_w, conv_b, conv_ln_g, conv_ln_b, mu_shift, w0, w_decay_up, a0, w_aaa_up, w_gate_up, k_k, k_a, r_k, gn_g, gn_b, w_out, norm2_g, w_router_group, b_router_group, w_router_expert, b_router_expert, w_exp_gate, w_exp_up, w_exp_down, norm_f_g):
    depth = norm1_g.shape[0]
    assert depth == 1, "single-layer trunk"
    b = x_prompt.shape[0]
    assert b == WKV_BATCH and x_prompt.shape[1] % LOAD_ROWS == 0
    assert x_sample.shape[0] * x_sample.shape[1] == WKV_BATCH * LOAD_ROWS
    lw = _prepare_weights(norm1_g[0], w_in[0], conv_w[0], conv_b[0], conv_ln_g[0], conv_ln_b[0],
                          mu_shift[0], w0[0], w_decay_up[0], a0[0], w_aaa_up[0], w_gate_up[0],
                          k_k[0], k_a[0], r_k[0], gn_g[0], gn_b[0], w_out[0], norm2_g[0],
                          w_router_group[0], b_router_group[0], w_router_expert[0],
                          b_router_expert[0], w_exp_gate[0], w_exp_up[0], w_exp_down[0], norm_f_g)
    dt = x_prompt.dtype
    tm = 512
    moe_p, conv_p, shift_p, wkv_p = _layer(
        x_prompt,
        jnp.zeros((b, CONV_WIDTH - 1, C_CONV), dt),
        jnp.zeros((b, D_SHIFT), dt),
        jnp.zeros((b, N_HEADS, HEAD_DIM, HEAD_DIM), dt), lw,
        tm=tm, conv_seqs=1, conv_rows=128, prep_rows=256, wkv_tblk=LOAD_ROWS)
    moe_s, conv_s, shift_s, wkv_s = _layer(
        x_sample, state_conv[0], state_shift[0], state_wkv[0], lw,
        tm=tm, conv_seqs=16, conv_rows=8, prep_rows=256, wkv_tblk=x_sample.shape[1])
    hs, xns, routes, cnts = zip(moe_p, moe_s)
    y_p, y_s = _moe(xns, routes, cnts, hs, lw["w_gate"], lw["w_up"], lw["w_down"],
                    lw["norm_f_g"], tm)
    y_p = y_p.reshape(x_prompt.shape)
    y_s = y_s.reshape(x_sample.shape)
    return (y_p, y_s, conv_p[None], shift_p[None], wkv_p[None],
            conv_s[None], shift_s[None], wkv_s[None])
```

```python
import functools

import numpy as np
import jax
import jax.numpy as jnp
from jax import lax
from jax.experimental import pallas as pl
from jax.experimental.pallas import tpu as pltpu

F32 = jnp.float32
BF16 = jnp.bfloat16

D_MODEL = 1024
C_CONV = 512
C_RWKV = 512
HEAD_DIM = 64
N_HEADS = 8
CONV_WIDTH = 31
D_SHIFT = 1792
N_EXPERT_GROUPS = 4
EXPERTS_PER_GROUP = 8
N_EXPERTS = N_EXPERT_GROUPS * EXPERTS_PER_GROUP
D_EXPERT = 256
RMS_EPS = 1e-6
LN_EPS = 1e-5
GN_EPS = 64e-5

LANES = 128
SUBLANES = 8
CONV_PAD = 32
WKV_BATCH = 8
BH_PER_GROUP = WKV_BATCH * N_HEADS
K_HALF = HEAD_DIM // 2
WKV_PARTS = 2
LOAD_ROWS = LANES
ROUTE_COLS = LANES
EXPERT_COL0 = N_EXPERT_GROUPS
MOE_ROWS = 512
DISPATCH_TILE = 1024
DMA_ISSUE_UNROLL = 8
VMEM_LIMIT = 56 * 1024 * 1024


def _key_perm():
    kp, kh, h = np.meshgrid(np.arange(K_HALF), np.arange(2), np.arange(N_HEADS), indexing="ij")
    return (h * HEAD_DIM + kh * K_HALF + kp).reshape(-1)


def _value_perm():
    v, h = np.meshgrid(np.arange(HEAD_DIM), np.arange(N_HEADS), indexing="ij")
    return (h * HEAD_DIM + v).reshape(-1)


PERM_KEY = _key_perm()
PERM_VAL = _value_perm()
PERM_Z = np.concatenate([PERM_KEY, C_RWKV + PERM_KEY, 2 * C_RWKV + PERM_VAL,
                         np.arange(3 * C_RWKV, D_SHIFT)])


def _reorder_last(x, split, order):
    lead = x.shape[:-1]
    k = len(lead)
    x = x.reshape(lead + split).transpose(tuple(range(k)) + tuple(k + o for o in order))
    return x.reshape(lead + (-1,))


def _key_order(x):
    return _reorder_last(x, (N_HEADS, 2, K_HALF), (2, 1, 0))


def _value_order(x):
    return _reorder_last(x, (N_HEADS, HEAD_DIM), (1, 0))


def _shift_order(x, inverse=False):
    if inverse:
        key = lambda a: _reorder_last(a, (K_HALF, 2, N_HEADS), (2, 1, 0))
        val = lambda a: _reorder_last(a, (HEAD_DIM, N_HEADS), (1, 0))
    else:
        key, val = _key_order, _value_order
    return jnp.concatenate([key(x[..., :C_RWKV]), key(x[..., C_RWKV:2 * C_RWKV]),
                            val(x[..., 2 * C_RWKV:3 * C_RWKV]), x[..., 3 * C_RWKV:]], axis=-1)


def _cparams(*sem):
    return pltpu.CompilerParams(dimension_semantics=sem, vmem_limit_bytes=VMEM_LIMIT)


def _split_bf16(x):
    hi = x.astype(BF16)
    lo = (x - hi.astype(F32)).astype(BF16)
    return hi, lo


def _segsum(x, same_head):
    hi, lo = _split_bf16(x)
    s = (jnp.dot(hi, same_head, preferred_element_type=F32)
         + jnp.dot(lo, same_head, preferred_element_type=F32))
    return jnp.concatenate([s] * (x.shape[1] // LANES), axis=1)


def _conv_kernel(h_ref, hist_ref, w_ref, b_ref, lg_ref, lb_ref, o_ref, win_ref, sh_ref,
                 *, rows, sub_rows, n_chunks, n_seq, seq_unroll):
    shifted_rows = rows + CONV_PAD - SUBLANES

    def chunk(s, u, c):
        win, sh = win_ref.at[u], sh_ref.at[u]
        t0 = pl.multiple_of(c * rows, SUBLANES)
        if n_chunks == 1:
            win[pl.ds(0, CONV_PAD), :] = hist_ref[s]
        else:
            prev0 = pl.multiple_of(jnp.maximum(t0 - CONV_PAD, 0), SUBLANES)
            win[pl.ds(0, CONV_PAD), :] = jnp.where(
                c == 0, hist_ref[s], h_ref[s, pl.ds(prev0, CONV_PAD), :])
        win[pl.ds(CONV_PAD, rows), :] = h_ref[s, pl.ds(t0, rows), :]
        sh[0] = win[...]
        for sft in range(1, SUBLANES):
            sh[sft, pl.ds(0, shifted_rows), :] = win[pl.ds(sft, shifted_rows), :]
        for blk in range(rows // sub_rows):
            r0 = blk * sub_rows
            acc = jnp.zeros((sub_rows, C_CONV), F32) + b_ref[...]
            for j in range(CONV_WIDTH):
                off = j + 2
                acc = acc + (sh[off % SUBLANES, pl.ds(off - off % SUBLANES + r0, sub_rows), :]
                             * w_ref[pl.ds(j, 1), :])
            mu = jnp.mean(acc, axis=-1, keepdims=True)
            d = acc - mu
            var = jnp.mean(d * d, axis=-1, keepdims=True)
            of = d * lax.rsqrt(var + LN_EPS) * lg_ref[...] + lb_ref[...]
            o_ref[s, pl.ds(t0 + r0, sub_rows), :] = (of * jax.nn.sigmoid(of)).astype(o_ref.dtype)

    def seq_body(i, _):
        def chunk_body(c, _):
            for u in range(seq_unroll):
                chunk(i * seq_unroll + u, u, c)
            return 0
        lax.fori_loop(0, n_chunks, chunk_body, 0)
        return 0
    lax.fori_loop(0, n_seq // seq_unroll, seq_body, 0)


def _conv(h, hist, conv_w, conv_b, ln_g, ln_b, n_seq, rows):
    b, t, _ = h.shape
    sub_rows = min(rows, 32)
    seq_unroll = max(1, min(n_seq, 32 // t))
    kern = functools.partial(_conv_kernel, rows=rows, sub_rows=sub_rows, n_chunks=t // rows,
                             n_seq=n_seq, seq_unroll=seq_unroll)
    vec = pl.BlockSpec((1, C_CONV), lambda i: (0, 0))
    return pl.pallas_call(
        kern,
        grid=(b // n_seq,),
        in_specs=[pl.BlockSpec((n_seq, t, C_CONV), lambda i: (i, 0, 0)),
                  pl.BlockSpec((n_seq, CONV_PAD, C_CONV), lambda i: (i, 0, 0)),
                  pl.BlockSpec((CONV_WIDTH, C_CONV), lambda i: (0, 0)),
                  vec, vec, vec],
        out_specs=pl.BlockSpec((n_seq, t, C_CONV), lambda i: (i, 0, 0)),
        out_shape=jax.ShapeDtypeStruct((b, t, C_CONV), BF16),
        scratch_shapes=[pltpu.VMEM((seq_unroll, rows + CONV_PAD, C_CONV), F32),
                        pltpu.VMEM((seq_unroll, SUBLANES, rows + CONV_PAD, C_CONV), F32)],
        compiler_params=_cparams("parallel"),
        name="conv",
    )(h, hist, conv_w, conv_b, ln_g, ln_b)


def _front_kernel(x_ref, g1_ref, win_ref, bnd_ref, mu_ref, w0_ref, wd_ref, a0_ref, wa_ref, wg_ref,
                  kk_ref, ka_ref, rk_ref, ones_ref,
                  h_ref, zlast_ref, r_ref, w_ref, k_ref, v_ref, nkk_ref, b_ref, bonus_ref, g_ref,
                  carry_ref, z_scr,
                  *, period, tiles_per_seq):
    x = x_ref[...]
    xn = x * lax.rsqrt(jnp.mean(x * x, axis=-1, keepdims=True) + RMS_EPS) * g1_ref[...]
    p = jnp.dot(xn.astype(BF16), win_ref[...], preferred_element_type=F32)
    h_ref[...] = p[:, :C_CONV] * jax.nn.sigmoid(p[:, C_CONV:2 * C_CONV])
    z = p[:, 2 * C_CONV:]
    rows = z.shape[0]
    row_id = lax.broadcasted_iota(jnp.int32, z.shape, 0)
    bnd = bnd_ref[...]
    if tiles_per_seq > 1:
        first_tile = pl.program_id(0) % tiles_per_seq == 0
        bnd = jnp.where(first_tile, bnd, carry_ref[pl.ds(SUBLANES - 1, 1), :])
        carry_ref[...] = z[rows - SUBLANES:, :]
    if period == rows:
        zlast_ref[...] = z[rows - 1:, :]
    else:
        for c in range(z_scr.shape[0]):
            cols = slice(c * LANES, (c + 1) * LANES)
            z_scr[c] = z[:, cols]
            zlast_ref[:, cols] = z_scr[c, pl.ds(period - 1, rows // period, stride=period), :]
    z_prev = jnp.where(row_id % period == 0, bnd, pltpu.roll(z, 1, axis=0))
    zs = z + (z_prev - z) * mu_ref[...]
    r = zs[:, :C_RWKV]
    k = zs[:, C_RWKV:2 * C_RWKV]
    v = zs[:, 2 * C_RWKV:3 * C_RWKV]
    wa = zs[:, 3 * C_RWKV:3 * C_RWKV + LANES]
    gl = zs[:, 3 * C_RWKV + LANES:]
    ones_bd = ones_ref[...]

    lw = w0_ref[...] + jnp.dot(jnp.tanh(wa).astype(BF16), wd_ref[...], preferred_element_type=F32)
    w_log = -(jnp.maximum(-lw, 0.0) + jnp.log(1.0 + jnp.exp(-jnp.abs(lw)))) - 0.5
    decay = jnp.exp(-jnp.exp(w_log))
    a = jax.nn.sigmoid(a0_ref[...] + jnp.dot(wa.astype(BF16), wa_ref[...],
                                             preferred_element_type=F32))
    g = jnp.dot(jax.nn.sigmoid(gl).astype(BF16), wg_ref[...], preferred_element_type=F32)

    kk = k * kk_ref[...]
    norm = jnp.maximum(jnp.sqrt(_segsum(kk * kk, ones_bd)), 1e-12)
    kk = kk / norm
    k_mod = k * (1.0 + (a - 1.0) * ka_ref[...])
    bonus = _segsum(r * k_mod * rk_ref[...], ones_bd) * v

    def store_cols(ref, x):
        xt = x.T
        ntok = ref.shape[2]
        for q in range(ref.shape[0]):
            ref[q] = xt[:, q * ntok:(q + 1) * ntok]

    store_cols(r_ref, r)
    store_cols(w_ref, decay)
    store_cols(k_ref, k_mod)
    store_cols(v_ref, v)
    store_cols(nkk_ref, -kk)
    store_cols(b_ref, kk * a)
    bonus_ref[...] = bonus
    g_ref[...] = g


def _col_major_spec(rows, length):
    if rows <= length:
        per_b = length // rows
        return pl.BlockSpec((1, C_RWKV, rows), lambda i: (i // per_b, 0, i % per_b))
    return pl.BlockSpec((rows // length, C_RWKV, length), lambda i: (i, 0, 0))


def _front(x2d, g1, w_in, bnd, params, rows, period, tiles_per_seq):
    n = x2d.shape[0]
    length = n // WKV_BATCH
    (mu, w0, wd_pad, a0, wa_pad, wg, k_k, k_a, r_k, ones_bd) = params
    kern = functools.partial(_front_kernel, period=period, tiles_per_seq=tiles_per_seq)
    vec = lambda c: pl.BlockSpec((1, c), lambda i: (0, 0))
    mat = lambda a: pl.BlockSpec(a.shape, lambda i: (0, 0))
    row = lambda c: pl.BlockSpec((rows, c), lambda i: (i, 0))
    row_sd = lambda c: jax.ShapeDtypeStruct((n, c), F32)
    col_spec = _col_major_spec(rows, length)
    col_sd = jax.ShapeDtypeStruct((WKV_BATCH, C_RWKV, length), F32)
    n_last = rows // period
    return pl.pallas_call(
        kern,
        grid=(n // rows,),
        in_specs=[row(D_MODEL), vec(D_MODEL), mat(w_in),
                  pl.BlockSpec((None, bnd.shape[1], D_SHIFT), lambda i: (i // tiles_per_seq, 0, 0)),
                  vec(D_SHIFT), vec(C_RWKV), mat(wd_pad), vec(C_RWKV), mat(wa_pad), mat(wg),
                  vec(C_RWKV), vec(C_RWKV), vec(C_RWKV), mat(ones_bd)],
        out_specs=[row(C_CONV), pl.BlockSpec((None, n_last, D_SHIFT), lambda i: (i, 0, 0))]
                  + [col_spec] * 6 + [row(C_RWKV)] * 2,
        out_shape=[row_sd(C_CONV), jax.ShapeDtypeStruct((n // rows, n_last, D_SHIFT), F32)]
                  + [col_sd] * 6 + [row_sd(C_RWKV)] * 2,
        scratch_shapes=[pltpu.VMEM((SUBLANES, D_SHIFT), F32),
                        pltpu.VMEM((D_SHIFT // LANES, rows, LANES), F32)],
        compiler_params=_cparams("arbitrary"),
        name="front",
    )(x2d, g1, w_in, bnd, mu, w0, wd_pad, a0, wa_pad, wg, k_k, k_a, r_k, ones_bd)


def _fold_halves(x):
    return x + pltpu.roll(x, BH_PER_GROUP, axis=x.ndim - 1)


def _key_cols_to_lanes(src_ref, dst_ref):
    chunk = 2 * N_HEADS
    for kp in range(K_HALF):
        pieces = [src_ref[b8, pl.ds(kp * chunk + half * N_HEADS, N_HEADS), :]
                  for half in range(2) for b8 in range(WKV_BATCH)]
        dst_ref[kp, :LOAD_ROWS, :] = jnp.concatenate(pieces, axis=0).T


def _value_cols_to_lanes(src_ref, dst_ref):
    for v in range(HEAD_DIM):
        pieces = [src_ref[b8, pl.ds(v * N_HEADS, N_HEADS), :]
                  for _ in range(2) for b8 in range(WKV_BATCH)]
        dst_ref[pl.ds(v, LOAD_ROWS, stride=HEAD_DIM), :] = jnp.concatenate(pieces, axis=0).T


def _lanes_to_value_cols(y_s, yt_ref):
    for v in range(HEAD_DIM):
        yt = y_s[pl.ds(v, LOAD_ROWS, stride=HEAD_DIM), :].T
        folded = yt[:BH_PER_GROUP] + yt[BH_PER_GROUP:]
        for b8 in range(WKV_BATCH):
            yt_ref[b8, pl.ds(v * N_HEADS, N_HEADS), :] = folded[b8 * N_HEADS:(b8 + 1) * N_HEADS]


def _wkv_kernel(w_ref, nkk_ref, b_ref, k_ref, r_ref, v_ref, s0_ref, y_ref, sout_ref,
                w_s, nkk_s, b_s, k_s, r_s, v_s, y_s, s_ref,
                *, t_blk, steps_per_load, carry_state, n_parts):
    j = pl.program_id(0)
    sub = j % steps_per_load
    part = HEAD_DIM // n_parts
    parts = [slice(i * part, (i + 1) * part) for i in range(n_parts)]

    @pl.when(sub == 0)
    def _():
        for src, dst in ((w_ref, w_s), (nkk_ref, nkk_s), (b_ref, b_s), (k_ref, k_s), (r_ref, r_s)):
            _key_cols_to_lanes(src, dst)
        _value_cols_to_lanes(v_ref, v_s)

    if carry_state:
        @pl.when(j == 0)
        def _():
            s_ref[...] = s0_ref[0]
    else:
        s_ref[...] = s0_ref[0]

    tok0 = sub * t_blk
    acc = jnp.zeros((HEAD_DIM, LANES), F32)
    for kp in range(K_HALF):
        acc = acc + s_ref[kp] * nkk_s[kp, pl.ds(tok0, 1), :]
    sa_init = tuple(_fold_halves(acc[p, :]) for p in parts[:-1]) + (acc[parts[-1], :],)

    def part_step(tok, nxt_tok, rows, sa):
        cur = pl.ds(tok, 1)
        vrows = pl.ds(pl.multiple_of(tok * HEAD_DIM, HEAD_DIM) + rows.start, part)
        vv = v_s[vrows, :]
        yacc = jnp.zeros((part, LANES), F32)
        sacc = jnp.zeros((part, LANES), F32)
        for kp in range(K_HALF):
            s = (s_ref[kp, rows, :] * w_s[kp, cur, :] + sa * b_s[kp, cur, :]
                 + vv * k_s[kp, cur, :])
            s_ref[kp, rows, :] = s
            yacc = yacc + s * r_s[kp, cur, :]
            sacc = sacc + s * nkk_s[kp, pl.ds(nxt_tok, 1), :]
        y_s[vrows, :] = yacc
        return sacc

    def step(t, sa_parts):
        tok = tok0 + t
        nxt_tok = tok0 + jnp.minimum(t + 1, t_blk - 1)
        sa_last = _fold_halves(sa_parts[-1])
        nxt = [_fold_halves(part_step(tok, nxt_tok, rows, sa))
               for rows, sa in zip(parts[:-1], sa_parts[:-1])]
        return tuple(nxt) + (part_step(tok, nxt_tok, parts[-1], sa_last),)

    lax.fori_loop(0, t_blk, step, sa_init)

    if carry_state:
        @pl.when(j == pl.num_programs(0) - 1)
        def _():
            sout_ref[0] = s_ref[...]
    else:
        sout_ref[0] = s_ref[...]

    @pl.when(sub == steps_per_load - 1)
    def _():
        _lanes_to_value_cols(y_s, y_ref)


def _wkv(w, nkk, b, k, r, v, s0, *, t_blk, carry_state):
    length = w.shape[2]
    steps_per_load = LOAD_ROWS // t_blk
    n_steps = length // t_blk
    tok_spec = pl.BlockSpec((WKV_BATCH, C_RWKV, LOAD_ROWS), lambda j: (0, 0, j // steps_per_load))
    state_blk = (1, K_HALF, HEAD_DIM, LANES)
    if carry_state:
        sspec = pl.BlockSpec(state_blk, lambda j: (0, 0, 0, 0))
    else:
        sspec = pl.BlockSpec(state_blk, lambda j: (j, 0, 0, 0))
    key_rows = pltpu.VMEM((K_HALF, LOAD_ROWS + SUBLANES, LANES), F32)
    val_rows = pltpu.VMEM((LOAD_ROWS * HEAD_DIM, LANES), F32)
    return pl.pallas_call(
        functools.partial(_wkv_kernel, t_blk=t_blk, steps_per_load=steps_per_load,
                          carry_state=carry_state, n_parts=WKV_PARTS),
        grid=(n_steps,),
        in_specs=[tok_spec] * 6 + [sspec],
        out_specs=[tok_spec, sspec],
        out_shape=[jax.ShapeDtypeStruct(w.shape, F32), jax.ShapeDtypeStruct(s0.shape, F32)],
        scratch_shapes=[key_rows] * 5 + [val_rows, val_rows,
                                         pltpu.VMEM((K_HALF, HEAD_DIM, LANES), F32)],
        compiler_params=_cparams("arbitrary"),
        name="wkv",
    )(w, nkk, b, k, r, v, s0)


def _state_to_lanes(s, n_grp):
    s = s.reshape(WKV_BATCH, n_grp, N_HEADS, HEAD_DIM, 2, K_HALF)
    return s.transpose(1, 5, 3, 4, 0, 2).reshape(n_grp, K_HALF, HEAD_DIM, LANES)


def _state_from_lanes(s, n_grp):
    s = s.reshape(n_grp, K_HALF, HEAD_DIM, 2, WKV_BATCH, N_HEADS)
    return s.transpose(4, 0, 5, 2, 3, 1).reshape(n_grp * WKV_BATCH, N_HEADS, HEAD_DIM, HEAD_DIM)


def _outproj_kernel(x_ref, c_ref, y_ref, bonus_ref, g_ref, gng_ref, gnb_ref, ones_ref,
                    wo_c_ref, wo_r_ref, n2_ref, wr_both_ref, br_ref,
                    h_ref, xn_ref, route_ref, cnt_ref):
    ones_bd = ones_ref[...]
    y = jnp.concatenate([y_ref[q].T for q in range(y_ref.shape[0])], axis=0)
    inv_n = 1.0 / HEAD_DIM
    mu = _segsum(y, ones_bd) * inv_n
    d = y - mu
    var = _segsum(d * d, ones_bd) * inv_n
    yn = d * lax.rsqrt(var + GN_EPS) * gng_ref[...] + gnb_ref[...]
    rw = (yn + bonus_ref[...]) * g_ref[...]
    h = (x_ref[...]
         + jnp.dot(c_ref[...], wo_c_ref[...], preferred_element_type=F32)
         + jnp.dot(rw.astype(BF16), wo_r_ref[...], preferred_element_type=F32))
    h_ref[...] = h
    xn = h * lax.rsqrt(jnp.mean(h * h, axis=-1, keepdims=True) + RMS_EPS) * n2_ref[...]
    xn_ref[...] = xn

    x_hi, x_lo = _split_bf16(xn)
    hi_both = jnp.dot(x_hi, wr_both_ref[...], preferred_element_type=F32)
    logits = (hi_both[:, :ROUTE_COLS]
              + jnp.dot(x_lo, wr_both_ref[:, :ROUTE_COLS], preferred_element_type=F32)
              + hi_both[:, ROUTE_COLS:]) + br_ref[...]

    col = lax.broadcasted_iota(jnp.int32, logits.shape, 1)
    neg = jnp.float32(-jnp.inf)
    big = jnp.int32(ROUTE_COLS)
    is_g = col < N_EXPERT_GROUPS
    g_logit = jnp.where(is_g, logits, neg)
    g_max = jnp.max(g_logit, axis=-1, keepdims=True)
    g_idx = jnp.min(jnp.where(g_logit == g_max, col, big), axis=-1, keepdims=True)
    p_group = 1.0 / jnp.sum(jnp.where(is_g, jnp.exp(logits - g_max), 0.0), axis=-1, keepdims=True)
    lo_col = EXPERT_COL0 + g_idx * EXPERTS_PER_GROUP
    in_grp = (col >= lo_col) & (col < lo_col + EXPERTS_PER_GROUP)
    e1 = jnp.where(in_grp, logits, neg)
    m1 = jnp.max(e1, axis=-1, keepdims=True)
    i1 = jnp.min(jnp.where(e1 == m1, col, big), axis=-1, keepdims=True)
    e2 = jnp.where(col == i1, neg, e1)
    m2 = jnp.max(e2, axis=-1, keepdims=True)
    i2 = jnp.min(jnp.where(e2 == m2, col, big), axis=-1, keepdims=True)
    ex = jnp.exp(m2 - m1)
    w1 = p_group / (1.0 + ex)
    w2 = p_group * ex / (1.0 + ex)

    exp1 = i1 - EXPERT_COL0
    exp2 = i2 - EXPERT_COL0
    oh1 = jnp.where(col == exp1, 1.0, 0.0)
    oh2 = jnp.where(col == exp2, 1.0, 0.0)
    tm = logits.shape[0]
    earlier = jnp.where(lax.broadcasted_iota(jnp.int32, (tm, tm), 0)
                        > lax.broadcasted_iota(jnp.int32, (tm, tm), 1), 1.0, 0.0).astype(BF16)
    before1 = jnp.dot(earlier, oh1.astype(BF16), preferred_element_type=F32)
    before2 = jnp.dot(earlier, oh2.astype(BF16), preferred_element_type=F32)
    cnt1 = jnp.sum(oh1, axis=0, keepdims=True)
    cnt2 = jnp.sum(oh2, axis=0, keepdims=True)
    rank1 = jnp.sum(before1 * oh1, axis=-1, keepdims=True)
    rank2 = jnp.sum((before2 + cnt1) * oh2, axis=-1, keepdims=True)
    fields = (exp1.astype(F32), exp2.astype(F32), w1, w2, rank1, rank2)
    route = jnp.zeros_like(logits)
    for c, val in enumerate(fields):
        route = jnp.where(col == c, val, route)
    route_ref[...] = route
    cnt_ref[...] = jnp.broadcast_to(cnt1 + cnt2, cnt_ref.shape)


def _outproj(x, conv_out, y, bonus, g, params, tm):
    n = x.shape[0]
    (gn_g, gn_b, ones_bd, wo_c, wo_r, n2_g, wr_both, br) = params
    row = lambda c: pl.BlockSpec((tm, c), lambda i: (i, 0))
    vec = lambda c: pl.BlockSpec((1, c), lambda i: (0, 0))
    mat = lambda a: pl.BlockSpec(a.shape, lambda i: (0, 0))
    return pl.pallas_call(
        _outproj_kernel,
        grid=(n // tm,),
        in_specs=[row(D_MODEL), row(C_CONV), _col_major_spec(tm, n // WKV_BATCH), row(C_RWKV),
                  row(C_RWKV),
                  vec(C_RWKV), vec(C_RWKV), mat(ones_bd), mat(wo_c), mat(wo_r), vec(D_MODEL),
                  mat(wr_both), vec(ROUTE_COLS)],
        out_specs=[row(D_MODEL), row(D_MODEL), row(ROUTE_COLS),
                   pl.BlockSpec((None, SUBLANES, ROUTE_COLS), lambda i: (i, 0, 0))],
        out_shape=[jax.ShapeDtypeStruct((n, D_MODEL), F32),
                   jax.ShapeDtypeStruct((n, D_MODEL), F32),
                   jax.ShapeDtypeStruct((n, ROUTE_COLS), F32),
                   jax.ShapeDtypeStruct((n // tm, SUBLANES, ROUTE_COLS), F32)],
        compiler_params=_cparams("parallel"),
        name="outproj_route",
    )(x, conv_out, y, bonus, g, gn_g, gn_b, ones_bd, wo_c, wo_r, n2_g, wr_both, br)


def _moe_plan(route, cnt, tm, moe_rows):
    n = route.shape[0]
    ids = route[:, 0:2].astype(jnp.int32)
    ranks = route[:, 4:6].astype(jnp.int32)
    tile_cnt = cnt[:, 0, :N_EXPERTS].astype(jnp.int32)
    before_tile = jnp.cumsum(tile_cnt, axis=0) - tile_cnt
    total = jnp.sum(tile_cnt, axis=0)
    tiles_e = (total + moe_rows - 1) // moe_rows
    tile_end = jnp.cumsum(tiles_e)
    row_start = (tile_end - tiles_e) * moe_rows
    first_row = (row_start[None, :] + before_tile)[:, None, None, :]
    one_hot = ids.reshape(-1, tm, 2, 1) == jnp.arange(N_EXPERTS, dtype=jnp.int32)
    pos = jnp.sum(jnp.where(one_hot, first_row, 0), axis=-1).reshape(n, 2) + ranks
    n_row_tiles = 2 * n // moe_rows + N_EXPERTS
    t = jnp.arange(n_row_tiles, dtype=jnp.int32)
    tile_expert = jnp.minimum(jnp.sum((t[:, None] >= tile_end[None, :]).astype(jnp.int32), axis=1),
                              N_EXPERTS - 1)
    tile_used = (t < tile_end[-1]).astype(jnp.int32)
    last_tile = jnp.maximum(tile_end[-1:] - 1, 0).astype(jnp.int32)
    is_last = jnp.any((t[:, None] == tile_end[None, :] - 1) & (tiles_e[None, :] > 0), axis=1)
    tile_fill = (is_last | (tile_used == 0)).astype(jnp.int32)
    return pos.reshape(-1), tile_fill, tile_expert.astype(jnp.int32), tile_used, last_tile


def _row_copy_wait(src_ref, dst_ref, sem, rows):
    pltpu.make_async_copy(src_ref.at[pl.ds(0, rows), :], dst_ref.at[pl.ds(0, rows), :], sem).wait()


def _dispatch_kernel(pos_ref, fill_ref, *refs, tm, group_tiles):
    x_refs, (xs_ref, zero_ref, sem) = refs[:len(group_tiles)], refs[len(group_tiles):]
    base = pl.program_id(0) * (2 * tm)
    fill_rows = zero_ref.shape[0]

    @pl.when(pl.program_id(0) == 0)
    def _():
        zero_ref[...] = jnp.zeros_like(zero_ref)
        n_row_tiles = xs_ref.shape[0] // fill_rows

        def fill(t, _):
            @pl.when(fill_ref[t] != 0)
            def _():
                start = pl.multiple_of(t * fill_rows, fill_rows)
                pltpu.make_async_copy(zero_ref, xs_ref.at[pl.ds(start, fill_rows), :], sem).start()
            return 0
        lax.fori_loop(0, n_row_tiles, fill, 0)

        def fill_wait(t, _):
            @pl.when(fill_ref[t] != 0)
            def _():
                pltpu.make_async_copy(zero_ref, xs_ref.at[pl.ds(0, fill_rows), :], sem).wait()
            return 0
        lax.fori_loop(0, n_row_tiles, fill_wait, 0)

    for x_ref, in_group in zip(x_refs, _group_preds(pl.program_id(0), group_tiles)):
        @pl.when(in_group)
        def _(x_ref=x_ref):
            def issue(r, _):
                src = x_ref.at[pl.ds(r, 1), :]
                for c in range(2):
                    dst = xs_ref.at[pl.ds(pos_ref[base + 2 * r + c], 1), :]
                    pltpu.make_async_copy(src, dst, sem).start()
                return 0
            lax.fori_loop(0, tm, issue, 0, unroll=DMA_ISSUE_UNROLL)
    for _ in range(2):
        _row_copy_wait(x_refs[0], xs_ref, sem, tm)


def _group_preds(i, group_tiles):
    preds, start = [], 0
    for tiles in group_tiles:
        preds.append((i >= start) & (i < start + tiles))
        start += tiles
    return preds


def _group_specs(block, group_tiles):
    specs, start = [], 0
    for tiles in group_tiles:
        specs.append(pl.BlockSpec(
            block, lambda i, *_, start=start, tiles=tiles: (jnp.clip(i - start, 0, tiles - 1), 0)))
        start += tiles
    return specs


def _dispatch(pos, tile_fill, xns, n_rows, moe_rows, tm):
    group_tiles = tuple(x.shape[0] // tm for x in xns)
    return pl.pallas_call(
        functools.partial(_dispatch_kernel, tm=tm, group_tiles=group_tiles),
        grid_spec=pltpu.PrefetchScalarGridSpec(
            num_scalar_prefetch=2,
            grid=(sum(group_tiles),),
            in_specs=_group_specs((tm, D_MODEL), group_tiles),
            out_specs=pl.BlockSpec(memory_space=pl.ANY),
            scratch_shapes=[pltpu.VMEM((moe_rows, D_MODEL), F32), pltpu.SemaphoreType.DMA(())]),
        out_shape=jax.ShapeDtypeStruct((n_rows, D_MODEL), F32),
        compiler_params=_cparams("arbitrary"),
        name="moe_dispatch",
    )(pos, tile_fill, *xns)


def _experts_kernel(expert_ref, used_ref, last_ref, xs_ref, wg_ref, wu_ref, wd_ref, y_ref,
                    wg_bf, wu_bf, wd_bf):
    del last_ref
    t = pl.program_id(0)

    @pl.when((t == 0) | (expert_ref[t] != expert_ref[jnp.maximum(t - 1, 0)]))
    def _():
        wg_bf[...] = wg_ref[0].astype(BF16)
        wu_bf[...] = wu_ref[0].astype(BF16)
        wd_bf[...] = wd_ref[0].astype(BF16)

    @pl.when(used_ref[t] != 0)
    def _():
        x = xs_ref[...].astype(BF16)
        hg = jnp.dot(x, wg_bf[...], preferred_element_type=F32)
        hu = jnp.dot(x, wu_bf[...], preferred_element_type=F32)
        act = hg * jax.nn.sigmoid(hg) * hu
        y_ref[...] = jnp.dot(act.astype(BF16), wd_bf[...], preferred_element_type=F32)

    @pl.when(used_ref[t] == 0)
    def _():
        y_ref[...] = jnp.zeros_like(y_ref)


def _experts(tile_expert, tile_used, last_tile, xs, w_gate, w_up, w_down):
    n_rows = xs.shape[0]
    moe_rows = n_rows // tile_expert.shape[0]
    row_tile = lambda t, ex, used, last: (jnp.where(used[t] != 0, t, last[0]), 0)
    weight = lambda t, ex, used, last: (ex[t], 0, 0)
    return pl.pallas_call(
        _experts_kernel,
        grid_spec=pltpu.PrefetchScalarGridSpec(
            num_scalar_prefetch=3,
            grid=(n_rows // moe_rows,),
            in_specs=[pl.BlockSpec((moe_rows, D_MODEL), row_tile),
                      pl.BlockSpec((1, D_MODEL, D_EXPERT), weight),
                      pl.BlockSpec((1, D_MODEL, D_EXPERT), weight),
                      pl.BlockSpec((1, D_EXPERT, D_MODEL), weight)],
            out_specs=pl.BlockSpec((moe_rows, D_MODEL), lambda t, ex, used, last: (t, 0)),
            scratch_shapes=[pltpu.VMEM((D_MODEL, D_EXPERT), BF16), pltpu.VMEM((D_MODEL, D_EXPERT), BF16),
                            pltpu.VMEM((D_EXPERT, D_MODEL), BF16)]),
        out_shape=jax.ShapeDtypeStruct((n_rows, D_MODEL), F32),
        compiler_params=_cparams("arbitrary"),
        name="moe_experts",
    )(tile_expert, tile_used, last_tile, xs, w_gate, w_up, w_down)


def _combine_kernel(pos_ref, *refs, tm, group_tiles):
    n_grp = len(group_tiles)
    h_refs, (route_ref, nf_ref, ys_ref) = refs[:n_grp], refs[n_grp:n_grp + 3]
    o_refs, (y_buf, sem) = refs[n_grp + 3:2 * n_grp + 3], refs[2 * n_grp + 3:]
    i = pl.program_id(0)
    last = pl.num_programs(0) - 1
    in_group = _group_preds(i, group_tiles)

    def issue_tile(base, slot):
        def issue(r, _):
            for c in range(2):
                src = ys_ref.at[pl.ds(pos_ref[base + 2 * r + c], 1), :]
                pltpu.make_async_copy(src, y_buf.at[slot, c, pl.ds(r, 1), :], sem.at[slot]).start()
            return 0
        lax.fori_loop(0, tm, issue, 0, unroll=DMA_ISSUE_UNROLL)

    def wait(slot):
        for c in range(2):
            _row_copy_wait(ys_ref, y_buf.at[slot, c], sem.at[slot], tm)

    @pl.when(i == 0)
    def _():
        issue_tile(0, 0)

    next_base = jnp.minimum(i + 1, last) * (2 * tm)

    def run(slot):
        other = 1 - slot
        issue_tile(next_base, other)
        wait(slot)
        route = route_ref[...]
        col = lax.broadcasted_iota(jnp.int32, route.shape, 1)
        w1 = jnp.sum(jnp.where(col == 2, route, 0.0), axis=-1, keepdims=True)
        w2 = jnp.sum(jnp.where(col == 3, route, 0.0), axis=-1, keepdims=True)
        moe = w1 * y_buf[slot, 0] + w2 * y_buf[slot, 1]
        for h_ref, o_ref, mine in zip(h_refs, o_refs, in_group):
            @pl.when(mine)
            def _(h_ref=h_ref, o_ref=o_ref):
                h = h_ref[...] + moe
                o_ref[...] = (h * lax.rsqrt(jnp.mean(h * h, axis=-1, keepdims=True) + RMS_EPS)
                              * nf_ref[...])

        @pl.when(i == last)
        def _():
            wait(other)

    for slot in range(2):
        pl.when(i % 2 == slot)(functools.partial(run, slot))


def _combine(pos, hs, route, nf_g, ys, tm):
    group_tiles = tuple(h.shape[0] // tm for h in hs)
    row_specs = _group_specs((tm, D_MODEL), group_tiles)
    return pl.pallas_call(
        functools.partial(_combine_kernel, tm=tm, group_tiles=group_tiles),
        grid_spec=pltpu.PrefetchScalarGridSpec(
            num_scalar_prefetch=1,
            grid=(sum(group_tiles),),
            in_specs=row_specs + [pl.BlockSpec((tm, ROUTE_COLS), lambda i, pos: (i, 0)),
                                  pl.BlockSpec((1, D_MODEL), lambda i, pos: (0, 0)),
                                  pl.BlockSpec(memory_space=pl.ANY)],
            out_specs=row_specs,
            scratch_shapes=[pltpu.VMEM((2, 2, tm, D_MODEL), F32), pltpu.SemaphoreType.DMA((2,))]),
        out_shape=[jax.ShapeDtypeStruct(h.shape, F32) for h in hs],
        compiler_params=_cparams("arbitrary"),
        name="moe_combine",
    )(pos, *hs, route, nf_g, ys)


def _moe(xns, routes, cnts, hs, w_gate, w_up, w_down, nf_g, tm):
    route = jnp.concatenate(routes, axis=0)
    cnt = jnp.concatenate(cnts, axis=0)
    moe_rows = min(MOE_ROWS, max(LANES, 2 * route.shape[0] // N_EXPERTS))
    pos, tile_fill, tile_expert, tile_used, last_tile = _moe_plan(route, cnt, tm, moe_rows)
    xs = _dispatch(pos, tile_fill, xns, tile_expert.shape[0] * moe_rows, moe_rows, DISPATCH_TILE)
    ys = _experts(tile_expert, tile_used, last_tile, xs, w_gate, w_up, w_down)
    return _combine(pos, hs, route, nf_g, ys, tm)


def _layer(x, conv_buf, shift_buf, wkv_state, lw, *, tm, conv_seqs, conv_rows, prep_rows,
           wkv_tblk):
    b, t, _ = x.shape
    n = b * t
    n_grp = b // WKV_BATCH
    x2d = x.reshape(n, D_MODEL)

    shift_buf = _shift_order(shift_buf)
    if prep_rows > t:
        period, tiles_per_seq = t, 1
        bnd = jnp.repeat(shift_buf, t, axis=0).reshape(n // prep_rows, prep_rows, D_SHIFT)
    else:
        period, tiles_per_seq = prep_rows, t // prep_rows
        bnd = shift_buf.reshape(b, 1, D_SHIFT)
    h_glu, z_last, r, w, k, v, nkk, bb, bonus, g = _front(
        x2d, lw["norm1_g"], lw["w_in"], bnd, lw["prep"], prep_rows, period, tiles_per_seq)
    z_last = z_last.reshape(b, -1, D_SHIFT)[:, -1]
    new_shift = _shift_order(z_last, inverse=True)

    h3 = h_glu.reshape(b, t, C_CONV)
    hist = jnp.concatenate([jnp.zeros((b, CONV_PAD - CONV_WIDTH + 1, C_CONV), F32), conv_buf], axis=1)
    n_hist = CONV_WIDTH - 1
    new_conv = h3[:, t - n_hist:] if t >= n_hist else jnp.concatenate([conv_buf[:, t:], h3], axis=1)
    conv_out = _conv(h3, hist, lw["conv_w"], lw["conv_b"], lw["conv_ln_g"], lw["conv_ln_b"],
                     conv_seqs, conv_rows)

    y_cols, s_l = _wkv(w, nkk, bb, k, r, v, _state_to_lanes(wkv_state, n_grp),
                       t_blk=wkv_tblk, carry_state=(n_grp == 1))
    new_wkv = _state_from_lanes(s_l, n_grp)

    moe_in = _outproj(x2d, conv_out.reshape(n, C_CONV), y_cols, bonus, g, lw["outproj"], tm)
    return moe_in, new_conv, new_shift, new_wkv


def _prepare_weights(norm1_g, w_in, conv_w, conv_b, conv_ln_g, conv_ln_b, mu_shift, w0,
                     w_decay_up, a0, w_aaa_up, w_gate_up, k_k, k_a, r_k, gn_g, gn_b, w_out,
                     norm2_g, w_router_group, b_router_group, w_router_expert, b_router_expert,
                     w_exp_gate, w_exp_up, w_exp_down, norm_f_g):
    row = lambda a: a.reshape(1, -1)
    head = jnp.arange(C_RWKV) % N_HEADS
    ones_bd = (head[:, None] == head[None, :LANES]).astype(BF16)
    w_in_perm = jnp.concatenate([w_in[:, :2 * C_CONV], _shift_order(w_in[:, 2 * C_CONV:])], axis=1)
    zeros_lora = jnp.zeros((LANES - w_decay_up.shape[0], C_RWKV), F32)
    wd_pad = jnp.concatenate([_key_order(w_decay_up), zeros_lora], axis=0).astype(BF16)
    wa_pad = jnp.concatenate([zeros_lora, _key_order(w_aaa_up)], axis=0).astype(BF16)
    key_vecs = _key_order(jnp.stack([w0, a0, k_k, k_a, r_k]))
    val_vecs = _value_order(jnp.stack([gn_g, gn_b]))
    w_out_r = (w_out[C_CONV:].reshape(N_HEADS, HEAD_DIM, D_MODEL).transpose(1, 0, 2)
               .reshape(C_RWKV, D_MODEL))
    pad_cols = ROUTE_COLS - N_EXPERT_GROUPS - N_EXPERTS
    w_route = jnp.concatenate([w_router_group, w_router_expert,
                               jnp.zeros((D_MODEL, pad_cols), F32)], axis=1)
    wr_both = jnp.concatenate(_split_bf16(w_route), axis=1)
    b_route = jnp.concatenate([b_router_group, b_router_expert, jnp.zeros((pad_cols,), F32)])
    return {
        "norm1_g": row(norm1_g), "w_in": w_in_perm.astype(BF16),
        "conv_w": conv_w, "conv_b": row(conv_b), "conv_ln_g": row(conv_ln_g),
        "conv_ln_b": row(conv_ln_b),
        "prep": (row(_shift_order(mu_shift)), key_vecs[0:1], wd_pad, key_vecs[1:2], wa_pad,
                 _value_order(w_gate_up).astype(BF16), key_vecs[2:3], key_vecs[3:4], key_vecs[4:5],
                 ones_bd),
        "outproj": (val_vecs[0:1], val_vecs[1:2], ones_bd, w_out[:C_CONV].astype(BF16),
                    w_out_r.astype(BF16), row(norm2_g), wr_both, row(b_route)),
        "w_gate": w_exp_gate.reshape(N_EXPERTS, D_MODEL, D_EXPERT),
        "w_up": w_exp_up.reshape(N_EXPERTS, D_MODEL, D_EXPERT),
        "w_down": w_exp_down.reshape(N_EXPERTS, D_EXPERT, D_MODEL),
        "norm_f_g": row(norm_f_g),
    }


def kernel(x_prompt, x_sample, state_conv, state_shift, state_wkv, norm1_g, w_in, conv_w, conv_b, conv_ln_g, conv_ln_b, mu_shift, w0, w_decay_up, a0, w_aaa_up, w_gate_up, k_k, k_a, r_k, gn_g, gn_b, w_out, norm2_g, w_router_group, b_router_group, w_router_expert, b_router_expert, w_exp_gate, w_exp_up, w_exp_down, norm_f_g):
    depth = norm1_g.shape[0]
    assert depth == 1, "single-layer trunk"
    b = x_prompt.shape[0]
    assert b == WKV_BATCH and x_prompt.shape[1] % LOAD_ROWS == 0
    assert x_sample.shape[0] * x_sample.shape[1] == WKV_BATCH * LOAD_ROWS
    lw = _prepare_weights(norm1_g[0], w_in[0], conv_w[0], conv_b[0], conv_ln_g[0], conv_ln_b[0],
                          mu_shift[0], w0[0], w_decay_up[0], a0[0], w_aaa_up[0], w_gate_up[0],
                          k_k[0], k_a[0], r_k[0], gn_g[0], gn_b[0], w_out[0], norm2_g[0],
                          w_router_group[0], b_router_group[0], w_router_expert[0],
                          b_router_expert[0], w_exp_gate[0], w_exp_up[0], w_exp_down[0], norm_f_g)
    dt = x_prompt.dtype
    tm = 512
    moe_p, conv_p, shift_p, wkv_p = _layer(
        x_prompt,
        jnp.zeros((b, CONV_WIDTH - 1, C_CONV), dt),
        jnp.zeros((b, D_SHIFT), dt),
        jnp.zeros((b, N_HEADS, HEAD_DIM, HEAD_DIM), dt), lw,
        tm=tm, conv_seqs=1, conv_rows=128, prep_rows=256, wkv_tblk=LOAD_ROWS)
    moe_s, conv_s, shift_s, wkv_s = _layer(
        x_sample, state_conv[0], state_shift[0], state_wkv[0], lw,
        tm=tm, conv_seqs=16, conv_rows=8, prep_rows=256, wkv_tblk=x_sample.shape[1])
    hs, xns, routes, cnts = zip(moe_p, moe_s)
    y_p, y_s = _moe(xns, routes, cnts, hs, lw["w_gate"], lw["w_up"], lw["w_down"],
                    lw["norm_f_g"], tm)
    y_p = y_p.reshape(x_prompt.shape)
    y_s = y_s.reshape(x_sample.shape)
    return (y_p, y_s, conv_p[None], shift_p[None], wkv_p[None],
            conv_s[None], shift_s[None], wkv_s[None])
```

```python
import functools

import jax
import jax.numpy as jnp
from jax import lax
from jax.experimental import pallas as pl
from jax.experimental.pallas import tpu as pltpu

F32 = jnp.float32
BF16 = jnp.bfloat16

D_MODEL = 1024
C_CONV = 512
C_RWKV = 512
HEAD_DIM = 64
N_HEADS = 8
CONV_WIDTH = 31
D_SHIFT = 1792
N_EXPERT_GROUPS = 4
EXPERTS_PER_GROUP = 8
N_EXPERTS = N_EXPERT_GROUPS * EXPERTS_PER_GROUP
D_EXPERT = 256
RMS_EPS = 1e-6
LN_EPS = 1e-5
GN_EPS = 64e-5

LANES = 128
SUBLANES = 8
CONV_PAD = 32
WKV_BATCH = 8
BH_PER_GROUP = WKV_BATCH * N_HEADS
K_HALF = HEAD_DIM // 2
WKV_PARTS = 2
LOAD_ROWS = LANES
ROUTE_COLS = LANES
EXPERT_COL0 = N_EXPERT_GROUPS
MOE_ROWS = 512
DISPATCH_TILE = 1024
DMA_ISSUE_UNROLL = 8
VMEM_LIMIT = 56 * 1024 * 1024


def _reorder_last(x, split, order):
    lead = x.shape[:-1]
    k = len(lead)
    x = x.reshape(lead + split).transpose(tuple(range(k)) + tuple(k + o for o in order))
    return x.reshape(lead + (-1,))


def _key_order(x):
    return _reorder_last(x, (N_HEADS, 2, K_HALF), (2, 1, 0))


def _value_order(x):
    return _reorder_last(x, (N_HEADS, HEAD_DIM), (1, 0))


def _shift_order(x, inverse=False):
    if inverse:
        key = lambda a: _reorder_last(a, (K_HALF, 2, N_HEADS), (2, 1, 0))
        val = lambda a: _reorder_last(a, (HEAD_DIM, N_HEADS), (1, 0))
    else:
        key, val = _key_order, _value_order
    return jnp.concatenate([key(x[..., :C_RWKV]), key(x[..., C_RWKV:2 * C_RWKV]),
                            val(x[..., 2 * C_RWKV:3 * C_RWKV]), x[..., 3 * C_RWKV:]], axis=-1)


def _cparams(*sem):
    return pltpu.CompilerParams(dimension_semantics=sem, vmem_limit_bytes=VMEM_LIMIT)


def _split_bf16(x):
    hi = x.astype(BF16)
    lo = (x - hi.astype(F32)).astype(BF16)
    return hi, lo


def _segsum(x, same_head):
    hi, lo = _split_bf16(x)
    s = (jnp.dot(hi, same_head, preferred_element_type=F32)
         + jnp.dot(lo, same_head, preferred_element_type=F32))
    return jnp.concatenate([s] * (x.shape[1] // LANES), axis=1)


def _conv_rows(prev, row0, win, sh, hs_ref, w_ref, b_ref, lg_ref, lb_ref, o_ref, *, rows, sub_rows):
    shifted_rows = rows + CONV_PAD - SUBLANES
    win[pl.ds(0, CONV_PAD), :] = prev
    win[pl.ds(CONV_PAD, rows), :] = hs_ref[pl.ds(row0, rows), :]
    sh[0] = win[...]
    for sft in range(1, SUBLANES):
        sh[sft, pl.ds(0, shifted_rows), :] = win[pl.ds(sft, shifted_rows), :]
    for blk in range(rows // sub_rows):
        r0 = blk * sub_rows
        acc = jnp.zeros((sub_rows, C_CONV), F32) + b_ref[...]
        for j in range(CONV_WIDTH):
            off = j + 2
            acc = acc + (sh[off % SUBLANES, pl.ds(off - off % SUBLANES + r0, sub_rows), :]
                         * w_ref[pl.ds(j, 1), :])
        mu = jnp.mean(acc, axis=-1, keepdims=True)
        d = acc - mu
        var = jnp.mean(d * d, axis=-1, keepdims=True)
        of = d * lax.rsqrt(var + LN_EPS) * lg_ref[...] + lb_ref[...]
        o_ref[pl.ds(row0 + r0, sub_rows), :] = of * jax.nn.sigmoid(of)


def _front_kernel(x_ref, g1_ref, w_in_ref, bnd_ref, hist_ref, cw_ref, cb_ref, clg_ref, clb_ref,
                  mu_ref, w0_ref, wd_ref, a0_ref, wa_ref, wg_ref, kk_ref, ka_ref, rk_ref, ones_ref,
                  h_ref, conv_ref, zlast_ref, r_ref, w_ref, k_ref, v_ref, nkk_ref, b_ref, bonus_ref,
                  g_ref,
                  carry_ref, z_scr, hcarry_ref, hs_ref, cwin_ref, csh_ref,
                  *, period, tiles_per_seq, conv_rows):
    x = x_ref[...]
    xn = x * lax.rsqrt(jnp.mean(x * x, axis=-1, keepdims=True) + RMS_EPS) * g1_ref[...]
    p = jnp.dot(xn.astype(BF16), w_in_ref[...], preferred_element_type=F32)
    h = p[:, :C_CONV] * jax.nn.sigmoid(p[:, C_CONV:2 * C_CONV])
    h_ref[...] = h
    hs_ref[...] = h
    z = p[:, 2 * C_CONV:]
    rows = z.shape[0]
    first_tile = pl.program_id(0) % tiles_per_seq == 0

    conv = functools.partial(_conv_rows, hs_ref=hs_ref, w_ref=cw_ref, b_ref=cb_ref, lg_ref=clg_ref,
                             lb_ref=clb_ref, o_ref=conv_ref, rows=conv_rows,
                             sub_rows=min(conv_rows, 32))
    n_slots = cwin_ref.shape[0]
    if period == rows:
        hist = hist_ref[0]
        if tiles_per_seq > 1:
            hist = jnp.where(first_tile, hist, hcarry_ref[...])
            hcarry_ref[...] = h[rows - CONV_PAD:, :]
        for c in range(rows // conv_rows):
            prev = hist if c == 0 else h[c * conv_rows - CONV_PAD:c * conv_rows, :]
            conv(prev, c * conv_rows, cwin_ref.at[c % n_slots], csh_ref.at[c % n_slots])
    else:
        def seq_group(i, _):
            for u in range(n_slots):
                s = i * n_slots + u
                conv(hist_ref[s], pl.multiple_of(s * period, SUBLANES), cwin_ref.at[u], csh_ref.at[u])
            return 0
        lax.fori_loop(0, rows // period // n_slots, seq_group, 0)

    row_id = lax.broadcasted_iota(jnp.int32, z.shape, 0)
    bnd = bnd_ref[...]
    if tiles_per_seq > 1:
        bnd = jnp.where(first_tile, bnd, carry_ref[pl.ds(SUBLANES - 1, 1), :])
        carry_ref[...] = z[rows - SUBLANES:, :]
    if period == rows:
        zlast_ref[...] = z[rows - 1:, :]
    else:
        for c in range(z_scr.shape[0]):
            cols = slice(c * LANES, (c + 1) * LANES)
            z_scr[c] = z[:, cols]
            zlast_ref[:, cols] = z_scr[c, pl.ds(period - 1, rows // period, stride=period), :]
    z_prev = jnp.where(row_id % period == 0, bnd, pltpu.roll(z, 1, axis=0))
    zs = z + (z_prev - z) * mu_ref[...]
    r = zs[:, :C_RWKV]
    k = zs[:, C_RWKV:2 * C_RWKV]
    v = zs[:, 2 * C_RWKV:3 * C_RWKV]
    wa = zs[:, 3 * C_RWKV:3 * C_RWKV + LANES]
    gl = zs[:, 3 * C_RWKV + LANES:]
    ones_bd = ones_ref[...]

    lw = w0_ref[...] + jnp.dot(jnp.tanh(wa).astype(BF16), wd_ref[...], preferred_element_type=F32)
    w_log = -(jnp.maximum(-lw, 0.0) + jnp.log(1.0 + jnp.exp(-jnp.abs(lw)))) - 0.5
    decay = jnp.exp(-jnp.exp(w_log))
    a = jax.nn.sigmoid(a0_ref[...] + jnp.dot(wa.astype(BF16), wa_ref[...],
                                             preferred_element_type=F32))
    g = jnp.dot(jax.nn.sigmoid(gl).astype(BF16), wg_ref[...], preferred_element_type=F32)

    kk = k * kk_ref[...]
    norm = jnp.maximum(jnp.sqrt(_segsum(kk * kk, ones_bd)), 1e-12)
    kk = kk / norm
    k_mod = k * (1.0 + (a - 1.0) * ka_ref[...])
    bonus = _segsum(r * k_mod * rk_ref[...], ones_bd) * v

    def store_cols(ref, x):
        xt = x.T
        ntok = ref.shape[2]
        for q in range(ref.shape[0]):
            ref[q] = xt[:, q * ntok:(q + 1) * ntok]

    store_cols(r_ref, r)
    store_cols(w_ref, decay)
    store_cols(k_ref, k_mod)
    store_cols(v_ref, v)
    store_cols(nkk_ref, -kk)
    store_cols(b_ref, kk * a)
    bonus_ref[...] = bonus
    g_ref[...] = g


def _col_major_spec(rows, length):
    if rows <= length:
        per_b = length // rows
        return pl.BlockSpec((1, C_RWKV, rows), lambda i: (i // per_b, 0, i % per_b))
    return pl.BlockSpec((rows // length, C_RWKV, length), lambda i: (i, 0, 0))


def _front(x2d, g1, w_in, bnd, hist, conv_params, params, rows, period, tiles_per_seq, conv_rows):
    n = x2d.shape[0]
    length = n // WKV_BATCH
    (mu, w0, wd_pad, a0, wa_pad, wg, k_k, k_a, r_k, ones_bd) = params
    kern = functools.partial(_front_kernel, period=period, tiles_per_seq=tiles_per_seq,
                             conv_rows=conv_rows)
    vec = lambda c: pl.BlockSpec((1, c), lambda i: (0, 0))
    mat = lambda a: pl.BlockSpec(a.shape, lambda i: (0, 0))
    row = lambda c: pl.BlockSpec((rows, c), lambda i: (i, 0))
    row_sd = lambda c: jax.ShapeDtypeStruct((n, c), F32)
    col_spec = _col_major_spec(rows, length)
    col_sd = jax.ShapeDtypeStruct((WKV_BATCH, C_RWKV, length), F32)
    n_last = rows // period
    n_slots = rows // conv_rows if period == rows else max(1, min(n_last, 32 // period))
    return pl.pallas_call(
        kern,
        grid=(n // rows,),
        in_specs=[row(D_MODEL), vec(D_MODEL), mat(w_in),
                  pl.BlockSpec((None, bnd.shape[1], D_SHIFT), lambda i: (i // tiles_per_seq, 0, 0)),
                  pl.BlockSpec((n_last, CONV_PAD, C_CONV), lambda i: (i // tiles_per_seq, 0, 0)),
                  pl.BlockSpec((CONV_WIDTH, C_CONV), lambda i: (0, 0)),
                  vec(C_CONV), vec(C_CONV), vec(C_CONV),
                  vec(D_SHIFT), vec(C_RWKV), mat(wd_pad), vec(C_RWKV), mat(wa_pad), mat(wg),
                  vec(C_RWKV), vec(C_RWKV), vec(C_RWKV), mat(ones_bd)],
        out_specs=[row(C_CONV), row(C_CONV),
                   pl.BlockSpec((None, n_last, D_SHIFT), lambda i: (i, 0, 0))]
                  + [col_spec] * 6 + [row(C_RWKV)] * 2,
        out_shape=[row_sd(C_CONV), row_sd(C_CONV),
                   jax.ShapeDtypeStruct((n // rows, n_last, D_SHIFT), F32)]
                  + [col_sd] * 6 + [row_sd(C_RWKV)] * 2,
        scratch_shapes=[pltpu.VMEM((SUBLANES, D_SHIFT), F32),
                        pltpu.VMEM((D_SHIFT // LANES, rows, LANES), F32),
                        pltpu.VMEM((CONV_PAD, C_CONV), F32),
                        pltpu.VMEM((rows, C_CONV), F32),
                        pltpu.VMEM((n_slots, conv_rows + CONV_PAD, C_CONV), F32),
                        pltpu.VMEM((n_slots, SUBLANES, conv_rows + CONV_PAD, C_CONV), F32)],
        compiler_params=_cparams("arbitrary"),
        name="front",
    )(x2d, g1, w_in, bnd, hist, *conv_params, mu, w0, wd_pad, a0, wa_pad, wg, k_k, k_a, r_k, ones_bd)


def _fold_halves(x):
    return x + pltpu.roll(x, BH_PER_GROUP, axis=x.ndim - 1)


def _key_cols_to_lanes(src_ref, dst_ref):
    chunk = 2 * N_HEADS
    for kp in range(K_HALF):
        pieces = [src_ref[b8, pl.ds(kp * chunk + half * N_HEADS, N_HEADS), :]
                  for half in range(2) for b8 in range(WKV_BATCH)]
        dst_ref[kp, :LOAD_ROWS, :] = jnp.concatenate(pieces, axis=0).T


def _value_cols_to_lanes(src_ref, dst_ref):
    for v in range(HEAD_DIM):
        pieces = [src_ref[b8, pl.ds(v * N_HEADS, N_HEADS), :]
                  for _ in range(2) for b8 in range(WKV_BATCH)]
        dst_ref[pl.ds(v, LOAD_ROWS, stride=HEAD_DIM), :] = jnp.concatenate(pieces, axis=0).T


def _lanes_to_value_cols(y_s, yt_ref):
    for v in range(HEAD_DIM):
        yt = y_s[pl.ds(v, LOAD_ROWS, stride=HEAD_DIM), :].T
        folded = yt[:BH_PER_GROUP] + yt[BH_PER_GROUP:]
        for b8 in range(WKV_BATCH):
            yt_ref[b8, pl.ds(v * N_HEADS, N_HEADS), :] = folded[b8 * N_HEADS:(b8 + 1) * N_HEADS]


def _wkv_kernel(w_ref, nkk_ref, b_ref, k_ref, r_ref, v_ref, s0_ref, y_ref, sout_ref,
                w_s, nkk_s, b_s, k_s, r_s, v_s, y_s, s_ref,
                *, t_blk, steps_per_load, carry_state, n_parts):
    j = pl.program_id(0)
    sub = j % steps_per_load
    part = HEAD_DIM // n_parts
    parts = [slice(i * part, (i + 1) * part) for i in range(n_parts)]

    @pl.when(sub == 0)
    def _():
        for src, dst in ((w_ref, w_s), (nkk_ref, nkk_s), (b_ref, b_s), (k_ref, k_s), (r_ref, r_s)):
            _key_cols_to_lanes(src, dst)
        _value_cols_to_lanes(v_ref, v_s)

    if carry_state:
        @pl.when(j == 0)
        def _():
            s_ref[...] = s0_ref[0]
    else:
        s_ref[...] = s0_ref[0]

    tok0 = sub * t_blk
    acc = jnp.zeros((HEAD_DIM, LANES), F32)
    for kp in range(K_HALF):
        acc = acc + s_ref[kp] * nkk_s[kp, pl.ds(tok0, 1), :]
    sa_init = tuple(_fold_halves(acc[p, :]) for p in parts[:-1]) + (acc[parts[-1], :],)

    def part_step(tok, nxt_tok, rows, sa):
        cur = pl.ds(tok, 1)
        vrows = pl.ds(pl.multiple_of(tok * HEAD_DIM, HEAD_DIM) + rows.start, part)
        vv = v_s[vrows, :]
        yacc = jnp.zeros((part, LANES), F32)
        sacc = jnp.zeros((part, LANES), F32)
        for kp in range(K_HALF):
            s = (s_ref[kp, rows, :] * w_s[kp, cur, :] + sa * b_s[kp, cur, :]
                 + vv * k_s[kp, cur, :])
            s_ref[kp, rows, :] = s
            yacc = yacc + s * r_s[kp, cur, :]
            sacc = sacc + s * nkk_s[kp, pl.ds(nxt_tok, 1), :]
        y_s[vrows, :] = yacc
        return sacc

    def step(t, sa_parts):
        tok = tok0 + t
        nxt_tok = tok0 + jnp.minimum(t + 1, t_blk - 1)
        sa_last = _fold_halves(sa_parts[-1])
        nxt = [_fold_halves(part_step(tok, nxt_tok, rows, sa))
               for rows, sa in zip(parts[:-1], sa_parts[:-1])]
        return tuple(nxt) + (part_step(tok, nxt_tok, parts[-1], sa_last),)

    lax.fori_loop(0, t_blk, step, sa_init)

    if carry_state:
        @pl.when(j == pl.num_programs(0) - 1)
        def _():
            sout_ref[0] = s_ref[...]
    else:
        sout_ref[0] = s_ref[...]

    @pl.when(sub == steps_per_load - 1)
    def _():
        _lanes_to_value_cols(y_s, y_ref)


def _wkv(w, nkk, b, k, r, v, s0, *, t_blk, carry_state):
    length = w.shape[2]
    steps_per_load = LOAD_ROWS // t_blk
    n_steps = length // t_blk
    tok_spec = pl.BlockSpec((WKV_BATCH, C_RWKV, LOAD_ROWS), lambda j: (0, 0, j // steps_per_load))
    state_blk = (1, K_HALF, HEAD_DIM, LANES)
    if carry_state:
        sspec = pl.BlockSpec(state_blk, lambda j: (0, 0, 0, 0))
    else:
        sspec = pl.BlockSpec(state_blk, lambda j: (j, 0, 0, 0))
    key_rows = pltpu.VMEM((K_HALF, LOAD_ROWS + SUBLANES, LANES), F32)
    val_rows = pltpu.VMEM((LOAD_ROWS * HEAD_DIM, LANES), F32)
    return pl.pallas_call(
        functools.partial(_wkv_kernel, t_blk=t_blk, steps_per_load=steps_per_load,
                          carry_state=carry_state, n_parts=WKV_PARTS),
        grid=(n_steps,),
        in_specs=[tok_spec] * 6 + [sspec],
        out_specs=[tok_spec, sspec],
        out_shape=[jax.ShapeDtypeStruct(w.shape, F32), jax.ShapeDtypeStruct(s0.shape, F32)],
        scratch_shapes=[key_rows] * 5 + [val_rows, val_rows,
                                         pltpu.VMEM((K_HALF, HEAD_DIM, LANES), F32)],
        compiler_params=_cparams("arbitrary"),
        name="wkv",
    )(w, nkk, b, k, r, v, s0)


def _state_to_lanes(s, n_grp):
    s = s.reshape(WKV_BATCH, n_grp, N_HEADS, HEAD_DIM, 2, K_HALF)
    return s.transpose(1, 5, 3, 4, 0, 2).reshape(n_grp, K_HALF, HEAD_DIM, LANES)


def _state_from_lanes(s, n_grp):
    s = s.reshape(n_grp, K_HALF, HEAD_DIM, 2, WKV_BATCH, N_HEADS)
    return s.transpose(4, 0, 5, 2, 3, 1).reshape(n_grp * WKV_BATCH, N_HEADS, HEAD_DIM, HEAD_DIM)


def _outproj_kernel(x_ref, c_ref, y_ref, bonus_ref, g_ref, gng_ref, gnb_ref, ones_ref,
                    wo_c_ref, wo_r_ref, n2_ref, wr_both_ref, br_ref,
                    h_ref, xn_ref, route_ref, cnt_ref):
    ones_bd = ones_ref[...]
    y = jnp.concatenate([y_ref[q].T for q in range(y_ref.shape[0])], axis=0)
    inv_n = 1.0 / HEAD_DIM
    mu = _segsum(y, ones_bd) * inv_n
    d = y - mu
    var = _segsum(d * d, ones_bd) * inv_n
    yn = d * lax.rsqrt(var + GN_EPS) * gng_ref[...] + gnb_ref[...]
    rw = (yn + bonus_ref[...]) * g_ref[...]
    h = (x_ref[...]
         + jnp.dot(c_ref[...].astype(BF16), wo_c_ref[...], preferred_element_type=F32)
         + jnp.dot(rw.astype(BF16), wo_r_ref[...], preferred_element_type=F32))
    h_ref[...] = h
    xn = h * lax.rsqrt(jnp.mean(h * h, axis=-1, keepdims=True) + RMS_EPS) * n2_ref[...]
    xn_ref[...] = xn

    x_hi, x_lo = _split_bf16(xn)
    hi_both = jnp.dot(x_hi, wr_both_ref[...], preferred_element_type=F32)
    logits = (hi_both[:, :ROUTE_COLS]
              + jnp.dot(x_lo, wr_both_ref[:, :ROUTE_COLS], preferred_element_type=F32)
              + hi_both[:, ROUTE_COLS:]) + br_ref[...]

    col = lax.broadcasted_iota(jnp.int32, logits.shape, 1)
    neg = jnp.float32(-jnp.inf)
    big = jnp.int32(ROUTE_COLS)
    is_g = col < N_EXPERT_GROUPS
    g_logit = jnp.where(is_g, logits, neg)
    g_max = jnp.max(g_logit, axis=-1, keepdims=True)
    g_idx = jnp.min(jnp.where(g_logit == g_max, col, big), axis=-1, keepdims=True)
    p_group = 1.0 / jnp.sum(jnp.where(is_g, jnp.exp(logits - g_max), 0.0), axis=-1, keepdims=True)
    lo_col = EXPERT_COL0 + g_idx * EXPERTS_PER_GROUP
    in_grp = (col >= lo_col) & (col < lo_col + EXPERTS_PER_GROUP)
    e1 = jnp.where(in_grp, logits, neg)
    m1 = jnp.max(e1, axis=-1, keepdims=True)
    i1 = jnp.min(jnp.where(e1 == m1, col, big), axis=-1, keepdims=True)
    e2 = jnp.where(col == i1, neg, e1)
    m2 = jnp.max(e2, axis=-1, keepdims=True)
    i2 = jnp.min(jnp.where(e2 == m2, col, big), axis=-1, keepdims=True)
    ex = jnp.exp(m2 - m1)
    w1 = p_group / (1.0 + ex)
    w2 = p_group * ex / (1.0 + ex)

    exp1 = i1 - EXPERT_COL0
    exp2 = i2 - EXPERT_COL0
    oh1 = jnp.where(col == exp1, 1.0, 0.0)
    oh2 = jnp.where(col == exp2, 1.0, 0.0)
    tm = logits.shape[0]
    earlier = jnp.where(lax.broadcasted_iota(jnp.int32, (tm, tm), 0)
                        > lax.broadcasted_iota(jnp.int32, (tm, tm), 1), 1.0, 0.0).astype(BF16)
    before1 = jnp.dot(earlier, oh1.astype(BF16), preferred_element_type=F32)
    before2 = jnp.dot(earlier, oh2.astype(BF16), preferred_element_type=F32)
    cnt1 = jnp.sum(oh1, axis=0, keepdims=True)
    cnt2 = jnp.sum(oh2, axis=0, keepdims=True)
    rank1 = jnp.sum(before1 * oh1, axis=-1, keepdims=True)
    rank2 = jnp.sum((before2 + cnt1) * oh2, axis=-1, keepdims=True)
    fields = (exp1.astype(F32), exp2.astype(F32), w1, w2, rank1, rank2)
    route = jnp.zeros_like(logits)
    for c, val in enumerate(fields):
        route = jnp.where(col == c, val, route)
    route_ref[...] = route
    cnt_ref[...] = jnp.broadcast_to(cnt1 + cnt2, cnt_ref.shape)


def _outproj(x, conv_out, y, bonus, g, params, tm):
    n = x.shape[0]
    (gn_g, gn_b, ones_bd, wo_c, wo_r, n2_g, wr_both, br) = params
    row = lambda c: pl.BlockSpec((tm, c), lambda i: (i, 0))
    vec = lambda c: pl.BlockSpec((1, c), lambda i: (0, 0))
    mat = lambda a: pl.BlockSpec(a.shape, lambda i: (0, 0))
    return pl.pallas_call(
        _outproj_kernel,
        grid=(n // tm,),
        in_specs=[row(D_MODEL), row(C_CONV), _col_major_spec(tm, n // WKV_BATCH), row(C_RWKV),
                  row(C_RWKV),
                  vec(C_RWKV), vec(C_RWKV), mat(ones_bd), mat(wo_c), mat(wo_r), vec(D_MODEL),
                  mat(wr_both), vec(ROUTE_COLS)],
        out_specs=[row(D_MODEL), row(D_MODEL), row(ROUTE_COLS),
                   pl.BlockSpec((None, SUBLANES, ROUTE_COLS), lambda i: (i, 0, 0))],
        out_shape=[jax.ShapeDtypeStruct((n, D_MODEL), F32),
                   jax.ShapeDtypeStruct((n, D_MODEL), F32),
                   jax.ShapeDtypeStruct((n, ROUTE_COLS), F32),
                   jax.ShapeDtypeStruct((n // tm, SUBLANES, ROUTE_COLS), F32)],
        compiler_params=_cparams("parallel"),
        name="outproj_route",
    )(x, conv_out, y, bonus, g, gn_g, gn_b, ones_bd, wo_c, wo_r, n2_g, wr_both, br)


def _moe_plan(route, cnt, tm, moe_rows):
    n = route.shape[0]
    ids = route[:, 0:2].astype(jnp.int32)
    ranks = route[:, 4:6].astype(jnp.int32)
    tile_cnt = cnt[:, 0, :N_EXPERTS].astype(jnp.int32)
    before_tile = jnp.cumsum(tile_cnt, axis=0) - tile_cnt
    total = jnp.sum(tile_cnt, axis=0)
    tiles_e = (total + moe_rows - 1) // moe_rows
    tile_end = jnp.cumsum(tiles_e)
    row_start = (tile_end - tiles_e) * moe_rows
    first_row = (row_start[None, :] + before_tile)[:, None, None, :]
    one_hot = ids.reshape(-1, tm, 2, 1) == jnp.arange(N_EXPERTS, dtype=jnp.int32)
    pos = jnp.sum(jnp.where(one_hot, first_row, 0), axis=-1).reshape(n, 2) + ranks
    n_row_tiles = 2 * n // moe_rows + N_EXPERTS
    t = jnp.arange(n_row_tiles, dtype=jnp.int32)
    tile_expert = jnp.minimum(jnp.sum((t[:, None] >= tile_end[None, :]).astype(jnp.int32), axis=1),
                              N_EXPERTS - 1)
    tile_used = (t < tile_end[-1]).astype(jnp.int32)
    last_tile = jnp.maximum(tile_end[-1:] - 1, 0).astype(jnp.int32)
    is_last = jnp.any((t[:, None] == tile_end[None, :] - 1) & (tiles_e[None, :] > 0), axis=1)
    tile_fill = (is_last | (tile_used == 0)).astype(jnp.int32)
    return pos.reshape(-1), tile_fill, tile_expert.astype(jnp.int32), tile_used, last_tile


def _row_copy_wait(src_ref, dst_ref, sem, rows):
    pltpu.make_async_copy(src_ref.at[pl.ds(0, rows), :], dst_ref.at[pl.ds(0, rows), :], sem).wait()


def _dispatch_kernel(pos_ref, fill_ref, *refs, tm, group_tiles):
    x_refs, (xs_ref, zero_ref, sem) = refs[:len(group_tiles)], refs[len(group_tiles):]
    base = pl.program_id(0) * (2 * tm)
    fill_rows = zero_ref.shape[0]

    @pl.when(pl.program_id(0) == 0)
    def _():
        zero_ref[...] = jnp.zeros_like(zero_ref)
        n_row_tiles = xs_ref.shape[0] // fill_rows

        def fill(t, _):
            @pl.when(fill_ref[t] != 0)
            def _():
                start = pl.multiple_of(t * fill_rows, fill_rows)
                pltpu.make_async_copy(zero_ref, xs_ref.at[pl.ds(start, fill_rows), :], sem).start()
            return 0
        lax.fori_loop(0, n_row_tiles, fill, 0)

        def fill_wait(t, _):
            @pl.when(fill_ref[t] != 0)
            def _():
                pltpu.make_async_copy(zero_ref, xs_ref.at[pl.ds(0, fill_rows), :], sem).wait()
            return 0
        lax.fori_loop(0, n_row_tiles, fill_wait, 0)

    for x_ref, in_group in zip(x_refs, _group_preds(pl.program_id(0), group_tiles)):
        @pl.when(in_group)
        def _(x_ref=x_ref):
            def issue(r, _):
                src = x_ref.at[pl.ds(r, 1), :]
                for c in range(2):
                    dst = xs_ref.at[pl.ds(pos_ref[base + 2 * r + c], 1), :]
                    pltpu.make_async_copy(src, dst, sem).start()
                return 0
            lax.fori_loop(0, tm, issue, 0, unroll=DMA_ISSUE_UNROLL)
    for _ in range(2):
        _row_copy_wait(x_refs[0], xs_ref, sem, tm)


def _group_preds(i, group_tiles):
    preds, start = [], 0
    for tiles in group_tiles:
        preds.append((i >= start) & (i < start + tiles))
        start += tiles
    return preds


def _group_specs(block, group_tiles):
    specs, start = [], 0
    for tiles in group_tiles:
        specs.append(pl.BlockSpec(
            block, lambda i, *_, start=start, tiles=tiles: (jnp.clip(i - start, 0, tiles - 1), 0)))
        start += tiles
    return specs


def _dispatch(pos, tile_fill, xns, n_rows, moe_rows, tm):
    group_tiles = tuple(x.shape[0] // tm for x in xns)
    return pl.pallas_call(
        functools.partial(_dispatch_kernel, tm=tm, group_tiles=group_tiles),
        grid_spec=pltpu.PrefetchScalarGridSpec(
            num_scalar_prefetch=2,
            grid=(sum(group_tiles),),
            in_specs=_group_specs((tm, D_MODEL), group_tiles),
            out_specs=pl.BlockSpec(memory_space=pl.ANY),
            scratch_shapes=[pltpu.VMEM((moe_rows, D_MODEL), F32), pltpu.SemaphoreType.DMA(())]),
        out_shape=jax.ShapeDtypeStruct((n_rows, D_MODEL), F32),
        compiler_params=_cparams("arbitrary"),
        name="moe_dispatch",
    )(pos, tile_fill, *xns)


def _experts_kernel(expert_ref, used_ref, last_ref, xs_ref, wg_ref, wu_ref, wd_ref, y_ref,
                    wg_bf, wu_bf, wd_bf):
    del last_ref
    t = pl.program_id(0)

    @pl.when((t == 0) | (expert_ref[t] != expert_ref[jnp.maximum(t - 1, 0)]))
    def _():
        wg_bf[...] = wg_ref[0].astype(BF16)
        wu_bf[...] = wu_ref[0].astype(BF16)
        wd_bf[...] = wd_ref[0].astype(BF16)

    @pl.when(used_ref[t] != 0)
    def _():
        x = xs_ref[...].astype(BF16)
        hg = jnp.dot(x, wg_bf[...], preferred_element_type=F32)
        hu = jnp.dot(x, wu_bf[...], preferred_element_type=F32)
        act = hg * jax.nn.sigmoid(hg) * hu
        y_ref[...] = jnp.dot(act.astype(BF16), wd_bf[...], preferred_element_type=F32)

    @pl.when(used_ref[t] == 0)
    def _():
        y_ref[...] = jnp.zeros_like(y_ref)


def _experts(tile_expert, tile_used, last_tile, xs, w_gate, w_up, w_down):
    n_rows = xs.shape[0]
    moe_rows = n_rows // tile_expert.shape[0]
    row_tile = lambda t, ex, used, last: (jnp.where(used[t] != 0, t, last[0]), 0)
    weight = lambda t, ex, used, last: (ex[t], 0, 0)
    return pl.pallas_call(
        _experts_kernel,
        grid_spec=pltpu.PrefetchScalarGridSpec(
            num_scalar_prefetch=3,
            grid=(n_rows // moe_rows,),
            in_specs=[pl.BlockSpec((moe_rows, D_MODEL), row_tile),
                      pl.BlockSpec((1, D_MODEL, D_EXPERT), weight),
                      pl.BlockSpec((1, D_MODEL, D_EXPERT), weight),
                      pl.BlockSpec((1, D_EXPERT, D_MODEL), weight)],
            out_specs=pl.BlockSpec((moe_rows, D_MODEL), lambda t, ex, used, last: (t, 0)),
            scratch_shapes=[pltpu.VMEM((D_MODEL, D_EXPERT), BF16), pltpu.VMEM((D_MODEL, D_EXPERT), BF16),
                            pltpu.VMEM((D_EXPERT, D_MODEL), BF16)]),
        out_shape=jax.ShapeDtypeStruct((n_rows, D_MODEL), F32),
        compiler_params=_cparams("arbitrary"),
        name="moe_experts",
    )(tile_expert, tile_used, last_tile, xs, w_gate, w_up, w_down)


def _combine_kernel(pos_ref, *refs, tm, group_tiles):
    n_grp = len(group_tiles)
    h_refs, (route_ref, nf_ref, ys_ref) = refs[:n_grp], refs[n_grp:n_grp + 3]
    o_refs, (y_buf, sem) = refs[n_grp + 3:2 * n_grp + 3], refs[2 * n_grp + 3:]
    i = pl.program_id(0)
    last = pl.num_programs(0) - 1
    in_group = _group_preds(i, group_tiles)

    def issue_tile(base, slot):
        def issue(r, _):
            for c in range(2):
                src = ys_ref.at[pl.ds(pos_ref[base + 2 * r + c], 1), :]
                pltpu.make_async_copy(src, y_buf.at[slot, c, pl.ds(r, 1), :], sem.at[slot]).start()
            return 0
        lax.fori_loop(0, tm, issue, 0, unroll=DMA_ISSUE_UNROLL)

    def wait(slot):
        for c in range(2):
            _row_copy_wait(ys_ref, y_buf.at[slot, c], sem.at[slot], tm)

    @pl.when(i == 0)
    def _():
        issue_tile(0, 0)

    next_base = jnp.minimum(i + 1, last) * (2 * tm)

    def run(slot):
        other = 1 - slot
        issue_tile(next_base, other)
        wait(slot)
        route = route_ref[...]
        col = lax.broadcasted_iota(jnp.int32, route.shape, 1)
        w1 = jnp.sum(jnp.where(col == 2, route, 0.0), axis=-1, keepdims=True)
        w2 = jnp.sum(jnp.where(col == 3, route, 0.0), axis=-1, keepdims=True)
        moe = w1 * y_buf[slot, 0] + w2 * y_buf[slot, 1]
        for h_ref, o_ref, mine in zip(h_refs, o_refs, in_group):
            @pl.when(mine)
            def _(h_ref=h_ref, o_ref=o_ref):
                h = h_ref[...] + moe
                o_ref[...] = (h * lax.rsqrt(jnp.mean(h * h, axis=-1, keepdims=True) + RMS_EPS)
                              * nf_ref[...])

        @pl.when(i == last)
        def _():
            wait(other)

    for slot in range(2):
        pl.when(i % 2 == slot)(functools.partial(run, slot))


def _combine(pos, hs, route, nf_g, ys, tm):
    group_tiles = tuple(h.shape[0] // tm for h in hs)
    row_specs = _group_specs((tm, D_MODEL), group_tiles)
    return pl.pallas_call(
        functools.partial(_combine_kernel, tm=tm, group_tiles=group_tiles),
        grid_spec=pltpu.PrefetchScalarGridSpec(
            num_scalar_prefetch=1,
            grid=(sum(group_tiles),),
            in_specs=row_specs + [pl.BlockSpec((tm, ROUTE_COLS), lambda i, pos: (i, 0)),
                                  pl.BlockSpec((1, D_MODEL), lambda i, pos: (0, 0)),
                                  pl.BlockSpec(memory_space=pl.ANY)],
            out_specs=row_specs,
            scratch_shapes=[pltpu.VMEM((2, 2, tm, D_MODEL), F32), pltpu.SemaphoreType.DMA((2,))]),
        out_shape=[jax.ShapeDtypeStruct(h.shape, F32) for h in hs],
        compiler_params=_cparams("arbitrary"),
        name="moe_combine",
    )(pos, *hs, route, nf_g, ys)


def _moe(xns, routes, cnts, hs, w_gate, w_up, w_down, nf_g, tm):
    route = jnp.concatenate(routes, axis=0)
    cnt = jnp.concatenate(cnts, axis=0)
    moe_rows = min(MOE_ROWS, max(LANES, 2 * route.shape[0] // N_EXPERTS))
    pos, tile_fill, tile_expert, tile_used, last_tile = _moe_plan(route, cnt, tm, moe_rows)
    xs = _dispatch(pos, tile_fill, xns, tile_expert.shape[0] * moe_rows, moe_rows, DISPATCH_TILE)
    ys = _experts(tile_expert, tile_used, last_tile, xs, w_gate, w_up, w_down)
    return _combine(pos, hs, route, nf_g, ys, tm)


def _layer(x, conv_buf, shift_buf, wkv_state, lw, *, tm, conv_rows, prep_rows, wkv_tblk):
    b, t, _ = x.shape
    n = b * t
    n_grp = b // WKV_BATCH
    x2d = x.reshape(n, D_MODEL)

    shift_buf = _shift_order(shift_buf)
    if prep_rows > t:
        period, tiles_per_seq = t, 1
        bnd = jnp.repeat(shift_buf, t, axis=0).reshape(n // prep_rows, prep_rows, D_SHIFT)
    else:
        period, tiles_per_seq = prep_rows, t // prep_rows
        bnd = shift_buf.reshape(b, 1, D_SHIFT)
    hist = jnp.concatenate([jnp.zeros((b, CONV_PAD - CONV_WIDTH + 1, C_CONV), F32), conv_buf], axis=1)
    h_glu, conv_out, z_last, r, w, k, v, nkk, bb, bonus, g = _front(
        x2d, lw["norm1_g"], lw["w_in"], bnd, hist, lw["conv"], lw["prep"], prep_rows, period,
        tiles_per_seq, conv_rows)
    z_last = z_last.reshape(b, -1, D_SHIFT)[:, -1]
    new_shift = _shift_order(z_last, inverse=True)
    h3 = h_glu.reshape(b, t, C_CONV)
    n_hist = CONV_WIDTH - 1
    new_conv = h3[:, t - n_hist:] if t >= n_hist else jnp.concatenate([conv_buf[:, t:], h3], axis=1)

    y_cols, s_l = _wkv(w, nkk, bb, k, r, v, _state_to_lanes(wkv_state, n_grp),
                       t_blk=wkv_tblk, carry_state=(n_grp == 1))
    new_wkv = _state_from_lanes(s_l, n_grp)

    moe_in = _outproj(x2d, conv_out, y_cols, bonus, g, lw["outproj"], tm)
    return moe_in, new_conv, new_shift, new_wkv


def _prepare_weights(norm1_g, w_in, conv_w, conv_b, conv_ln_g, conv_ln_b, mu_shift, w0,
                     w_decay_up, a0, w_aaa_up, w_gate_up, k_k, k_a, r_k, gn_g, gn_b, w_out,
                     norm2_g, w_router_group, b_router_group, w_router_expert, b_router_expert,
                     w_exp_gate, w_exp_up, w_exp_down, norm_f_g):
    row = lambda a: a.reshape(1, -1)
    head = jnp.arange(C_RWKV) % N_HEADS
    ones_bd = (head[:, None] == head[None, :LANES]).astype(BF16)
    w_in_perm = jnp.concatenate([w_in[:, :2 * C_CONV], _shift_order(w_in[:, 2 * C_CONV:])], axis=1)
    zeros_lora = jnp.zeros((LANES - w_decay_up.shape[0], C_RWKV), F32)
    wd_pad = jnp.concatenate([_key_order(w_decay_up), zeros_lora], axis=0).astype(BF16)
    wa_pad = jnp.concatenate([zeros_lora, _key_order(w_aaa_up)], axis=0).astype(BF16)
    key_vecs = _key_order(jnp.stack([w0, a0, k_k, k_a, r_k]))
    val_vecs = _value_order(jnp.stack([gn_g, gn_b]))
    w_out_r = (w_out[C_CONV:].reshape(N_HEADS, HEAD_DIM, D_MODEL).transpose(1, 0, 2)
               .reshape(C_RWKV, D_MODEL))
    pad_cols = ROUTE_COLS - N_EXPERT_GROUPS - N_EXPERTS
    w_route = jnp.concatenate([w_router_group, w_router_expert,
                               jnp.zeros((D_MODEL, pad_cols), F32)], axis=1)
    wr_both = jnp.concatenate(_split_bf16(w_route), axis=1)
    b_route = jnp.concatenate([b_router_group, b_router_expert, jnp.zeros((pad_cols,), F32)])
    return {
        "norm1_g": row(norm1_g), "w_in": w_in_perm.astype(BF16),
        "conv": (conv_w, row(conv_b), row(conv_ln_g), row(conv_ln_b)),
        "prep": (row(_shift_order(mu_shift)), key_vecs[0:1], wd_pad, key_vecs[1:2], wa_pad,
                 _value_order(w_gate_up).astype(BF16), key_vecs[2:3], key_vecs[3:4], key_vecs[4:5],
                 ones_bd),
        "outproj": (val_vecs[0:1], val_vecs[1:2], ones_bd, w_out[:C_CONV].astype(BF16),
                    w_out_r.astype(BF16), row(norm2_g), wr_both, row(b_route)),
        "w_gate": w_exp_gate.reshape(N_EXPERTS, D_MODEL, D_EXPERT),
        "w_up": w_exp_up.reshape(N_EXPERTS, D_MODEL, D_EXPERT),
        "w_down": w_exp_down.reshape(N_EXPERTS, D_EXPERT, D_MODEL),
        "norm_f_g": row(norm_f_g),
    }


def kernel(x_prompt, x_sample, state_conv, state_shift, state_wkv, norm1_g, w_in, conv_w, conv_b, conv_ln_g, conv_ln_b, mu_shift, w0, w_decay_up, a0, w_aaa_up, w_gate_up, k_k, k_a, r_k, gn_g, gn_b, w_out, norm2_g, w_router_group, b_router_group, w_router_expert, b_router_expert, w_exp_gate, w_exp_up, w_exp_down, norm_f_g):
    depth = norm1_g.shape[0]
    assert depth == 1, "single-layer trunk"
    b = x_prompt.shape[0]
    assert b == WKV_BATCH and x_prompt.shape[1] % LOAD_ROWS == 0
    assert x_sample.shape[0] * x_sample.shape[1] == WKV_BATCH * LOAD_ROWS
    lw = _prepare_weights(norm1_g[0], w_in[0], conv_w[0], conv_b[0], conv_ln_g[0], conv_ln_b[0],
                          mu_shift[0], w0[0], w_decay_up[0], a0[0], w_aaa_up[0], w_gate_up[0],
                          k_k[0], k_a[0], r_k[0], gn_g[0], gn_b[0], w_out[0], norm2_g[0],
                          w_router_group[0], b_router_group[0], w_router_expert[0],
                          b_router_expert[0], w_exp_gate[0], w_exp_up[0], w_exp_down[0], norm_f_g)
    dt = x_prompt.dtype
    tm = 512
    moe_p, conv_p, shift_p, wkv_p = _layer(
        x_prompt,
        jnp.zeros((b, CONV_WIDTH - 1, C_CONV), dt),
        jnp.zeros((b, D_SHIFT), dt),
        jnp.zeros((b, N_HEADS, HEAD_DIM, HEAD_DIM), dt), lw,
        tm=tm, conv_rows=128, prep_rows=256, wkv_tblk=LOAD_ROWS)
    moe_s, conv_s, shift_s, wkv_s = _layer(
        x_sample, state_conv[0], state_shift[0], state_wkv[0], lw,
        tm=tm, conv_rows=x_sample.shape[1], prep_rows=256, wkv_tblk=x_sample.shape[1])
    hs, xns, routes, cnts = zip(moe_p, moe_s)
    y_p, y_s = _moe(xns, routes, cnts, hs, lw["w_gate"], lw["w_up"], lw["w_down"],
                    lw["norm_f_g"], tm)
    y_p = y_p.reshape(x_prompt.shape)
    y_s = y_s.reshape(x_sample.shape)
    return (y_p, y_s, conv_p[None], shift_p[None], wkv_p[None],
            conv_s[None], shift_s[None], wkv_s[None])
```

```python
import functools

import jax
import jax.numpy as jnp
from jax import lax
from jax.experimental import pallas as pl
from jax.experimental.pallas import tpu as pltpu

F32 = jnp.float32
BF16 = jnp.bfloat16

D_MODEL = 1024
C_CONV = 512
C_RWKV = 512
HEAD_DIM = 64
N_HEADS = 8
CONV_WIDTH = 31
D_SHIFT = 1792
N_EXPERT_GROUPS = 4
EXPERTS_PER_GROUP = 8
N_EXPERTS = N_EXPERT_GROUPS * EXPERTS_PER_GROUP
D_EXPERT = 256
RMS_EPS = 1e-6
LN_EPS = 1e-5
GN_EPS = 64e-5

LANES = 128
SUBLANES = 8
CONV_PAD = 32
WKV_BATCH = 8
BH_PER_GROUP = WKV_BATCH * N_HEADS
K_HALF = HEAD_DIM // 2
WKV_PARTS = 2
LOAD_ROWS = LANES
VAL_PITCH = HEAD_DIM + SUBLANES
ROUTE_COLS = LANES
EXPERT_COL0 = N_EXPERT_GROUPS
MOE_ROWS = 512
DISPATCH_TILE = 1024
DMA_ISSUE_UNROLL = 8
VMEM_LIMIT = 56 * 1024 * 1024


def _reorder_last(x, split, order):
    lead = x.shape[:-1]
    k = len(lead)
    x = x.reshape(lead + split).transpose(tuple(range(k)) + tuple(k + o for o in order))
    return x.reshape(lead + (-1,))


def _key_order(x):
    return _reorder_last(x, (N_HEADS, 2, K_HALF), (2, 1, 0))


def _value_order(x):
    return _reorder_last(x, (N_HEADS, HEAD_DIM), (1, 0))


def _shift_order(x, inverse=False):
    if inverse:
        key = lambda a: _reorder_last(a, (K_HALF, 2, N_HEADS), (2, 1, 0))
        val = lambda a: _reorder_last(a, (HEAD_DIM, N_HEADS), (1, 0))
    else:
        key, val = _key_order, _value_order
    return jnp.concatenate([key(x[..., :C_RWKV]), key(x[..., C_RWKV:2 * C_RWKV]),
                            val(x[..., 2 * C_RWKV:3 * C_RWKV]), x[..., 3 * C_RWKV:]], axis=-1)


def _cparams(*sem):
    return pltpu.CompilerParams(dimension_semantics=sem, vmem_limit_bytes=VMEM_LIMIT)


def _split_bf16(x):
    hi = x.astype(BF16)
    lo = (x - hi.astype(F32)).astype(BF16)
    return hi, lo


def _segsum(x, same_head):
    hi, lo = _split_bf16(x)
    s = (jnp.dot(hi, same_head, preferred_element_type=F32)
         + jnp.dot(lo, same_head, preferred_element_type=F32))
    return jnp.concatenate([s] * (x.shape[1] // LANES), axis=1)


def _conv_rows(prev, row0, win, sh, hs_ref, w_ref, b_ref, lg_ref, lb_ref, o_ref, *, rows, sub_rows):
    shifted_rows = rows + CONV_PAD - SUBLANES
    win[pl.ds(0, CONV_PAD), :] = prev
    win[pl.ds(CONV_PAD, rows), :] = hs_ref[pl.ds(row0, rows), :]
    sh[0] = win[...]
    for sft in range(1, SUBLANES):
        sh[sft, pl.ds(0, shifted_rows), :] = win[pl.ds(sft, shifted_rows), :]
    for blk in range(rows // sub_rows):
        r0 = blk * sub_rows
        acc = jnp.zeros((sub_rows, C_CONV), F32) + b_ref[...]
        for j in range(CONV_WIDTH):
            off = j + 2
            acc = acc + (sh[off % SUBLANES, pl.ds(off - off % SUBLANES + r0, sub_rows), :]
                         * w_ref[pl.ds(j, 1), :])
        mu = jnp.mean(acc, axis=-1, keepdims=True)
        d = acc - mu
        var = jnp.mean(d * d, axis=-1, keepdims=True)
        of = d * lax.rsqrt(var + LN_EPS) * lg_ref[...] + lb_ref[...]
        o_ref[pl.ds(row0 + r0, sub_rows), :] = of * jax.nn.sigmoid(of)


def _front_kernel(x_ref, g1_ref, w_in_ref, bnd_ref, hist_ref, cw_ref, cb_ref, clg_ref, clb_ref,
                  mu_ref, w0_ref, wd_ref, a0_ref, wa_ref, wg_ref, kk_ref, ka_ref, rk_ref, ones_ref,
                  h_ref, conv_ref, zlast_ref, r_ref, w_ref, k_ref, v_ref, nkk_ref, b_ref, bonus_ref,
                  g_ref,
                  carry_ref, z_scr, hcarry_ref, hs_ref, cwin_ref, csh_ref,
                  *, period, tiles_per_seq, conv_rows):
    x = x_ref[...]
    xn = x * lax.rsqrt(jnp.mean(x * x, axis=-1, keepdims=True) + RMS_EPS) * g1_ref[...]
    p = jnp.dot(xn.astype(BF16), w_in_ref[...], preferred_element_type=F32)
    h = p[:, :C_CONV] * jax.nn.sigmoid(p[:, C_CONV:2 * C_CONV])
    h_ref[...] = h
    hs_ref[...] = h
    z = p[:, 2 * C_CONV:]
    rows = z.shape[0]
    first_tile = pl.program_id(0) % tiles_per_seq == 0

    conv = functools.partial(_conv_rows, hs_ref=hs_ref, w_ref=cw_ref, b_ref=cb_ref, lg_ref=clg_ref,
                             lb_ref=clb_ref, o_ref=conv_ref, rows=conv_rows,
                             sub_rows=min(conv_rows, 32))
    n_slots = cwin_ref.shape[0]
    if period == rows:
        hist = hist_ref[0]
        if tiles_per_seq > 1:
            hist = jnp.where(first_tile, hist, hcarry_ref[...])
            hcarry_ref[...] = h[rows - CONV_PAD:, :]
        for c in range(rows // conv_rows):
            prev = hist if c == 0 else h[c * conv_rows - CONV_PAD:c * conv_rows, :]
            conv(prev, c * conv_rows, cwin_ref.at[c % n_slots], csh_ref.at[c % n_slots])
    else:
        def seq_group(i, _):
            for u in range(n_slots):
                s = i * n_slots + u
                conv(hist_ref[s], pl.multiple_of(s * period, SUBLANES), cwin_ref.at[u], csh_ref.at[u])
            return 0
        lax.fori_loop(0, rows // period // n_slots, seq_group, 0)

    row_id = lax.broadcasted_iota(jnp.int32, z.shape, 0)
    bnd = bnd_ref[...]
    if tiles_per_seq > 1:
        bnd = jnp.where(first_tile, bnd, carry_ref[pl.ds(SUBLANES - 1, 1), :])
        carry_ref[...] = z[rows - SUBLANES:, :]
    if period == rows:
        zlast_ref[...] = z[rows - 1:, :]
    else:
        for c in range(z_scr.shape[0]):
            cols = slice(c * LANES, (c + 1) * LANES)
            z_scr[c] = z[:, cols]
            zlast_ref[:, cols] = z_scr[c, pl.ds(period - 1, rows // period, stride=period), :]
    z_prev = jnp.where(row_id % period == 0, bnd, pltpu.roll(z, 1, axis=0))
    zs = z + (z_prev - z) * mu_ref[...]
    r = zs[:, :C_RWKV]
    k = zs[:, C_RWKV:2 * C_RWKV]
    v = zs[:, 2 * C_RWKV:3 * C_RWKV]
    wa = zs[:, 3 * C_RWKV:3 * C_RWKV + LANES]
    gl = zs[:, 3 * C_RWKV + LANES:]
    ones_bd = ones_ref[...]

    lw = w0_ref[...] + jnp.dot(jnp.tanh(wa).astype(BF16), wd_ref[...], preferred_element_type=F32)
    w_log = -(jnp.maximum(-lw, 0.0) + jnp.log(1.0 + jnp.exp(-jnp.abs(lw)))) - 0.5
    decay = jnp.exp(-jnp.exp(w_log))
    a = jax.nn.sigmoid(a0_ref[...] + jnp.dot(wa.astype(BF16), wa_ref[...],
                                             preferred_element_type=F32))
    g = jnp.dot(jax.nn.sigmoid(gl).astype(BF16), wg_ref[...], preferred_element_type=F32)

    kk = k * kk_ref[...]
    norm = jnp.maximum(jnp.sqrt(_segsum(kk * kk, ones_bd)), 1e-12)
    kk = kk / norm
    k_mod = k * (1.0 + (a - 1.0) * ka_ref[...])
    bonus = _segsum(r * k_mod * rk_ref[...], ones_bd) * v

    def store_cols(ref, x):
        xt = x.T
        ntok = ref.shape[2]
        for q in range(ref.shape[0]):
            ref[q] = xt[:, q * ntok:(q + 1) * ntok]

    store_cols(r_ref, r)
    store_cols(w_ref, decay)
    store_cols(k_ref, k_mod)
    store_cols(v_ref, v)
    store_cols(nkk_ref, -kk)
    store_cols(b_ref, kk * a)
    bonus_ref[...] = bonus
    g_ref[...] = g


def _col_major_spec(rows, length):
    if rows <= length:
        per_b = length // rows
        return pl.BlockSpec((1, C_RWKV, rows), lambda i: (i // per_b, 0, i % per_b))
    return pl.BlockSpec((rows // length, C_RWKV, length), lambda i: (i, 0, 0))


def _front(x2d, g1, w_in, bnd, hist, conv_params, params, rows, period, tiles_per_seq, conv_rows):
    n = x2d.shape[0]
    length = n // WKV_BATCH
    (mu, w0, wd_pad, a0, wa_pad, wg, k_k, k_a, r_k, ones_bd) = params
    kern = functools.partial(_front_kernel, period=period, tiles_per_seq=tiles_per_seq,
                             conv_rows=conv_rows)
    vec = lambda c: pl.BlockSpec((1, c), lambda i: (0, 0))
    mat = lambda a: pl.BlockSpec(a.shape, lambda i: (0, 0))
    row = lambda c: pl.BlockSpec((rows, c), lambda i: (i, 0))
    row_sd = lambda c: jax.ShapeDtypeStruct((n, c), F32)
    col_spec = _col_major_spec(rows, length)
    col_sd = jax.ShapeDtypeStruct((WKV_BATCH, C_RWKV, length), F32)
    n_last = rows // period
    n_slots = rows // conv_rows if period == rows else max(1, min(n_last, 32 // period))
    return pl.pallas_call(
        kern,
        grid=(n // rows,),
        in_specs=[row(D_MODEL), vec(D_MODEL), mat(w_in),
                  pl.BlockSpec((None, bnd.shape[1], D_SHIFT), lambda i: (i // tiles_per_seq, 0, 0)),
                  pl.BlockSpec((n_last, CONV_PAD, C_CONV), lambda i: (i // tiles_per_seq, 0, 0)),
                  pl.BlockSpec((CONV_WIDTH, C_CONV), lambda i: (0, 0)),
                  vec(C_CONV), vec(C_CONV), vec(C_CONV),
                  vec(D_SHIFT), vec(C_RWKV), mat(wd_pad), vec(C_RWKV), mat(wa_pad), mat(wg),
                  vec(C_RWKV), vec(C_RWKV), vec(C_RWKV), mat(ones_bd)],
        out_specs=[row(C_CONV), row(C_CONV),
                   pl.BlockSpec((None, n_last, D_SHIFT), lambda i: (i, 0, 0))]
                  + [col_spec] * 6 + [row(C_RWKV)] * 2,
        out_shape=[row_sd(C_CONV), row_sd(C_CONV),
                   jax.ShapeDtypeStruct((n // rows, n_last, D_SHIFT), F32)]
                  + [col_sd] * 6 + [row_sd(C_RWKV)] * 2,
        scratch_shapes=[pltpu.VMEM((SUBLANES, D_SHIFT), F32),
                        pltpu.VMEM((D_SHIFT // LANES, rows, LANES), F32),
                        pltpu.VMEM((CONV_PAD, C_CONV), F32),
                        pltpu.VMEM((rows, C_CONV), F32),
                        pltpu.VMEM((n_slots, conv_rows + CONV_PAD, C_CONV), F32),
                        pltpu.VMEM((n_slots, SUBLANES, conv_rows + CONV_PAD, C_CONV), F32)],
        compiler_params=_cparams("arbitrary"),
        name="front",
    )(x2d, g1, w_in, bnd, hist, *conv_params, mu, w0, wd_pad, a0, wa_pad, wg, k_k, k_a, r_k, ones_bd)


def _fold_halves(x):
    return x + pltpu.roll(x, BH_PER_GROUP, axis=x.ndim - 1)


def _key_cols_to_lanes(src_ref, dst_ref):
    chunk = 2 * N_HEADS
    for kp in range(K_HALF):
        pieces = [src_ref[b8, pl.ds(kp * chunk + half * N_HEADS, N_HEADS), :]
                  for half in range(2) for b8 in range(WKV_BATCH)]
        dst_ref[kp, :LOAD_ROWS, :] = jnp.concatenate(pieces, axis=0).T


def _value_cols_to_lanes(src_ref, dst_ref):
    for v in range(HEAD_DIM):
        pieces = [src_ref[b8, pl.ds(v * N_HEADS, N_HEADS), :]
                  for _ in range(2) for b8 in range(WKV_BATCH)]
        dst_ref[pl.ds(v, LOAD_ROWS, stride=VAL_PITCH), :] = jnp.concatenate(pieces, axis=0).T


def _lanes_to_value_cols(y_s, yt_ref):
    for v in range(HEAD_DIM):
        yt = y_s[pl.ds(v, LOAD_ROWS, stride=VAL_PITCH), :].T
        folded = yt[:BH_PER_GROUP] + yt[BH_PER_GROUP:]
        for b8 in range(WKV_BATCH):
            yt_ref[b8, pl.ds(v * N_HEADS, N_HEADS), :] = folded[b8 * N_HEADS:(b8 + 1) * N_HEADS]


def _wkv_kernel(w_ref, nkk_ref, b_ref, k_ref, r_ref, v_ref, s0_ref, y_ref, sout_ref,
                w_s, nkk_s, b_s, k_s, r_s, v_s, y_s, s_ref,
                *, t_blk, steps_per_load, carry_state, n_parts):
    j = pl.program_id(0)
    sub = j % steps_per_load
    part = HEAD_DIM // n_parts
    parts = [slice(i * part, (i + 1) * part) for i in range(n_parts)]

    @pl.when(sub == 0)
    def _():
        for src, dst in ((w_ref, w_s), (nkk_ref, nkk_s), (b_ref, b_s), (k_ref, k_s), (r_ref, r_s)):
            _key_cols_to_lanes(src, dst)
        _value_cols_to_lanes(v_ref, v_s)

    if carry_state:
        @pl.when(j == 0)
        def _():
            s_ref[...] = s0_ref[0]
    else:
        s_ref[...] = s0_ref[0]

    tok0 = sub * t_blk
    acc = jnp.zeros((HEAD_DIM, LANES), F32)
    for kp in range(K_HALF):
        acc = acc + s_ref[kp] * nkk_s[kp, pl.ds(tok0, 1), :]
    sa_init = tuple(_fold_halves(acc[p, :]) for p in parts[:-1]) + (acc[parts[-1], :],)

    def part_step(tok, nxt_tok, rows, sa):
        cur = pl.ds(tok, 1)
        vrows = pl.ds(pl.multiple_of(tok * VAL_PITCH, SUBLANES) + rows.start, part)
        vv = v_s[vrows, :]
        yacc = jnp.zeros((part, LANES), F32)
        sacc = jnp.zeros((part, LANES), F32)
        for kp in range(K_HALF):
            s = (s_ref[kp, rows, :] * w_s[kp, cur, :] + sa * b_s[kp, cur, :]
                 + vv * k_s[kp, cur, :])
            s_ref[kp, rows, :] = s
            yacc = yacc + s * r_s[kp, cur, :]
            sacc = sacc + s * nkk_s[kp, pl.ds(nxt_tok, 1), :]
        y_s[vrows, :] = yacc
        return sacc

    def step(t, sa_parts):
        tok = tok0 + t
        nxt_tok = tok0 + jnp.minimum(t + 1, t_blk - 1)
        sa_last = _fold_halves(sa_parts[-1])
        nxt = [_fold_halves(part_step(tok, nxt_tok, rows, sa))
               for rows, sa in zip(parts[:-1], sa_parts[:-1])]
        return tuple(nxt) + (part_step(tok, nxt_tok, parts[-1], sa_last),)

    lax.fori_loop(0, t_blk, step, sa_init)

    if carry_state:
        @pl.when(j == pl.num_programs(0) - 1)
        def _():
            sout_ref[0] = s_ref[...]
    else:
        sout_ref[0] = s_ref[...]

    @pl.when(sub == steps_per_load - 1)
    def _():
        _lanes_to_value_cols(y_s, y_ref)


def _wkv(w, nkk, b, k, r, v, s0, *, t_blk, carry_state):
    length = w.shape[2]
    steps_per_load = LOAD_ROWS // t_blk
    n_steps = length // t_blk
    tok_spec = pl.BlockSpec((WKV_BATCH, C_RWKV, LOAD_ROWS), lambda j: (0, 0, j // steps_per_load))
    state_blk = (1, K_HALF, HEAD_DIM, LANES)
    if carry_state:
        sspec = pl.BlockSpec(state_blk, lambda j: (0, 0, 0, 0))
    else:
        sspec = pl.BlockSpec(state_blk, lambda j: (j, 0, 0, 0))
    key_rows = pltpu.VMEM((K_HALF, LOAD_ROWS + SUBLANES, LANES), F32)
    val_rows = pltpu.VMEM((LOAD_ROWS * VAL_PITCH, LANES), F32)
    return pl.pallas_call(
        functools.partial(_wkv_kernel, t_blk=t_blk, steps_per_load=steps_per_load,
                          carry_state=carry_state, n_parts=WKV_PARTS),
        grid=(n_steps,),
        in_specs=[tok_spec] * 6 + [sspec],
        out_specs=[tok_spec, sspec],
        out_shape=[jax.ShapeDtypeStruct(w.shape, F32), jax.ShapeDtypeStruct(s0.shape, F32)],
        scratch_shapes=[key_rows] * 5 + [val_rows, val_rows,
                                         pltpu.VMEM((K_HALF, HEAD_DIM, LANES), F32)],
        compiler_params=_cparams("arbitrary"),
        name="wkv",
    )(w, nkk, b, k, r, v, s0)


def _state_to_lanes(s, n_grp):
    s = s.reshape(WKV_BATCH, n_grp, N_HEADS, HEAD_DIM, 2, K_HALF)
    return s.transpose(1, 5, 3, 4, 0, 2).reshape(n_grp, K_HALF, HEAD_DIM, LANES)


def _state_from_lanes(s, n_grp):
    s = s.reshape(n_grp, K_HALF, HEAD_DIM, 2, WKV_BATCH, N_HEADS)
    return s.transpose(4, 0, 5, 2, 3, 1).reshape(n_grp * WKV_BATCH, N_HEADS, HEAD_DIM, HEAD_DIM)


def _outproj_kernel(x_ref, c_ref, y_ref, bonus_ref, g_ref, gng_ref, gnb_ref, ones_ref,
                    wo_c_ref, wo_r_ref, n2_ref, wr_both_ref, br_ref,
                    h_ref, xn_ref, route_ref, cnt_ref):
    ones_bd = ones_ref[...]
    y = jnp.concatenate([y_ref[q].T for q in range(y_ref.shape[0])], axis=0)
    inv_n = 1.0 / HEAD_DIM
    mu = _segsum(y, ones_bd) * inv_n
    d = y - mu
    var = _segsum(d * d, ones_bd) * inv_n
    yn = d * lax.rsqrt(var + GN_EPS) * gng_ref[...] + gnb_ref[...]
    rw = (yn + bonus_ref[...]) * g_ref[...]
    h = (x_ref[...]
         + jnp.dot(c_ref[...].astype(BF16), wo_c_ref[...], preferred_element_type=F32)
         + jnp.dot(rw.astype(BF16), wo_r_ref[...], preferred_element_type=F32))
    h_ref[...] = h
    xn = h * lax.rsqrt(jnp.mean(h * h, axis=-1, keepdims=True) + RMS_EPS) * n2_ref[...]
    xn_ref[...] = xn

    x_hi, x_lo = _split_bf16(xn)
    hi_both = jnp.dot(x_hi, wr_both_ref[...], preferred_element_type=F32)
    logits = (hi_both[:, :ROUTE_COLS]
              + jnp.dot(x_lo, wr_both_ref[:, :ROUTE_COLS], preferred_element_type=F32)
              + hi_both[:, ROUTE_COLS:]) + br_ref[...]

    col = lax.broadcasted_iota(jnp.int32, logits.shape, 1)
    neg = jnp.float32(-jnp.inf)
    big = jnp.int32(ROUTE_COLS)
    is_g = col < N_EXPERT_GROUPS
    g_logit = jnp.where(is_g, logits, neg)
    g_max = jnp.max(g_logit, axis=-1, keepdims=True)
    g_idx = jnp.min(jnp.where(g_logit == g_max, col, big), axis=-1, keepdims=True)
    p_group = 1.0 / jnp.sum(jnp.where(is_g, jnp.exp(logits - g_max), 0.0), axis=-1, keepdims=True)
    lo_col = EXPERT_COL0 + g_idx * EXPERTS_PER_GROUP
    in_grp = (col >= lo_col) & (col < lo_col + EXPERTS_PER_GROUP)
    e1 = jnp.where(in_grp, logits, neg)
    m1 = jnp.max(e1, axis=-1, keepdims=True)
    i1 = jnp.min(jnp.where(e1 == m1, col, big), axis=-1, keepdims=True)
    e2 = jnp.where(col == i1, neg, e1)
    m2 = jnp.max(e2, axis=-1, keepdims=True)
    i2 = jnp.min(jnp.where(e2 == m2, col, big), axis=-1, keepdims=True)
    ex = jnp.exp(m2 - m1)
    w1 = p_group / (1.0 + ex)
    w2 = p_group * ex / (1.0 + ex)

    exp1 = i1 - EXPERT_COL0
    exp2 = i2 - EXPERT_COL0
    oh1 = jnp.where(col == exp1, 1.0, 0.0)
    oh2 = jnp.where(col == exp2, 1.0, 0.0)
    tm = logits.shape[0]
    earlier = jnp.where(lax.broadcasted_iota(jnp.int32, (tm, tm), 0)
                        > lax.broadcasted_iota(jnp.int32, (tm, tm), 1), 1.0, 0.0).astype(BF16)
    before1 = jnp.dot(earlier, oh1.astype(BF16), preferred_element_type=F32)
    before2 = jnp.dot(earlier, oh2.astype(BF16), preferred_element_type=F32)
    cnt1 = jnp.sum(oh1, axis=0, keepdims=True)
    cnt2 = jnp.sum(oh2, axis=0, keepdims=True)
    rank1 = jnp.sum(before1 * oh1, axis=-1, keepdims=True)
    rank2 = jnp.sum((before2 + cnt1) * oh2, axis=-1, keepdims=True)
    fields = (exp1.astype(F32), exp2.astype(F32), w1, w2, rank1, rank2)
    route = jnp.zeros_like(logits)
    for c, val in enumerate(fields):
        route = jnp.where(col == c, val, route)
    route_ref[...] = route
    cnt_ref[...] = jnp.broadcast_to(cnt1 + cnt2, cnt_ref.shape)


def _outproj(x, conv_out, y, bonus, g, params, tm):
    n = x.shape[0]
    (gn_g, gn_b, ones_bd, wo_c, wo_r, n2_g, wr_both, br) = params
    row = lambda c: pl.BlockSpec((tm, c), lambda i: (i, 0))
    vec = lambda c: pl.BlockSpec((1, c), lambda i: (0, 0))
    mat = lambda a: pl.BlockSpec(a.shape, lambda i: (0, 0))
    return pl.pallas_call(
        _outproj_kernel,
        grid=(n // tm,),
        in_specs=[row(D_MODEL), row(C_CONV), _col_major_spec(tm, n // WKV_BATCH), row(C_RWKV),
                  row(C_RWKV),
                  vec(C_RWKV), vec(C_RWKV), mat(ones_bd), mat(wo_c), mat(wo_r), vec(D_MODEL),
                  mat(wr_both), vec(ROUTE_COLS)],
        out_specs=[row(D_MODEL), row(D_MODEL), row(ROUTE_COLS),
                   pl.BlockSpec((None, SUBLANES, ROUTE_COLS), lambda i: (i, 0, 0))],
        out_shape=[jax.ShapeDtypeStruct((n, D_MODEL), F32),
                   jax.ShapeDtypeStruct((n, D_MODEL), F32),
                   jax.ShapeDtypeStruct((n, ROUTE_COLS), F32),
                   jax.ShapeDtypeStruct((n // tm, SUBLANES, ROUTE_COLS), F32)],
        compiler_params=_cparams("parallel"),
        name="outproj_route",
    )(x, conv_out, y, bonus, g, gn_g, gn_b, ones_bd, wo_c, wo_r, n2_g, wr_both, br)


def _moe_plan(route, cnt, tm, moe_rows):
    n = route.shape[0]
    ids = route[:, 0:2].astype(jnp.int32)
    ranks = route[:, 4:6].astype(jnp.int32)
    tile_cnt = cnt[:, 0, :N_EXPERTS].astype(jnp.int32)
    before_tile = jnp.cumsum(tile_cnt, axis=0) - tile_cnt
    total = jnp.sum(tile_cnt, axis=0)
    tiles_e = (total + moe_rows - 1) // moe_rows
    tile_end = jnp.cumsum(tiles_e)
    row_start = (tile_end - tiles_e) * moe_rows
    first_row = (row_start[None, :] + before_tile)[:, None, None, :]
    one_hot = ids.reshape(-1, tm, 2, 1) == jnp.arange(N_EXPERTS, dtype=jnp.int32)
    pos = jnp.sum(jnp.where(one_hot, first_row, 0), axis=-1).reshape(n, 2) + ranks
    n_row_tiles = 2 * n // moe_rows + N_EXPERTS
    t = jnp.arange(n_row_tiles, dtype=jnp.int32)
    tile_expert = jnp.minimum(jnp.sum((t[:, None] >= tile_end[None, :]).astype(jnp.int32), axis=1),
                              N_EXPERTS - 1)
    tile_used = (t < tile_end[-1]).astype(jnp.int32)
    last_tile = jnp.maximum(tile_end[-1:] - 1, 0).astype(jnp.int32)
    is_last = jnp.any((t[:, None] == tile_end[None, :] - 1) & (tiles_e[None, :] > 0), axis=1)
    tile_fill = (is_last | (tile_used == 0)).astype(jnp.int32)
    return pos.reshape(-1), tile_fill, tile_expert.astype(jnp.int32), tile_used, last_tile


def _row_copy_wait(src_ref, dst_ref, sem, rows):
    pltpu.make_async_copy(src_ref.at[pl.ds(0, rows), :], dst_ref.at[pl.ds(0, rows), :], sem).wait()


def _dispatch_kernel(pos_ref, fill_ref, *refs, tm, group_tiles):
    x_refs, (xs_ref, zero_ref, sem) = refs[:len(group_tiles)], refs[len(group_tiles):]
    base = pl.program_id(0) * (2 * tm)
    fill_rows = zero_ref.shape[0]

    @pl.when(pl.program_id(0) == 0)
    def _():
        zero_ref[...] = jnp.zeros_like(zero_ref)
        n_row_tiles = xs_ref.shape[0] // fill_rows

        def fill(t, _):
            @pl.when(fill_ref[t] != 0)
            def _():
                start = pl.multiple_of(t * fill_rows, fill_rows)
                pltpu.make_async_copy(zero_ref, xs_ref.at[pl.ds(start, fill_rows), :], sem).start()
            return 0
        lax.fori_loop(0, n_row_tiles, fill, 0)

        def fill_wait(t, _):
            @pl.when(fill_ref[t] != 0)
            def _():
                pltpu.make_async_copy(zero_ref, xs_ref.at[pl.ds(0, fill_rows), :], sem).wait()
            return 0
        lax.fori_loop(0, n_row_tiles, fill_wait, 0)

    for x_ref, in_group in zip(x_refs, _group_preds(pl.program_id(0), group_tiles)):
        @pl.when(in_group)
        def _(x_ref=x_ref):
            def issue(r, _):
                src = x_ref.at[pl.ds(r, 1), :]
                for c in range(2):
                    dst = xs_ref.at[pl.ds(pos_ref[base + 2 * r + c], 1), :]
                    pltpu.make_async_copy(src, dst, sem).start()
                return 0
            lax.fori_loop(0, tm, issue, 0, unroll=DMA_ISSUE_UNROLL)
    for _ in range(2):
        _row_copy_wait(x_refs[0], xs_ref, sem, tm)


def _group_preds(i, group_tiles):
    preds, start = [], 0
    for tiles in group_tiles:
        preds.append((i >= start) & (i < start + tiles))
        start += tiles
    return preds


def _group_specs(block, group_tiles):
    specs, start = [], 0
    for tiles in group_tiles:
        specs.append(pl.BlockSpec(
            block, lambda i, *_, start=start, tiles=tiles: (jnp.clip(i - start, 0, tiles - 1), 0)))
        start += tiles
    return specs


def _dispatch(pos, tile_fill, xns, n_rows, moe_rows, tm):
    group_tiles = tuple(x.shape[0] // tm for x in xns)
    return pl.pallas_call(
        functools.partial(_dispatch_kernel, tm=tm, group_tiles=group_tiles),
        grid_spec=pltpu.PrefetchScalarGridSpec(
            num_scalar_prefetch=2,
            grid=(sum(group_tiles),),
            in_specs=_group_specs((tm, D_MODEL), group_tiles),
            out_specs=pl.BlockSpec(memory_space=pl.ANY),
            scratch_shapes=[pltpu.VMEM((moe_rows, D_MODEL), F32), pltpu.SemaphoreType.DMA(())]),
        out_shape=jax.ShapeDtypeStruct((n_rows, D_MODEL), F32),
        compiler_params=_cparams("arbitrary"),
        name="moe_dispatch",
    )(pos, tile_fill, *xns)


def _experts_kernel(expert_ref, used_ref, last_ref, xs_ref, wg_ref, wu_ref, wd_ref, y_ref,
                    wg_bf, wu_bf, wd_bf):
    del last_ref
    t = pl.program_id(0)

    @pl.when((t == 0) | (expert_ref[t] != expert_ref[jnp.maximum(t - 1, 0)]))
    def _():
        wg_bf[...] = wg_ref[0].astype(BF16)
        wu_bf[...] = wu_ref[0].astype(BF16)
        wd_bf[...] = wd_ref[0].astype(BF16)

    @pl.when(used_ref[t] != 0)
    def _():
        x = xs_ref[...].astype(BF16)
        hg = jnp.dot(x, wg_bf[...], preferred_element_type=F32)
        hu = jnp.dot(x, wu_bf[...], preferred_element_type=F32)
        act = hg * jax.nn.sigmoid(hg) * hu
        y_ref[...] = jnp.dot(act.astype(BF16), wd_bf[...], preferred_element_type=F32)

    @pl.when(used_ref[t] == 0)
    def _():
        y_ref[...] = jnp.zeros_like(y_ref)


def _experts(tile_expert, tile_used, last_tile, xs, w_gate, w_up, w_down):
    n_rows = xs.shape[0]
    moe_rows = n_rows // tile_expert.shape[0]
    row_tile = lambda t, ex, used, last: (jnp.where(used[t] != 0, t, last[0]), 0)
    weight = lambda t, ex, used, last: (ex[t], 0, 0)
    return pl.pallas_call(
        _experts_kernel,
        grid_spec=pltpu.PrefetchScalarGridSpec(
            num_scalar_prefetch=3,
            grid=(n_rows // moe_rows,),
            in_specs=[pl.BlockSpec((moe_rows, D_MODEL), row_tile),
                      pl.BlockSpec((1, D_MODEL, D_EXPERT), weight),
                      pl.BlockSpec((1, D_MODEL, D_EXPERT), weight),
                      pl.BlockSpec((1, D_EXPERT, D_MODEL), weight)],
            out_specs=pl.BlockSpec((moe_rows, D_MODEL), lambda t, ex, used, last: (t, 0)),
            scratch_shapes=[pltpu.VMEM((D_MODEL, D_EXPERT), BF16), pltpu.VMEM((D_MODEL, D_EXPERT), BF16),
                            pltpu.VMEM((D_EXPERT, D_MODEL), BF16)]),
        out_shape=jax.ShapeDtypeStruct((n_rows, D_MODEL), F32),
        compiler_params=_cparams("arbitrary"),
        name="moe_experts",
    )(tile_expert, tile_used, last_tile, xs, w_gate, w_up, w_down)


def _combine_kernel(pos_ref, *refs, tm, group_tiles):
    n_grp = len(group_tiles)
    h_refs, (route_ref, nf_ref, ys_ref) = refs[:n_grp], refs[n_grp:n_grp + 3]
    o_refs, (y_buf, sem) = refs[n_grp + 3:2 * n_grp + 3], refs[2 * n_grp + 3:]
    i = pl.program_id(0)
    last = pl.num_programs(0) - 1
    in_group = _group_preds(i, group_tiles)

    def issue_tile(base, slot):
        def issue(r, _):
            for c in range(2):
                src = ys_ref.at[pl.ds(pos_ref[base + 2 * r + c], 1), :]
                pltpu.make_async_copy(src, y_buf.at[slot, c, pl.ds(r, 1), :], sem.at[slot]).start()
            return 0
        lax.fori_loop(0, tm, issue, 0, unroll=DMA_ISSUE_UNROLL)

    def wait(slot):
        for c in range(2):
            _row_copy_wait(ys_ref, y_buf.at[slot, c], sem.at[slot], tm)

    @pl.when(i == 0)
    def _():
        issue_tile(0, 0)

    next_base = jnp.minimum(i + 1, last) * (2 * tm)

    def run(slot):
        other = 1 - slot
        issue_tile(next_base, other)
        wait(slot)
        route = route_ref[...]
        col = lax.broadcasted_iota(jnp.int32, route.shape, 1)
        w1 = jnp.sum(jnp.where(col == 2, route, 0.0), axis=-1, keepdims=True)
        w2 = jnp.sum(jnp.where(col == 3, route, 0.0), axis=-1, keepdims=True)
        moe = w1 * y_buf[slot, 0] + w2 * y_buf[slot, 1]
        for h_ref, o_ref, mine in zip(h_refs, o_refs, in_group):
            @pl.when(mine)
            def _(h_ref=h_ref, o_ref=o_ref):
                h = h_ref[...] + moe
                o_ref[...] = (h * lax.rsqrt(jnp.mean(h * h, axis=-1, keepdims=True) + RMS_EPS)
                              * nf_ref[...])

        @pl.when(i == last)
        def _():
            wait(other)

    for slot in range(2):
        pl.when(i % 2 == slot)(functools.partial(run, slot))


def _combine(pos, hs, route, nf_g, ys, tm):
    group_tiles = tuple(h.shape[0] // tm for h in hs)
    row_specs = _group_specs((tm, D_MODEL), group_tiles)
    return pl.pallas_call(
        functools.partial(_combine_kernel, tm=tm, group_tiles=group_tiles),
        grid_spec=pltpu.PrefetchScalarGridSpec(
            num_scalar_prefetch=1,
            grid=(sum(group_tiles),),
            in_specs=row_specs + [pl.BlockSpec((tm, ROUTE_COLS), lambda i, pos: (i, 0)),
                                  pl.BlockSpec((1, D_MODEL), lambda i, pos: (0, 0)),
                                  pl.BlockSpec(memory_space=pl.ANY)],
            out_specs=row_specs,
            scratch_shapes=[pltpu.VMEM((2, 2, tm, D_MODEL), F32), pltpu.SemaphoreType.DMA((2,))]),
        out_shape=[jax.ShapeDtypeStruct(h.shape, F32) for h in hs],
        compiler_params=_cparams("arbitrary"),
        name="moe_combine",
    )(pos, *hs, route, nf_g, ys)


def _moe(xns, routes, cnts, hs, w_gate, w_up, w_down, nf_g, tm):
    route = jnp.concatenate(routes, axis=0)
    cnt = jnp.concatenate(cnts, axis=0)
    moe_rows = min(MOE_ROWS, max(LANES, 2 * route.shape[0] // N_EXPERTS))
    pos, tile_fill, tile_expert, tile_used, last_tile = _moe_plan(route, cnt, tm, moe_rows)
    xs = _dispatch(pos, tile_fill, xns, tile_expert.shape[0] * moe_rows, moe_rows, DISPATCH_TILE)
    ys = _experts(tile_expert, tile_used, last_tile, xs, w_gate, w_up, w_down)
    return _combine(pos, hs, route, nf_g, ys, tm)


def _layer(x, conv_buf, shift_buf, wkv_state, lw, *, tm, conv_rows, prep_rows, wkv_tblk):
    b, t, _ = x.shape
    n = b * t
    n_grp = b // WKV_BATCH
    x2d = x.reshape(n, D_MODEL)

    shift_buf = _shift_order(shift_buf)
    if prep_rows > t:
        period, tiles_per_seq = t, 1
        bnd = jnp.repeat(shift_buf, t, axis=0).reshape(n // prep_rows, prep_rows, D_SHIFT)
    else:
        period, tiles_per_seq = prep_rows, t // prep_rows
        bnd = shift_buf.reshape(b, 1, D_SHIFT)
    hist = jnp.concatenate([jnp.zeros((b, CONV_PAD - CONV_WIDTH + 1, C_CONV), F32), conv_buf], axis=1)
    h_glu, conv_out, z_last, r, w, k, v, nkk, bb, bonus, g = _front(
        x2d, lw["norm1_g"], lw["w_in"], bnd, hist, lw["conv"], lw["prep"], prep_rows, period,
        tiles_per_seq, conv_rows)
    z_last = z_last.reshape(b, -1, D_SHIFT)[:, -1]
    new_shift = _shift_order(z_last, inverse=True)
    h3 = h_glu.reshape(b, t, C_CONV)
    n_hist = CONV_WIDTH - 1
    new_conv = h3[:, t - n_hist:] if t >= n_hist else jnp.concatenate([conv_buf[:, t:], h3], axis=1)

    y_cols, s_l = _wkv(w, nkk, bb, k, r, v, _state_to_lanes(wkv_state, n_grp),
                       t_blk=wkv_tblk, carry_state=(n_grp == 1))
    new_wkv = _state_from_lanes(s_l, n_grp)

    moe_in = _outproj(x2d, conv_out, y_cols, bonus, g, lw["outproj"], tm)
    return moe_in, new_conv, new_shift, new_wkv


def _prepare_weights(norm1_g, w_in, conv_w, conv_b, conv_ln_g, conv_ln_b, mu_shift, w0,
                     w_decay_up, a0, w_aaa_up, w_gate_up, k_k, k_a, r_k, gn_g, gn_b, w_out,
                     norm2_g, w_router_group, b_router_group, w_router_expert, b_router_expert,
                     w_exp_gate, w_exp_up, w_exp_down, norm_f_g):
    row = lambda a: a.reshape(1, -1)
    head = jnp.arange(C_RWKV) % N_HEADS
    ones_bd = (head[:, None] == head[None, :LANES]).astype(BF16)
    w_in_perm = jnp.concatenate([w_in[:, :2 * C_CONV], _shift_order(w_in[:, 2 * C_CONV:])], axis=1)
    zeros_lora = jnp.zeros((LANES - w_decay_up.shape[0], C_RWKV), F32)
    wd_pad = jnp.concatenate([_key_order(w_decay_up), zeros_lora], axis=0).astype(BF16)
    wa_pad = jnp.concatenate([zeros_lora, _key_order(w_aaa_up)], axis=0).astype(BF16)
    key_vecs = _key_order(jnp.stack([w0, a0, k_k, k_a, r_k]))
    val_vecs = _value_order(jnp.stack([gn_g, gn_b]))
    w_out_r = (w_out[C_CONV:].reshape(N_HEADS, HEAD_DIM, D_MODEL).transpose(1, 0, 2)
               .reshape(C_RWKV, D_MODEL))
    pad_cols = ROUTE_COLS - N_EXPERT_GROUPS - N_EXPERTS
    w_route = jnp.concatenate([w_router_group, w_router_expert,
                               jnp.zeros((D_MODEL, pad_cols), F32)], axis=1)
    wr_both = jnp.concatenate(_split_bf16(w_route), axis=1)
    b_route = jnp.concatenate([b_router_group, b_router_expert, jnp.zeros((pad_cols,), F32)])
    return {
        "norm1_g": row(norm1_g), "w_in": w_in_perm.astype(BF16),
        "conv": (conv_w, row(conv_b), row(conv_ln_g), row(conv_ln_b)),
        "prep": (row(_shift_order(mu_shift)), key_vecs[0:1], wd_pad, key_vecs[1:2], wa_pad,
                 _value_order(w_gate_up).astype(BF16), key_vecs[2:3], key_vecs[3:4], key_vecs[4:5],
                 ones_bd),
        "outproj": (val_vecs[0:1], val_vecs[1:2], ones_bd, w_out[:C_CONV].astype(BF16),
                    w_out_r.astype(BF16), row(norm2_g), wr_both, row(b_route)),
        "w_gate": w_exp_gate.reshape(N_EXPERTS, D_MODEL, D_EXPERT),
        "w_up": w_exp_up.reshape(N_EXPERTS, D_MODEL, D_EXPERT),
        "w_down": w_exp_down.reshape(N_EXPERTS, D_EXPERT, D_MODEL),
        "norm_f_g": row(norm_f_g),
    }


def kernel(x_prompt, x_sample, state_conv, state_shift, state_wkv, norm1_g, w_in, conv_w, conv_b, conv_ln_g, conv_ln_b, mu_shift, w0, w_decay_up, a0, w_aaa_up, w_gate_up, k_k, k_a, r_k, gn_g, gn_b, w_out, norm2_g, w_router_group, b_router_group, w_router_expert, b_router_expert, w_exp_gate, w_exp_up, w_exp_down, norm_f_g):
    depth = norm1_g.shape[0]
    assert depth == 1, "single-layer trunk"
    b = x_prompt.shape[0]
    assert b == WKV_BATCH and x_prompt.shape[1] % LOAD_ROWS == 0
    assert x_sample.shape[0] * x_sample.shape[1] == WKV_BATCH * LOAD_ROWS
    lw = _prepare_weights(norm1_g[0], w_in[0], conv_w[0], conv_b[0], conv_ln_g[0], conv_ln_b[0],
                          mu_shift[0], w0[0], w_decay_up[0], a0[0], w_aaa_up[0], w_gate_up[0],
                          k_k[0], k_a[0], r_k[0], gn_g[0], gn_b[0], w_out[0], norm2_g[0],
                          w_router_group[0], b_router_group[0], w_router_expert[0],
                          b_router_expert[0], w_exp_gate[0], w_exp_up[0], w_exp_down[0], norm_f_g)
    dt = x_prompt.dtype
    tm = 512
    moe_p, conv_p, shift_p, wkv_p = _layer(
        x_prompt,
        jnp.zeros((b, CONV_WIDTH - 1, C_CONV), dt),
        jnp.zeros((b, D_SHIFT), dt),
        jnp.zeros((b, N_HEADS, HEAD_DIM, HEAD_DIM), dt), lw,
        tm=tm, conv_rows=128, prep_rows=256, wkv_tblk=LOAD_ROWS)
    moe_s, conv_s, shift_s, wkv_s = _layer(
        x_sample, state_conv[0], state_shift[0], state_wkv[0], lw,
        tm=tm, conv_rows=x_sample.shape[1], prep_rows=256, wkv_tblk=x_sample.shape[1])
    hs, xns, routes, cnts = zip(moe_p, moe_s)
    y_p, y_s = _moe(xns, routes, cnts, hs, lw["w_gate"], lw["w_up"], lw["w_down"],
                    lw["norm_f_g"], tm)
    y_p = y_p.reshape(x_prompt.shape)
    y_s = y_s.reshape(x_sample.shape)
    return (y_p, y_s, conv_p[None], shift_p[None], wkv_p[None],
            conv_s[None], shift_s[None], wkv_s[None])
```
